```python
import math
import jax, jax.numpy as jnp
from jax import lax
import numpy as np

D_MODEL = 1024
BATCH = 8
SEQ = 4096
DEPTH = 4

GRID_W = 64
CTX_LEN = 256
N_MIXERS = 3
HEAD_DIM = 64
MIX_WIDTH = D_MODEL
A_HEADS = 16
A_KV_HEADS = 4
A_GROUP = A_HEADS // A_KV_HEADS
A_IN = A_HEADS * HEAD_DIM + 2 * A_KV_HEADS * HEAD_DIM + MIX_WIDTH
B_HEADS = MIX_WIDTH // HEAD_DIM
WIN_H = 8
WIN_W = 16
B_IN = 3 * B_HEADS * HEAD_DIM + MIX_WIDTH
C_HEADS = MIX_WIDTH // (2 * HEAD_DIM)
C_V_DIM = 2 * HEAD_DIM
C_IN = 2 * C_HEADS * 2 * HEAD_DIM + C_HEADS * C_V_DIM + MIX_WIDTH
N_A = (DEPTH + 2) // 3
N_B = (DEPTH + 1) // 3
N_C = DEPTH // 3
Q_BLOCK = 128
QB_ROWS = Q_BLOCK // GRID_W
ROPE_BASE = 10000.0
EPS = 1e-6
NEG_INF = -1e30

kernel_name = "hybrid_dit_interleaved_gqa_na_diff"


def rms_norm(x, g):
    xf = x.astype(jnp.float32)
    y = xf * lax.rsqrt(jnp.mean(xf * xf, axis=-1, keepdims=True) + EPS)
    return (y * g.astype(jnp.float32)).astype(x.dtype)


def lambda_init_fn(layer):
    return 0.8 - 0.6 * math.exp(-0.3 * layer)


def axial_rope_angles(n_tokens):
    t = jnp.arange(n_tokens, dtype=jnp.int32)
    rows = (t // GRID_W).astype(jnp.float32)
    cols = (t % GRID_W).astype(jnp.float32)
    n_freq = HEAD_DIM // 4
    inv_freq = ROPE_BASE ** (-jnp.arange(n_freq, dtype=jnp.float32) / n_freq)
    return rows[:, None] * inv_freq, cols[:, None] * inv_freq


def _rotate(x, ang):
    x1, x2 = jnp.split(x, 2, axis=-1)
    shape = (1, ang.shape[0]) + (1,) * (x.ndim - 3) + (ang.shape[1],)
    cos = jnp.cos(ang).reshape(shape).astype(x.dtype)
    sin = jnp.sin(ang).reshape(shape).astype(x.dtype)
    return jnp.concatenate([x1 * cos - x2 * sin, x2 * cos + x1 * sin], axis=-1)


def axial_rope(x, ang):
    ang_r, ang_c = ang
    x_r, x_c = jnp.split(x, 2, axis=-1)
    return jnp.concatenate([_rotate(x_r, ang_r), _rotate(x_c, ang_c)], axis=-1)


def map_query_blocks(block_fn, q):
    b, s = q.shape[:2]
    n_blk = s // Q_BLOCK
    qb = q.reshape((b, n_blk, Q_BLOCK) + q.shape[2:]).swapaxes(0, 1)
    out = lax.map(lambda args: block_fn(*args), (jnp.arange(n_blk, dtype=jnp.int32), qb))
    return out.swapaxes(0, 1).reshape((b, s) + out.shape[3:])


def gqa_attention(q, k, v):
    scale = HEAD_DIM ** -0.5

    def block(j, qb):
        logits = jnp.einsum("bqhgd,bkhd->bhgqk", qb, k).astype(jnp.float32) * scale
        p = jax.nn.softmax(logits, axis=-1).astype(v.dtype)
        return jnp.einsum("bhgqk,bkhd->bqhgd", p, v)

    return map_query_blocks(block, q)


def neighbourhood_attention(q, k, v, kc, vc, rpb, rows):
    b, s, h, d = q.shape
    kh = min(WIN_H, rows)
    kb = min(kh + QB_ROWS - 1, rows)
    scale = HEAD_DIM ** -0.5
    k_grid = k.reshape(b, rows, GRID_W, h, d)
    v_grid = v.reshape(b, rows, GRID_W, h, d)
    col = jnp.arange(GRID_W, dtype=jnp.int32)
    col_start = jnp.clip(col - WIN_W // 2, 0, GRID_W - WIN_W)
    q_col = jnp.tile(col, QB_ROWS)[:, None]
    q_cs = jnp.tile(col_start, QB_ROWS)[:, None]
    k_col = jnp.tile(col, kb)[None, :]

    def block(j, qb):
        q_rows = j * QB_ROWS + jnp.arange(QB_ROWS, dtype=jnp.int32)
        row_start = jnp.clip(q_rows - kh // 2, 0, rows - kh)
        band = jnp.clip(row_start[0], 0, rows - kb)
        k_band = lax.dynamic_slice_in_dim(k_grid, band, kb, axis=1).reshape(b, kb * GRID_W, h, d)
        v_band = lax.dynamic_slice_in_dim(v_grid, band, kb, axis=1).reshape(b, kb * GRID_W, h, d)
        q_row = jnp.repeat(q_rows, GRID_W)[:, None]
        q_rs = jnp.repeat(row_start, GRID_W)[:, None]
        k_row = (band + jnp.repeat(jnp.arange(kb, dtype=jnp.int32), GRID_W))[None, :]
        in_window = ((k_row >= q_rs) & (k_row < q_rs + kh)
                     & (k_col >= q_cs) & (k_col < q_cs + WIN_W))
        d_row = jnp.clip(k_row - q_row + WIN_H - 1, 0, 2 * WIN_H - 2)
        d_col = jnp.clip(k_col - q_col + WIN_W - 1, 0, 2 * WIN_W - 2)
        bias = rpb[:, d_row, d_col].astype(jnp.float32)
        logit_lat = jnp.einsum("bqhd,bkhd->bhqk", qb, k_band).astype(jnp.float32) * scale + bias
        logit_lat = jnp.where(in_window, logit_lat, NEG_INF)
        logit_ctx = jnp.einsum("bqhd,bkhd->bhqk", qb, kc).astype(jnp.float32) * scale
        p = jax.nn.softmax(jnp.concatenate([logit_ctx, logit_lat], axis=-1), axis=-1).astype(v.dtype)
        return jnp.einsum("bhqk,bkhd->bqhd", p, jnp.concatenate([vc, v_band], axis=1))

    return map_query_blocks(block, q)


def diff_attention(q, k, v, lam):
    scale = HEAD_DIM ** -0.5

    def block(j, qb):
        logits = jnp.einsum("bqhmd,bkhmd->bhmqk", qb, k).astype(jnp.float32) * scale
        p = jax.nn.softmax(logits, axis=-1)
        a = (p[:, :, 0] - lam * p[:, :, 1]).astype(v.dtype)
        return jnp.einsum("bhqk,bkhd->bqhd", a, v)

    return map_query_blocks(block, q)


def mixer_a(u, uc, q_g, k_g, ang, need_ctx):
    b, s, _ = u.shape
    n_ctx = uc.shape[1]
    q_dim, kv_dim = A_HEADS * HEAD_DIM, A_KV_HEADS * HEAD_DIM

    def heads(t):
        n = t.shape[1]
        q, k, v = jnp.split(t, [q_dim, q_dim + kv_dim], axis=-1)
        q = rms_norm(q.reshape(b, n, A_KV_HEADS, A_GROUP, HEAD_DIM), q_g)
        k = rms_norm(k.reshape(b, n, A_KV_HEADS, HEAD_DIM), k_g)
        return q, k, v.reshape(b, n, A_KV_HEADS, HEAD_DIM)

    q, k, v = heads(u)
    qc, kc, vc = heads(uc)
    q, k = axial_rope(q, ang), axial_rope(k, ang)
    o = gqa_attention(q, jnp.concatenate([kc, k], axis=1),
                      jnp.concatenate([vc, v], axis=1)).reshape(b, s, MIX_WIDTH)
    oc = gqa_attention(qc, kc, vc).reshape(b, n_ctx, MIX_WIDTH) if need_ctx else None
    return o, oc


def mixer_b(u, uc, q_g, k_g, rpb, rows, need_ctx):
    b, s, _ = u.shape
    n_ctx = uc.shape[1]

    def heads(t):
        n = t.shape[1]
        q, k, v = jnp.split(t, 3, axis=-1)
        shp = (b, n, B_HEADS, HEAD_DIM)
        return rms_norm(q.reshape(shp), q_g), rms_norm(k.reshape(shp), k_g), v.reshape(shp)

    q, k, v = heads(u)
    qc, kc, vc = heads(uc)
    o = neighbourhood_attention(q, k, v, kc, vc, rpb, rows).reshape(b, s, MIX_WIDTH)
    oc = gqa_attention(qc[:, :, :, None], kc, vc).reshape(b, n_ctx, MIX_WIDTH) if need_ctx else None
    return o, oc


def mixer_c(u, uc, q_g, k_g, lq1, lk1, lq2, lk2, subln_g, lambda_init, ang, need_ctx):
    b, s, _ = u.shape
    n_ctx = uc.shape[1]
    qk_dim = C_HEADS * 2 * HEAD_DIM

    def heads(t):
        n = t.shape[1]
        q, k, v = jnp.split(t, [qk_dim, 2 * qk_dim], axis=-1)
        shp = (b, n, C_HEADS, 2, HEAD_DIM)
        return (rms_norm(q.reshape(shp), q_g), rms_norm(k.reshape(shp), k_g),
                v.reshape(b, n, C_HEADS, C_V_DIM))

    q, k, v = heads(u)
    qc, kc, vc = heads(uc)
    q, k = axial_rope(q, ang), axial_rope(k, ang)
    f32 = jnp.float32
    lam = (jnp.exp(jnp.sum(lq1.astype(f32) * lk1.astype(f32)))
           - jnp.exp(jnp.sum(lq2.astype(f32) * lk2.astype(f32))) + lambda_init)

    def finish(o, n):
        return (rms_norm(o, subln_g) * (1.0 - lambda_init)).reshape(b, n, MIX_WIDTH)

    o = finish(diff_attention(q, jnp.concatenate([kc, k], axis=1),
                              jnp.concatenate([vc, v], axis=1), lam), s)
    oc = finish(diff_attention(qc, kc, vc, lam), n_ctx) if need_ctx else None
    return o, oc


def setup_inputs(seed: int = 0) -> dict:
    key = jax.random.key(seed)
    ks = jax.random.split(key, 28)
    f32 = jnp.float32

    def nrm(k, shape, scale):
        return jax.random.normal(k, shape, f32) * scale

    def gain(k, shape):
        return 1.0 + 0.02 * jax.random.normal(k, shape, f32)

    d_in = D_MODEL ** -0.5
    d_mix = MIX_WIDTH ** -0.5
    return {
        "x": nrm(ks[0], (BATCH, SEQ, D_MODEL), 1.0),
        "c": nrm(ks[1], (BATCH, D_MODEL), 1.0),
        "ctx": nrm(ks[2], (BATCH, CTX_LEN, D_MODEL), 1.0),
        "c_ctx": nrm(ks[3], (D_MODEL,), 1.0),
        "norm_g": gain(ks[4], (DEPTH, D_MODEL)),
        "ada_w": nrm(ks[5], (DEPTH, D_MODEL, 3 * D_MODEL), 0.5 * d_in),
        "ada_b": nrm(ks[6], (DEPTH, 3 * D_MODEL), 0.01),
        "a_w_in": nrm(ks[7], (N_A, D_MODEL, A_IN), d_in),
        "a_q_g": gain(ks[8], (N_A, HEAD_DIM)),
        "a_k_g": gain(ks[9], (N_A, HEAD_DIM)),
        "a_w_out": nrm(ks[10], (N_A, MIX_WIDTH, D_MODEL), d_mix),
        "b_w_in": nrm(ks[11], (N_B, D_MODEL, B_IN), d_in),
        "b_q_g": gain(ks[12], (N_B, HEAD_DIM)),
        "b_k_g": gain(ks[13], (N_B, HEAD_DIM)),
        "b_rpb": nrm(ks[14], (N_B, B_HEADS, 2 * WIN_H - 1, 2 * WIN_W - 1), 0.1),
        "b_w_out": nrm(ks[15], (N_B, MIX_WIDTH, D_MODEL), d_mix),
        "c_w_in": nrm(ks[16], (N_C, D_MODEL, C_IN), d_in),
        "c_q_g": gain(ks[17], (N_C, HEAD_DIM)),
        "c_k_g": gain(ks[18], (N_C, HEAD_DIM)),
        "c_lam_q1": nrm(ks[19], (N_C, HEAD_DIM), 0.1),
        "c_lam_k1": nrm(ks[20], (N_C, HEAD_DIM), 0.1),
        "c_lam_q2": nrm(ks[21], (N_C, HEAD_DIM), 0.1),
        "c_lam_k2": nrm(ks[22], (N_C, HEAD_DIM), 0.1),
        "c_subln_g": gain(ks[23], (N_C, C_V_DIM)),
        "c_w_out": nrm(ks[24], (N_C, MIX_WIDTH, D_MODEL), d_mix),
    }


def reference(x, c, ctx, c_ctx, norm_g, ada_w, ada_b,
              a_w_in, a_q_g, a_k_g, a_w_out,
              b_w_in, b_q_g, b_k_g, b_rpb, b_w_out,
              c_w_in, c_q_g, c_k_g, c_lam_q1, c_lam_k1, c_lam_q2, c_lam_k2, c_subln_g, c_w_out):
    s = x.shape[1]
    rows = s // GRID_W
    ang = axial_rope_angles(s)
    xc = ctx
    for i in range(DEPTH):
        kind, j = i % N_MIXERS, i // N_MIXERS
        need_ctx = i < DEPTH - 1
        sh, sc, gt = jnp.split(jax.nn.silu(c) @ ada_w[i] + ada_b[i], 3, axis=-1)
        shc, scc, gtc = jnp.split(jax.nn.silu(c_ctx) @ ada_w[i] + ada_b[i], 3, axis=-1)
        h = rms_norm(x, norm_g[i]) * (1.0 + sc[:, None]) + sh[:, None]
        hc = rms_norm(xc, norm_g[i]) * (1.0 + scc) + shc
        w_in = (a_w_in, b_w_in, c_w_in)[kind][j]
        w_out = (a_w_out, b_w_out, c_w_out)[kind][j]
        n_mix_cols = w_in.shape[1] - MIX_WIDTH
        u, z = jnp.split(h @ w_in, [n_mix_cols], axis=-1)
        if need_ctx:
            uc, zc = jnp.split(hc @ w_in, [n_mix_cols], axis=-1)
        else:
            uc = hc @ w_in[:, :n_mix_cols]
        if kind == 0:
            o, oc = mixer_a(u, uc, a_q_g[j], a_k_g[j], ang, need_ctx)
        elif kind == 1:
            o, oc = mixer_b(u, uc, b_q_g[j], b_k_g[j], b_rpb[j], rows, need_ctx)
        else:
            o, oc = mixer_c(u, uc, c_q_g[j], c_k_g[j], c_lam_q1[j], c_lam_k1[j],
                            c_lam_q2[j], c_lam_k2[j], c_subln_g[j], lambda_init_fn(i), ang, need_ctx)
        x = x + gt[:, None] * ((o * jax.nn.silu(z)) @ w_out)
        if need_ctx:
            xc = xc + gtc * ((oc * jax.nn.silu(zc)) @ w_out)
    return x
```

```python
import functools
import math

import jax
import jax.numpy as jnp
from jax import lax
from jax.experimental import pallas as pl
from jax.experimental.pallas import tpu as pltpu

D_MODEL = 1024
DEPTH = 4
GRID_W = 64
CTX_LEN = 256
HEAD_DIM = 64
MIX_WIDTH = 1024
A_HEADS = 16
A_KV_HEADS = 4
B_HEADS = 16
C_HEADS = 8
C_V_DIM = 128
WIN_H = 8
WIN_W = 16
ROPE_BASE = 10000.0
EPS = 1e-6
NEG_INF = -1e30

LANES = 128
TOK = 256
SLOT = 4
VMEM_LIMIT = 56 * 1024 * 1024

_BF16 = jnp.bfloat16
_F32 = jnp.float32


def _cparams(*sem):
    return pltpu.CompilerParams(dimension_semantics=sem, vmem_limit_bytes=VMEM_LIMIT)


def _mod_kernel(c_ref, w_ref, b_ref, o_ref):
    cv = c_ref[...]
    s = cv * jax.nn.sigmoid(cv)
    o_ref[0] = jnp.dot(s, w_ref[0], preferred_element_type=_F32) + b_ref[0]


def _modulation(cvec, ada_w, ada_b):
    n_col = 512
    return pl.pallas_call(
        _mod_kernel,
        grid=(DEPTH, 3 * D_MODEL // n_col),
        in_specs=[
            pl.BlockSpec((16, D_MODEL), lambda l, n: (0, 0)),
            pl.BlockSpec((1, D_MODEL, n_col), lambda l, n: (l, 0, n)),
            pl.BlockSpec((1, 1, n_col), lambda l, n: (l, 0, n)),
        ],
        out_specs=pl.BlockSpec((1, 16, n_col), lambda l, n: (l, 0, n)),
        out_shape=jax.ShapeDtypeStruct((DEPTH, 16, 3 * D_MODEL), _F32),
        compiler_params=_cparams("arbitrary", "arbitrary"),
        name="adaln_mod",
    )(cvec, ada_w, ada_b.reshape(DEPTH, 1, 3 * D_MODEL))


def _lane_tile(col_ref_val):
    return jnp.concatenate([col_ref_val] * (TOK // LANES), axis=1)


def _head_norm_rope(u, gain, cos, sin, rope):
    ms = jnp.sum(u * u, axis=0, keepdims=True) * (1.0 / HEAD_DIM)
    y = u * lax.rsqrt(ms + EPS) * gain
    if not rope:
        return y
    x1r, x2r, x1c, x2c = y[0:16], y[16:32], y[32:48], y[48:64]
    cr, cc = cos[0:16], cos[16:32]
    sr, sc = sin[0:16], sin[16:32]
    return jnp.concatenate(
        [x1r * cr - x2r * sr, x2r * cr + x1r * sr, x1c * cc - x2c * sc, x2c * cc + x1c * sc], axis=0)


def _proj_kernel(x_ref, a_ref, sh_ref, w_ref, cos_ref, sin_ref, qg_ref, kg_ref,
                 q_ref, k_ref, v_ref, g_ref, *, n_q, n_k, n_v, rope):
    x = x_ref[0]
    ms = jnp.sum(x * x, axis=0, keepdims=True) * (1.0 / D_MODEL)
    h = x * lax.rsqrt(ms + EPS) * _lane_tile(a_ref[0]) + _lane_tile(sh_ref[0])
    hb = h.astype(_BF16)
    cos, sin = cos_ref[...], sin_ref[...]
    qg, kg = _lane_tile(qg_ref[...]), _lane_tile(kg_ref[...])

    uq = jnp.dot(w_ref[0:n_q, :], hb, preferred_element_type=_F32)
    for hd in range(n_q // HEAD_DIM):
        r = slice(hd * HEAD_DIM, (hd + 1) * HEAD_DIM)
        q_ref[0, r, :] = _head_norm_rope(uq[r], qg, cos, sin, rope).astype(_BF16)

    uk = jnp.dot(w_ref[n_q:n_q + n_k, :], hb, preferred_element_type=_F32)
    for blk in range(n_k // TOK):
        heads = [_head_norm_rope(uk[blk * TOK + s * HEAD_DIM: blk * TOK + (s + 1) * HEAD_DIM],
                                 kg, cos, sin, rope) for s in range(SLOT)]
        kt = jnp.concatenate(heads, axis=0)
        k_ref[0, 0, blk] = kt.T.astype(_BF16)

    o = n_q + n_k
    uv = jnp.dot(w_ref[o:o + n_v, :], hb, preferred_element_type=_F32)
    v_ref[0, 0] = uv.astype(_BF16)

    o = o + n_v
    uz = jnp.dot(w_ref[o:o + MIX_WIDTH, :], hb, preferred_element_type=_F32)
    g_ref[0] = (uz * jax.nn.sigmoid(uz)).astype(_BF16)


def _in_proj(xt, acol, shcol, w_t, cos_t, sin_t, qg, kg, *, n_q, n_k, n_v, rope):
    bsz, _, t = xt.shape
    nt = t // TOK
    n_in = w_t.shape[0]
    col = lambda b, j: (jnp.where(j == 0, bsz, b), 0, 0)
    return pl.pallas_call(
        functools.partial(_proj_kernel, n_q=n_q, n_k=n_k, n_v=n_v, rope=rope),
        grid=(bsz, nt),
        in_specs=[
            pl.BlockSpec((1, D_MODEL, TOK), lambda b, j: (b, 0, j)),
            pl.BlockSpec((1, D_MODEL, LANES), col),
            pl.BlockSpec((1, D_MODEL, LANES), col),
            pl.BlockSpec((n_in, D_MODEL), lambda b, j: (0, 0)),
            pl.BlockSpec((32, TOK), lambda b, j: (0, j)),
            pl.BlockSpec((32, TOK), lambda b, j: (0, j)),
            pl.BlockSpec((HEAD_DIM, LANES), lambda b, j: (0, 0)),
            pl.BlockSpec((HEAD_DIM, LANES), lambda b, j: (0, 0)),
        ],
        out_specs=[
            pl.BlockSpec((1, n_q, TOK), lambda b, j: (b, 0, j)),
            pl.BlockSpec((1, 1, n_k // TOK, TOK, TOK), lambda b, j: (b, j, 0, 0, 0)),
            pl.BlockSpec((1, 1, n_v, TOK), lambda b, j: (b, j, 0, 0)),
            pl.BlockSpec((1, MIX_WIDTH, TOK), lambda b, j: (b, 0, j)),
        ],
        out_shape=[
            jax.ShapeDtypeStruct((bsz, n_q, t), _BF16),
            jax.ShapeDtypeStruct((bsz, nt, n_k // TOK, TOK, TOK), _BF16),
            jax.ShapeDtypeStruct((bsz, nt, n_v, TOK), _BF16),
            jax.ShapeDtypeStruct((bsz, MIX_WIDTH, t), _BF16),
        ],
        compiler_params=_cparams("arbitrary", "arbitrary"),
        name="in_proj",
    )(xt, acol, shcol, w_t, cos_t, sin_t, qg, kg)


def _pad_slot(qh, slot):
    z = jnp.zeros_like(qh)
    return jnp.concatenate([qh if s == slot else z for s in range(SLOT)], axis=0)


def _online_step(s, vblk, m_ref, l_ref, acc_ref, i):
    m_old = m_ref[i]
    m_new = jnp.maximum(m_old, jnp.max(s, axis=0, keepdims=True))
    alpha = jnp.exp(m_old - m_new)
    p = jnp.exp(s - m_new)
    l_ref[i] = alpha * l_ref[i] + jnp.sum(p, axis=0, keepdims=True)
    acc_ref[i] = alpha * acc_ref[i] + jnp.dot(vblk, p.astype(_BF16), preferred_element_type=_F32)
    m_ref[i] = m_new


def _init_state(m_ref, l_ref, acc_ref):
    m_ref[...] = jnp.full(m_ref.shape, NEG_INF, _F32)
    l_ref[...] = jnp.zeros(l_ref.shape, _F32)
    acc_ref[...] = jnp.zeros(acc_ref.shape, _F32)


def _gqa_kernel(q_ref, k_ref, v_ref, o_ref, m_ref, l_ref, acc_ref):
    kvh = pl.program_id(1)
    n_kt = jnp.where(pl.program_id(2) == 0, 1, k_ref.shape[1])
    _init_state(m_ref, l_ref, acc_ref)
    row_slot = lax.broadcasted_iota(jnp.int32, (SLOT * HEAD_DIM, TOK), 0) // HEAD_DIM
    qpads = []
    for g in range(SLOT):
        qh = q_ref[0, g * HEAD_DIM:(g + 1) * HEAD_DIM, :].astype(_F32)
        qrep = jnp.concatenate([qh] * SLOT, axis=0)
        qpads.append(jnp.where(row_slot == kvh, qrep, 0.0).astype(_BF16))

    def body(kt, carry):
        kblk = k_ref[0, kt, 0]
        vblk = v_ref[0, kt]
        for g in range(SLOT):
            s = jnp.dot(kblk, qpads[g], preferred_element_type=_F32)
            _online_step(s, vblk, m_ref, l_ref, acc_ref, g)
        return carry

    lax.fori_loop(0, n_kt, body, 0)
    for g in range(SLOT):
        o_ref[0, g * HEAD_DIM:(g + 1) * HEAD_DIM, :] = (acc_ref[g] / l_ref[g]).astype(_BF16)


def _gqa_attention(qt, kt, vt):
    bsz, _, t = qt.shape
    nt = t // TOK
    return pl.pallas_call(
        _gqa_kernel,
        grid=(bsz, A_KV_HEADS, nt),
        in_specs=[
            pl.BlockSpec((1, SLOT * HEAD_DIM, TOK), lambda b, h, j: (b, h, j)),
            pl.BlockSpec((1, nt, 1, TOK, TOK), lambda b, h, j: (b, 0, 0, 0, 0)),
            pl.BlockSpec((1, nt, HEAD_DIM, TOK), lambda b, h, j: (b, 0, h, 0)),
        ],
        out_specs=pl.BlockSpec((1, SLOT * HEAD_DIM, TOK), lambda b, h, j: (b, h, j)),
        out_shape=jax.ShapeDtypeStruct((bsz, MIX_WIDTH, t), _BF16),
        scratch_shapes=[
            pltpu.VMEM((SLOT, 1, TOK), _F32),
            pltpu.VMEM((SLOT, 1, TOK), _F32),
            pltpu.VMEM((SLOT, HEAD_DIM, TOK), _F32),
        ],
        compiler_params=_cparams("arbitrary", "arbitrary", "arbitrary"),
        name="gqa_attn",
    )(qt, kt, vt)


def _diff_kernel(q_ref, k_ref, v_ref, lam_ref, sg_ref, o_ref, m_ref, l_ref, acc_ref, *, out_scale, lam_init):
    n_kt = jnp.where(pl.program_id(2) == 0, 1, k_ref.shape[1])
    _init_state(m_ref, l_ref, acc_ref)
    qpads = [_pad_slot(q_ref[0, s * HEAD_DIM:(s + 1) * HEAD_DIM, :], s) for s in range(SLOT)]

    def body(kt, carry):
        kblk = k_ref[0, kt, 0]
        for s_i in range(SLOT):
            hh = s_i // 2
            vblk = v_ref[0, kt, hh * C_V_DIM:(hh + 1) * C_V_DIM, :]
            s = jnp.dot(kblk, qpads[s_i], preferred_element_type=_F32)
            _online_step(s, vblk, m_ref, l_ref, acc_ref, s_i)
        return carry

    lax.fori_loop(0, n_kt, body, 0)
    lv = lam_ref[...]
    lam = (jnp.exp(jnp.sum(lv[0:1] * lv[1:2], axis=1, keepdims=True))
           - jnp.exp(jnp.sum(lv[2:3] * lv[3:4], axis=1, keepdims=True)) + lam_init)
    sg = _lane_tile(sg_ref[...])
    for hh in range(2):
        o = acc_ref[2 * hh] / l_ref[2 * hh] - lam * (acc_ref[2 * hh + 1] / l_ref[2 * hh + 1])
        ms = jnp.sum(o * o, axis=0, keepdims=True) * (1.0 / C_V_DIM)
        y = o * lax.rsqrt(ms + EPS) * sg * out_scale
        o_ref[0, hh * C_V_DIM:(hh + 1) * C_V_DIM, :] = y.astype(_BF16)


def _diff_attention(qt, kt, vt, lamv, sg, lam_init):
    bsz, _, t = qt.shape
    nt = t // TOK
    n_blk = kt.shape[2]
    return pl.pallas_call(
        functools.partial(_diff_kernel, out_scale=1.0 - lam_init, lam_init=lam_init),
        grid=(bsz, n_blk, nt),
        in_specs=[
            pl.BlockSpec((1, SLOT * HEAD_DIM, TOK), lambda b, c, j: (b, c, j)),
            pl.BlockSpec((1, nt, 1, TOK, TOK), lambda b, c, j: (b, 0, c, 0, 0)),
            pl.BlockSpec((1, nt, 2 * C_V_DIM, TOK), lambda b, c, j: (b, 0, c, 0)),
            pl.BlockSpec((8, LANES), lambda b, c, j: (0, 0)),
            pl.BlockSpec((C_V_DIM, LANES), lambda b, c, j: (0, 0)),
        ],
        out_specs=pl.BlockSpec((1, 2 * C_V_DIM, TOK), lambda b, c, j: (b, c, j)),
        out_shape=jax.ShapeDtypeStruct((bsz, MIX_WIDTH, t), _BF16),
        scratch_shapes=[
            pltpu.VMEM((SLOT, 1, TOK), _F32),
            pltpu.VMEM((SLOT, 1, TOK), _F32),
            pltpu.VMEM((SLOT, C_V_DIM, TOK), _F32),
        ],
        compiler_params=_cparams("arbitrary", "arbitrary", "arbitrary"),
        name="diff_attn",
    )(qt, kt, vt, lamv, sg)


N_BAND = 3


def _nbr_kernel(q_ref, kc_ref, k0_ref, k1_ref, k2_ref, vc_ref, v0_ref, v1_ref, v2_ref, bias_ref, o_ref):
    is_lat = pl.program_id(0) > 0
    kb = (k0_ref, k1_ref, k2_ref)
    vb = (v0_ref, v1_ref, v2_ref)

    def quad(c, carry):
        row0 = pl.multiple_of(c * (SLOT * HEAD_DIM), SLOT * HEAD_DIM)
        for g in range(SLOT):
            hrow = pl.ds(pl.multiple_of(row0 + g * HEAD_DIM, HEAD_DIM), HEAD_DIM)
            qh = q_ref[0, hrow, :]
            qpad = _pad_slot(qh, g)
            s_all = [jnp.dot(kc_ref[0, 0, c], qpad, preferred_element_type=_F32)]
            for d in range(N_BAND):
                s = jnp.dot(kb[d][0, 0, c], qpad, preferred_element_type=_F32)
                s = s + bias_ref[c * SLOT + g, 0, d * TOK:(d + 1) * TOK, :]
                s_all.append(jnp.where(is_lat, s, NEG_INF))
            m = s_all[0].max(axis=0, keepdims=True)
            for s in s_all[1:]:
                m = jnp.maximum(m, s.max(axis=0, keepdims=True))
            vs = [vc_ref[0, 0, hrow, :]] + [vb[d][0, 0, hrow, :] for d in range(N_BAND)]
            l = jnp.zeros((1, TOK), _F32)
            acc = jnp.zeros((HEAD_DIM, TOK), _F32)
            for s, vblk in zip(s_all, vs):
                p = jnp.exp(s - m)
                l = l + jnp.sum(p, axis=0, keepdims=True)
                acc = acc + jnp.dot(vblk, p.astype(_BF16), preferred_element_type=_F32)
            o_ref[0, hrow, :] = (acc / l).astype(_BF16)
        return carry

    lax.fori_loop(0, B_HEADS // SLOT, quad, 0)


def _nbr_attention(qt, kt, vt, bias):
    bsz, _, t = qt.shape
    nt = t // TOK
    n_blk = kt.shape[2]
    n_lat = nt - 1

    def band(d):
        return lambda j, b: (b, 1 + jnp.clip(j - 2, 0, n_lat - N_BAND) + d, 0, 0, 0)

    def vband(d):
        return lambda j, b: (b, 1 + jnp.clip(j - 2, 0, n_lat - N_BAND) + d, 0, 0)

    case = lambda j, b: (0, jnp.where(j <= 1, 0, jnp.where(j == nt - 1, 2, 1)), 0, 0)
    kspec = lambda im: pl.BlockSpec((1, 1, n_blk, TOK, TOK), im)
    vspec = lambda im: pl.BlockSpec((1, 1, MIX_WIDTH, TOK), im)
    return pl.pallas_call(
        _nbr_kernel,
        grid=(nt, bsz),
        in_specs=[
            pl.BlockSpec((1, MIX_WIDTH, TOK), lambda j, b: (b, 0, j)),
            kspec(lambda j, b: (b, 0, 0, 0, 0)), kspec(band(0)), kspec(band(1)), kspec(band(2)),
            vspec(lambda j, b: (b, 0, 0, 0)), vspec(vband(0)), vspec(vband(1)), vspec(vband(2)),
            pl.BlockSpec((B_HEADS, 1, N_BAND * TOK, TOK), case),
        ],
        out_specs=pl.BlockSpec((1, MIX_WIDTH, TOK), lambda j, b: (b, 0, j)),
        out_shape=jax.ShapeDtypeStruct((bsz, MIX_WIDTH, t), _BF16),
        compiler_params=_cparams("arbitrary", "arbitrary"),
        name="nbr_attn",
    )(qt, kt, kt, kt, kt, vt, vt, vt, vt, bias)


def _nbr_bias(rpb):
    rows = 4096 // GRID_W
    q_per = TOK // GRID_W
    tables = []
    for i in (0, 1, rows // q_per - 1):
        s = min(max(i - 1, 0), rows // q_per - N_BAND)
        kr = (q_per * s + jnp.arange(N_BAND * q_per, dtype=jnp.int32))[:, None, None, None]
        kc = jnp.arange(GRID_W, dtype=jnp.int32)[None, :, None, None]
        qr = (q_per * i + jnp.arange(q_per, dtype=jnp.int32))[None, None, :, None]
        qc = jnp.arange(GRID_W, dtype=jnp.int32)[None, None, None, :]
        rs = jnp.clip(qr - WIN_H // 2, 0, rows - WIN_H)
        cs = jnp.clip(qc - WIN_W // 2, 0, GRID_W - WIN_W)
        inside = (kr >= rs) & (kr < rs + WIN_H) & (kc >= cs) & (kc < cs + WIN_W)
        d_row = jnp.broadcast_to(jnp.clip(kr - qr + WIN_H - 1, 0, 2 * WIN_H - 2), inside.shape)
        d_col = jnp.broadcast_to(jnp.clip(kc - qc + WIN_W - 1, 0, 2 * WIN_W - 2), inside.shape)
        tab = jnp.where(inside[None], rpb[:, d_row, d_col].astype(_F32), NEG_INF)
        tables.append(tab.reshape(B_HEADS, N_BAND * TOK, TOK))
    return jnp.stack(tables, axis=1)


def _out_kernel(o_ref, g_ref, w_ref, gt_ref, x_ref, y_ref):
    og = (o_ref[0].astype(_F32) * g_ref[0].astype(_F32)).astype(_BF16)
    y = jnp.dot(w_ref[...], og, preferred_element_type=_F32)
    y_ref[0] = x_ref[0] + _lane_tile(gt_ref[0]) * y


def _out_proj(ot, gt, w_t, gtcol, xt):
    bsz, _, t = xt.shape
    nt = t // TOK
    col = lambda b, j: (jnp.where(j == 0, bsz, b), 0, 0)
    tile = lambda b, j: (b, 0, j)
    return pl.pallas_call(
        _out_kernel,
        grid=(bsz, nt),
        in_specs=[
            pl.BlockSpec((1, MIX_WIDTH, TOK), tile),
            pl.BlockSpec((1, MIX_WIDTH, TOK), tile),
            pl.BlockSpec((D_MODEL, MIX_WIDTH), lambda b, j: (0, 0)),
            pl.BlockSpec((1, D_MODEL, LANES), col),
            pl.BlockSpec((1, D_MODEL, TOK), tile),
        ],
        out_specs=pl.BlockSpec((1, D_MODEL, TOK), tile),
        out_shape=jax.ShapeDtypeStruct(xt.shape, _F32),
        input_output_aliases={4: 0},
        compiler_params=_cparams("arbitrary", "arbitrary"),
        name="out_proj",
    )(ot, gt, w_t, gtcol, xt)


def _rope_tables(s):
    tok = jnp.arange(s, dtype=jnp.int32)
    rows = (tok // GRID_W).astype(_F32)
    cols = (tok % GRID_W).astype(_F32)
    n_freq = HEAD_DIM // 4
    inv_freq = ROPE_BASE ** (-jnp.arange(n_freq, dtype=_F32) / n_freq)
    ang = jnp.concatenate([rows[:, None] * inv_freq, cols[:, None] * inv_freq], axis=1)
    ang = jnp.concatenate([jnp.zeros((CTX_LEN, 32), _F32), ang], axis=0)
    return jnp.cos(ang).T, jnp.sin(ang).T


def _col(v):
    return jnp.broadcast_to(v[..., None], v.shape + (LANES,))


def _lambda_init(layer):
    return 0.8 - 0.6 * math.exp(-0.3 * layer)


def kernel(x, c, ctx, c_ctx, norm_g, ada_w, ada_b, a_w_in, a_q_g, a_k_g, a_w_out, b_w_in, b_q_g, b_k_g, b_rpb, b_w_out, c_w_in, c_q_g, c_k_g, c_lam_q1, c_lam_k1, c_lam_q2, c_lam_k2, c_subln_g, c_w_out):
    bsz, s, _ = x.shape
    assert c.shape[0] + 1 <= 16
    scale = HEAD_DIM ** -0.5

    cvec = jnp.zeros((16, D_MODEL), _F32).at[:bsz].set(c).at[bsz].set(c_ctx)
    mod = _modulation(cvec, ada_w, ada_b)[:, :bsz + 1]
    sh, sc, gt = jnp.split(mod, 3, axis=-1)
    acol = _col(norm_g[:, None, :] * (1.0 + sc))
    shcol = _col(sh)
    gtcol = _col(gt)

    cos_t, sin_t = _rope_tables(s)
    xt = jnp.swapaxes(jnp.concatenate([ctx, x], axis=1), 1, 2)

    for i in range(DEPTH):
        kind, j = i % 3, i // 3
        if kind == 0:
            w_in, w_out, q_g, k_g = a_w_in[j], a_w_out[j], a_q_g[j], a_k_g[j]
            dims = dict(n_q=A_HEADS * HEAD_DIM, n_k=A_KV_HEADS * HEAD_DIM, n_v=A_KV_HEADS * HEAD_DIM, rope=True)
        elif kind == 1:
            w_in, w_out, q_g, k_g = b_w_in[j], b_w_out[j], b_q_g[j], b_k_g[j]
            dims = dict(n_q=MIX_WIDTH, n_k=MIX_WIDTH, n_v=MIX_WIDTH, rope=False)
        else:
            w_in, w_out, q_g, k_g = c_w_in[j], c_w_out[j], c_q_g[j], c_k_g[j]
            dims = dict(n_q=MIX_WIDTH, n_k=MIX_WIDTH, n_v=MIX_WIDTH, rope=True)
        qt, kt, vt, zt = _in_proj(xt, acol[i], shcol[i], w_in.T.astype(_BF16), cos_t, sin_t,
                                  _col(q_g * scale), _col(k_g), **dims)
        if kind == 0:
            ot = _gqa_attention(qt, kt, vt)
        elif kind == 1:
            ot = _nbr_attention(qt, kt, vt, _nbr_bias(b_rpb[j]))
        else:
            lamv = jnp.zeros((8, LANES), _F32).at[0:4, :HEAD_DIM].set(
                jnp.stack([c_lam_q1[j], c_lam_k1[j], c_lam_q2[j], c_lam_k2[j]]).astype(_F32))
            ot = _diff_attention(qt, kt, vt, lamv, _col(c_subln_g[j]), _lambda_init(i))
        xt = _out_proj(ot, zt, w_out.T.astype(_BF16), gtcol[i], xt)

    return jnp.swapaxes(xt[:, :, CTX_LEN:], 1, 2)
```

```python
import functools
import math

import jax
import jax.numpy as jnp
import numpy as np
from jax import lax
from jax.experimental import pallas as pl
from jax.experimental.pallas import tpu as pltpu

D_MODEL = 1024
DEPTH = 4
GRID_W = 64
CTX_LEN = 256
HEAD_DIM = 64
MIX_WIDTH = 1024
A_HEADS = 16
A_KV_HEADS = 4
B_HEADS = 16
C_HEADS = 8
C_V_DIM = 128
WIN_H = 8
WIN_W = 16
ROPE_BASE = 10000.0
EPS = 1e-6
NEG_INF = -1e30
LOG2E = 1.4426950408889634

LANES = 128
TOK = 256
SLOT = 4
VMEM_LIMIT = 56 * 1024 * 1024

_BF16 = jnp.bfloat16
_F32 = jnp.float32


def _cparams(*sem):
    return pltpu.CompilerParams(dimension_semantics=sem, vmem_limit_bytes=VMEM_LIMIT)


def _mod_kernel(c_ref, w_ref, b_ref, o_ref):
    cv = c_ref[...]
    s = cv * jax.nn.sigmoid(cv)
    o_ref[0] = jnp.dot(s, w_ref[0], preferred_element_type=_F32) + b_ref[0]


def _modulation(cvec, ada_w, ada_b):
    n_col = 512
    return pl.pallas_call(
        _mod_kernel,
        grid=(DEPTH, 3 * D_MODEL // n_col),
        in_specs=[
            pl.BlockSpec((16, D_MODEL), lambda l, n: (0, 0)),
            pl.BlockSpec((1, D_MODEL, n_col), lambda l, n: (l, 0, n)),
            pl.BlockSpec((1, 1, n_col), lambda l, n: (l, 0, n)),
        ],
        out_specs=pl.BlockSpec((1, 16, n_col), lambda l, n: (l, 0, n)),
        out_shape=jax.ShapeDtypeStruct((DEPTH, 16, 3 * D_MODEL), _F32),
        compiler_params=_cparams("arbitrary", "arbitrary"),
        name="adaln_mod",
    )(cvec, ada_w, ada_b.reshape(DEPTH, 1, 3 * D_MODEL))


def _lane_tile(col_ref_val):
    return jnp.concatenate([col_ref_val] * (TOK // LANES), axis=1)


def _head_norm_rope(u, gain, cos, sin, rope):
    ms = jnp.sum(u * u, axis=0, keepdims=True) * (1.0 / HEAD_DIM)
    y = u * lax.rsqrt(ms + EPS) * gain
    if not rope:
        return y
    x1r, x2r, x1c, x2c = y[0:16], y[16:32], y[32:48], y[48:64]
    cr, cc = cos[0:16], cos[16:32]
    sr, sc = sin[0:16], sin[16:32]
    return jnp.concatenate(
        [x1r * cr - x2r * sr, x2r * cr + x1r * sr, x1c * cc - x2c * sc, x2c * cc + x1c * sc], axis=0)


def _proj_kernel(x_ref, a_ref, sh_ref, w_ref, cos_ref, sin_ref, qg_ref, kg_ref,
                 q_ref, k_ref, v_ref, g_ref, *, n_q, n_k, n_v, rope):
    x = x_ref[0]
    ms = jnp.sum(x * x, axis=0, keepdims=True) * (1.0 / D_MODEL)
    h = x * lax.rsqrt(ms + EPS) * _lane_tile(a_ref[0]) + _lane_tile(sh_ref[0])
    hb = h.astype(_BF16)
    cos, sin = cos_ref[...], sin_ref[...]
    qg, kg = _lane_tile(qg_ref[...]), _lane_tile(kg_ref[...])

    uq = jnp.dot(w_ref[0:n_q, :], hb, preferred_element_type=_F32)
    for hd in range(n_q // HEAD_DIM):
        r = slice(hd * HEAD_DIM, (hd + 1) * HEAD_DIM)
        q_ref[0, r, :] = _head_norm_rope(uq[r], qg, cos, sin, rope).astype(_BF16)

    uk = jnp.dot(w_ref[n_q:n_q + n_k, :], hb, preferred_element_type=_F32)
    for blk in range(n_k // TOK):
        heads = [_head_norm_rope(uk[blk * TOK + s * HEAD_DIM: blk * TOK + (s + 1) * HEAD_DIM],
                                 kg, cos, sin, rope) for s in range(SLOT)]
        kt = jnp.concatenate(heads, axis=0)
        k_ref[0, 0, blk] = kt.T.astype(_BF16)

    o = n_q + n_k
    uv = jnp.dot(w_ref[o:o + n_v, :], hb, preferred_element_type=_F32)
    v_ref[0, 0] = uv.astype(_BF16)

    o = o + n_v
    uz = jnp.dot(w_ref[o:o + MIX_WIDTH, :], hb, preferred_element_type=_F32)
    g_ref[0] = (uz * jax.nn.sigmoid(uz)).astype(_BF16)


def _in_proj(xt, acol, shcol, w_t, cos_t, sin_t, qg, kg, *, n_q, n_k, n_v, rope):
    bsz, _, t = xt.shape
    nt = t // TOK
    n_in = w_t.shape[0]
    col = lambda b, j: (jnp.where(j == 0, bsz, b), 0, 0)
    return pl.pallas_call(
        functools.partial(_proj_kernel, n_q=n_q, n_k=n_k, n_v=n_v, rope=rope),
        grid=(bsz, nt),
        in_specs=[
            pl.BlockSpec((1, D_MODEL, TOK), lambda b, j: (b, 0, j)),
            pl.BlockSpec((1, D_MODEL, LANES), col),
            pl.BlockSpec((1, D_MODEL, LANES), col),
            pl.BlockSpec((n_in, D_MODEL), lambda b, j: (0, 0)),
            pl.BlockSpec((32, TOK), lambda b, j: (0, j)),
            pl.BlockSpec((32, TOK), lambda b, j: (0, j)),
            pl.BlockSpec((HEAD_DIM, LANES), lambda b, j: (0, 0)),
            pl.BlockSpec((HEAD_DIM, LANES), lambda b, j: (0, 0)),
        ],
        out_specs=[
            pl.BlockSpec((1, n_q, TOK), lambda b, j: (b, 0, j)),
            pl.BlockSpec((1, 1, n_k // TOK, TOK, TOK), lambda b, j: (b, j, 0, 0, 0)),
            pl.BlockSpec((1, 1, n_v, TOK), lambda b, j: (b, j, 0, 0)),
            pl.BlockSpec((1, MIX_WIDTH, TOK), lambda b, j: (b, 0, j)),
        ],
        out_shape=[
            jax.ShapeDtypeStruct((bsz, n_q, t), _BF16),
            jax.ShapeDtypeStruct((bsz, nt, n_k // TOK, TOK, TOK), _BF16),
            jax.ShapeDtypeStruct((bsz, nt, n_v, TOK), _BF16),
            jax.ShapeDtypeStruct((bsz, MIX_WIDTH, t), _BF16),
        ],
        compiler_params=_cparams("arbitrary", "arbitrary"),
        name="in_proj",
    )(xt, acol, shcol, w_t, cos_t, sin_t, qg, kg)


def _pad_slot(qh, slot):
    z = jnp.zeros_like(qh)
    return jnp.concatenate([qh if s == slot else z for s in range(SLOT)], axis=0)


NCOL = SLOT * TOK
ROW_CHUNK = 32
PIPE_UNROLL = 2


def _score_stage(k_tile, qcat_ref, s_ref, sb):
    s_ref[sb] = jnp.dot(k_tile, qcat_ref[...], preferred_element_type=_F32)


def _softmax_stage(s_ref, p_ref, al_ref, m_ref, sb):
    m_old = m_ref[...]
    m_new = jnp.maximum(m_old, jnp.max(s_ref[sb], axis=0, keepdims=True))
    al_ref[sb] = jnp.exp2(m_old - m_new)
    m_ref[...] = m_new
    for r in range(0, TOK, ROW_CHUNK):
        p_ref[sb, r:r + ROW_CHUNK, :] = jnp.exp2(s_ref[sb, r:r + ROW_CHUNK, :] - m_new).astype(_BF16)


ONES_ROWS = 16


def _value_stage(v_tile, p_ref, al_ref, acc_ref, sb, cols=slice(None)):
    v_aug = jnp.concatenate([v_tile, jnp.ones((ONES_ROWS, v_tile.shape[1]), _BF16)], axis=0)
    acc_ref[:, cols] = (al_ref[sb][:, cols] * acc_ref[:, cols]
                        + jnp.dot(v_aug, p_ref[sb, :, cols], preferred_element_type=_F32))


def _attend(n_kt, k_tile, value, qcat_ref, s_ref, p_ref, al_ref, m_ref, acc_ref):
    stages = (s_ref, p_ref, al_ref, m_ref)
    m_ref[...] = jnp.full(m_ref.shape, NEG_INF, _F32)
    acc_ref[...] = jnp.zeros(acc_ref.shape, _F32)
    p_ref[1] = jnp.zeros(p_ref.shape[1:], _BF16)
    al_ref[1] = jnp.ones(al_ref.shape[1:], _F32)
    _score_stage(k_tile(0), qcat_ref, s_ref, 0)

    def slots(u, carry):
        for j in range(PIPE_UNROLL):
            k = PIPE_UNROLL * u + j
            par = j % 2
            _score_stage(k_tile(k + 1), qcat_ref, s_ref, 1 - par)
            _softmax_stage(*stages, par)
            value(jnp.maximum(k - 1, 0), 1 - par)
        return carry

    lax.fori_loop(0, (n_kt - 1) // PIPE_UNROLL, slots, 0)
    last = n_kt - 1
    _softmax_stage(*stages, 0)
    value(jnp.maximum(last - 1, 0), 1)
    value(last, 0)


def _attend_scratch(dv):
    return [
        pltpu.VMEM((SLOT * HEAD_DIM, NCOL), _BF16),
        pltpu.VMEM((2, TOK, NCOL), _F32),
        pltpu.VMEM((2, TOK, NCOL), _BF16),
        pltpu.VMEM((2, 1, NCOL), _F32),
        pltpu.VMEM((1, NCOL), _F32),
        pltpu.VMEM((dv + ONES_ROWS, NCOL), _F32),
    ]


def _gqa_kernel(q_ref, k_ref, v_ref, o_ref, qcat_ref, s_ref, p_ref, al_ref, m_ref, acc_ref):
    kvh = pl.program_id(1)
    n_kt = jnp.where(pl.program_id(2) == 0, 1, k_ref.shape[1])
    row_slot = lax.broadcasted_iota(jnp.int32, (SLOT * HEAD_DIM, TOK), 0) // HEAD_DIM
    for g in range(SLOT):
        qh = q_ref[0, g * HEAD_DIM:(g + 1) * HEAD_DIM, :].astype(_F32)
        qrep = jnp.concatenate([qh] * SLOT, axis=0)
        qcat_ref[:, g * TOK:(g + 1) * TOK] = jnp.where(row_slot == kvh, qrep, 0.0).astype(_BF16)

    _attend(n_kt, lambda t: k_ref[0, t, 0],
            lambda t, sb: _value_stage(v_ref[0, t], p_ref, al_ref, acc_ref, sb),
            qcat_ref, s_ref, p_ref, al_ref, m_ref, acc_ref)
    o = acc_ref[0:HEAD_DIM, :] / acc_ref[HEAD_DIM:HEAD_DIM + 1, :]
    for g in range(SLOT):
        o_ref[0, g * HEAD_DIM:(g + 1) * HEAD_DIM, :] = o[:, g * TOK:(g + 1) * TOK].astype(_BF16)


def _gqa_attention(qt, kt, vt):
    bsz, _, t = qt.shape
    nt = t // TOK
    assert nt % 2 == 1
    return pl.pallas_call(
        _gqa_kernel,
        grid=(bsz, A_KV_HEADS, nt),
        in_specs=[
            pl.BlockSpec((1, SLOT * HEAD_DIM, TOK), lambda b, h, j: (b, h, j)),
            pl.BlockSpec((1, nt, 1, TOK, TOK), lambda b, h, j: (b, 0, 0, 0, 0)),
            pl.BlockSpec((1, nt, HEAD_DIM, TOK), lambda b, h, j: (b, 0, h, 0)),
        ],
        out_specs=pl.BlockSpec((1, SLOT * HEAD_DIM, TOK), lambda b, h, j: (b, h, j)),
        out_shape=jax.ShapeDtypeStruct((bsz, MIX_WIDTH, t), _BF16),
        scratch_shapes=_attend_scratch(HEAD_DIM),
        compiler_params=_cparams("arbitrary", "arbitrary", "arbitrary"),
        name="gqa_attn",
    )(qt, kt, vt)


def _diff_kernel(q_ref, k_ref, v_ref, lam_ref, sg_ref, o_ref, qcat_ref, s_ref, p_ref, al_ref, m_ref, acc_ref,
                 *, out_scale, lam_init):
    n_kt = jnp.where(pl.program_id(2) == 0, 1, k_ref.shape[1])
    for j in range(SLOT):
        qcat_ref[:, j * TOK:(j + 1) * TOK] = _pad_slot(q_ref[0, j * HEAD_DIM:(j + 1) * HEAD_DIM, :], j)

    def value(t, sb):
        for hh in range(2):
            _value_stage(v_ref[0, t, hh * C_V_DIM:(hh + 1) * C_V_DIM, :], p_ref, al_ref, acc_ref, sb,
                         cols=slice(2 * hh * TOK, 2 * (hh + 1) * TOK))

    _attend(n_kt, lambda t: k_ref[0, t, 0], value, qcat_ref, s_ref, p_ref, al_ref, m_ref, acc_ref)
    lv = lam_ref[...]
    lam = (jnp.exp(jnp.sum(lv[0:1] * lv[1:2], axis=1, keepdims=True))
           - jnp.exp(jnp.sum(lv[2:3] * lv[3:4], axis=1, keepdims=True)) + lam_init)
    sg = _lane_tile(sg_ref[...])
    attn = acc_ref[0:C_V_DIM, :] / acc_ref[C_V_DIM:C_V_DIM + 1, :]
    for hh in range(2):
        o = attn[:, 2 * hh * TOK:(2 * hh + 1) * TOK] - lam * attn[:, (2 * hh + 1) * TOK:(2 * hh + 2) * TOK]
        ms = jnp.sum(o * o, axis=0, keepdims=True) * (1.0 / C_V_DIM)
        y = o * lax.rsqrt(ms + EPS) * sg * out_scale
        o_ref[0, hh * C_V_DIM:(hh + 1) * C_V_DIM, :] = y.astype(_BF16)


def _diff_attention(qt, kt, vt, lamv, sg, lam_init):
    bsz, _, t = qt.shape
    nt = t // TOK
    n_blk = kt.shape[2]
    return pl.pallas_call(
        functools.partial(_diff_kernel, out_scale=1.0 - lam_init, lam_init=lam_init),
        grid=(bsz, n_blk, nt),
        in_specs=[
            pl.BlockSpec((1, SLOT * HEAD_DIM, TOK), lambda b, c, j: (b, c, j)),
            pl.BlockSpec((1, nt, 1, TOK, TOK), lambda b, c, j: (b, 0, c, 0, 0)),
            pl.BlockSpec((1, nt, 2 * C_V_DIM, TOK), lambda b, c, j: (b, 0, c, 0)),
            pl.BlockSpec((8, LANES), lambda b, c, j: (0, 0)),
            pl.BlockSpec((C_V_DIM, LANES), lambda b, c, j: (0, 0)),
        ],
        out_specs=pl.BlockSpec((1, 2 * C_V_DIM, TOK), lambda b, c, j: (b, c, j)),
        out_shape=jax.ShapeDtypeStruct((bsz, MIX_WIDTH, t), _BF16),
        scratch_shapes=_attend_scratch(C_V_DIM),
        compiler_params=_cparams("arbitrary", "arbitrary", "arbitrary"),
        name="diff_attn",
    )(qt, kt, vt, lamv, sg)


N_BAND = 3


def _nbr_kernel(q_ref, kc_ref, k0_ref, k1_ref, k2_ref, vc_ref, v0_ref, v1_ref, v2_ref, bias_ref, o_ref):
    is_lat = pl.program_id(0) > 0
    kb = (k0_ref, k1_ref, k2_ref)
    vb = (v0_ref, v1_ref, v2_ref)

    def quad(c, carry):
        row0 = pl.multiple_of(c * (SLOT * HEAD_DIM), SLOT * HEAD_DIM)
        for g in range(SLOT):
            hrow = pl.ds(pl.multiple_of(row0 + g * HEAD_DIM, HEAD_DIM), HEAD_DIM)
            qh = q_ref[0, hrow, :]
            qpad = _pad_slot(qh, g)
            s_all = [jnp.dot(kc_ref[0, 0, c], qpad, preferred_element_type=_F32)]
            for d in range(N_BAND):
                s = jnp.dot(kb[d][0, 0, c], qpad, preferred_element_type=_F32)
                s = s + bias_ref[c * SLOT + g, 0, d * TOK:(d + 1) * TOK, :]
                s_all.append(jnp.where(is_lat, s, NEG_INF))
            m = s_all[0].max(axis=0, keepdims=True)
            for s in s_all[1:]:
                m = jnp.maximum(m, s.max(axis=0, keepdims=True))
            vs = [vc_ref[0, 0, hrow, :]] + [vb[d][0, 0, hrow, :] for d in range(N_BAND)]
            l = jnp.zeros((1, TOK), _F32)
            acc = jnp.zeros((HEAD_DIM, TOK), _F32)
            for s, vblk in zip(s_all, vs):
                p = jnp.exp(s - m)
                l = l + jnp.sum(p, axis=0, keepdims=True)
                acc = acc + jnp.dot(vblk, p.astype(_BF16), preferred_element_type=_F32)
            o_ref[0, hrow, :] = (acc / l).astype(_BF16)
        return carry

    lax.fori_loop(0, B_HEADS // SLOT, quad, 0)


def _nbr_attention(qt, kt, vt, bias):
    bsz, _, t = qt.shape
    nt = t // TOK
    n_blk = kt.shape[2]
    n_lat = nt - 1

    def band(d):
        return lambda j, b: (b, 1 + jnp.clip(j - 2, 0, n_lat - N_BAND) + d, 0, 0, 0)

    def vband(d):
        return lambda j, b: (b, 1 + jnp.clip(j - 2, 0, n_lat - N_BAND) + d, 0, 0)

    case = lambda j, b: (0, jnp.where(j <= 1, 0, jnp.where(j == nt - 1, 2, 1)), 0, 0)
    kspec = lambda im: pl.BlockSpec((1, 1, n_blk, TOK, TOK), im)
    vspec = lambda im: pl.BlockSpec((1, 1, MIX_WIDTH, TOK), im)
    return pl.pallas_call(
        _nbr_kernel,
        grid=(nt, bsz),
        in_specs=[
            pl.BlockSpec((1, MIX_WIDTH, TOK), lambda j, b: (b, 0, j)),
            kspec(lambda j, b: (b, 0, 0, 0, 0)), kspec(band(0)), kspec(band(1)), kspec(band(2)),
            vspec(lambda j, b: (b, 0, 0, 0)), vspec(vband(0)), vspec(vband(1)), vspec(vband(2)),
            pl.BlockSpec((B_HEADS, 1, N_BAND * TOK, TOK), case),
        ],
        out_specs=pl.BlockSpec((1, MIX_WIDTH, TOK), lambda j, b: (b, 0, j)),
        out_shape=jax.ShapeDtypeStruct((bsz, MIX_WIDTH, t), _BF16),
        compiler_params=_cparams("arbitrary", "arbitrary"),
        name="nbr_attn",
    )(qt, kt, kt, kt, kt, vt, vt, vt, vt, bias)


def _nbr_bias(rpb):
    rows = 4096 // GRID_W
    q_per = TOK // GRID_W
    hi = lax.Precision.HIGHEST
    kc = np.arange(GRID_W)[:, None]
    qc = np.arange(GRID_W)[None, :]
    d_col = np.clip(kc - qc + WIN_W - 1, 0, 2 * WIN_W - 2)
    cs = np.clip(qc - WIN_W // 2, 0, GRID_W - WIN_W)
    col_in = (kc >= cs) & (kc < cs + WIN_W)
    e_col = (d_col[None] == np.arange(2 * WIN_W - 1)[:, None, None]).astype(np.float32)
    by_col = jnp.einsum("hdj,jkq->hdkq", rpb.astype(_F32), e_col, precision=hi)
    tables = []
    for i in (0, 1, rows // q_per - 1):
        s = min(max(i - 1, 0), rows // q_per - N_BAND)
        kr = (q_per * s + np.arange(N_BAND * q_per))[:, None]
        qr = (q_per * i + np.arange(q_per))[None, :]
        d_row = np.clip(kr - qr + WIN_H - 1, 0, 2 * WIN_H - 2)
        rs = np.clip(qr - WIN_H // 2, 0, rows - WIN_H)
        row_in = (kr >= rs) & (kr < rs + WIN_H)
        e_row = (d_row[None] == np.arange(2 * WIN_H - 1)[:, None, None]).astype(np.float32)
        tab = jnp.einsum("dab,hdkq->hakbq", e_row, by_col, precision=hi)
        inside = row_in[:, None, :, None] & col_in[None, :, None, :]
        tab = jnp.where(inside[None], tab, NEG_INF)
        tables.append(tab.reshape(B_HEADS, N_BAND * TOK, TOK))
    return jnp.stack(tables, axis=1)


def _out_kernel(o_ref, g_ref, w_ref, gt_ref, x_ref, y_ref):
    og = (o_ref[0].astype(_F32) * g_ref[0].astype(_F32)).astype(_BF16)
    y = jnp.dot(w_ref[...], og, preferred_element_type=_F32)
    y_ref[0] = x_ref[0] + _lane_tile(gt_ref[0]) * y


def _out_proj(ot, gt, w_t, gtcol, xt):
    bsz, _, t = xt.shape
    nt = t // TOK
    col = lambda b, j: (jnp.where(j == 0, bsz, b), 0, 0)
    tile = lambda b, j: (b, 0, j)
    return pl.pallas_call(
        _out_kernel,
        grid=(bsz, nt),
        in_specs=[
            pl.BlockSpec((1, MIX_WIDTH, TOK), tile),
            pl.BlockSpec((1, MIX_WIDTH, TOK), tile),
            pl.BlockSpec((D_MODEL, MIX_WIDTH), lambda b, j: (0, 0)),
            pl.BlockSpec((1, D_MODEL, LANES), col),
            pl.BlockSpec((1, D_MODEL, TOK), tile),
        ],
        out_specs=pl.BlockSpec((1, D_MODEL, TOK), tile),
        out_shape=jax.ShapeDtypeStruct(xt.shape, _F32),
        input_output_aliases={4: 0},
        compiler_params=_cparams("arbitrary", "arbitrary"),
        name="out_proj",
    )(ot, gt, w_t, gtcol, xt)


def _rope_tables(s):
    tok = jnp.arange(s, dtype=jnp.int32)
    rows = (tok // GRID_W).astype(_F32)
    cols = (tok % GRID_W).astype(_F32)
    n_freq = HEAD_DIM // 4
    inv_freq = ROPE_BASE ** (-jnp.arange(n_freq, dtype=_F32) / n_freq)
    ang = jnp.concatenate([rows[:, None] * inv_freq, cols[:, None] * inv_freq], axis=1)
    ang = jnp.concatenate([jnp.zeros((CTX_LEN, 32), _F32), ang], axis=0)
    return jnp.cos(ang).T, jnp.sin(ang).T


def _col(v):
    return jnp.broadcast_to(v[..., None], v.shape + (LANES,))


def _lambda_init(layer):
    return 0.8 - 0.6 * math.exp(-0.3 * layer)


def kernel(x, c, ctx, c_ctx, norm_g, ada_w, ada_b, a_w_in, a_q_g, a_k_g, a_w_out, b_w_in, b_q_g, b_k_g, b_rpb, b_w_out, c_w_in, c_q_g, c_k_g, c_lam_q1, c_lam_k1, c_lam_q2, c_lam_k2, c_subln_g, c_w_out):
    bsz, s, _ = x.shape
    assert c.shape[0] + 1 <= 16
    scale = HEAD_DIM ** -0.5

    cvec = jnp.zeros((16, D_MODEL), _F32).at[:bsz].set(c).at[bsz].set(c_ctx)
    mod = _modulation(cvec, ada_w, ada_b)[:, :bsz + 1]
    sh, sc, gt = jnp.split(mod, 3, axis=-1)
    acol = _col(norm_g[:, None, :] * (1.0 + sc))
    shcol = _col(sh)
    gtcol = _col(gt)

    cos_t, sin_t = _rope_tables(s)
    xt = jnp.swapaxes(jnp.concatenate([ctx, x], axis=1), 1, 2)

    for i in range(DEPTH):
        kind, j = i % 3, i // 3
        if kind == 0:
            w_in, w_out, q_g, k_g = a_w_in[j], a_w_out[j], a_q_g[j], a_k_g[j]
            dims = dict(n_q=A_HEADS * HEAD_DIM, n_k=A_KV_HEADS * HEAD_DIM, n_v=A_KV_HEADS * HEAD_DIM, rope=True)
        elif kind == 1:
            w_in, w_out, q_g, k_g = b_w_in[j], b_w_out[j], b_q_g[j], b_k_g[j]
            dims = dict(n_q=MIX_WIDTH, n_k=MIX_WIDTH, n_v=MIX_WIDTH, rope=False)
        else:
            w_in, w_out, q_g, k_g = c_w_in[j], c_w_out[j], c_q_g[j], c_k_g[j]
            dims = dict(n_q=MIX_WIDTH, n_k=MIX_WIDTH, n_v=MIX_WIDTH, rope=True)
        qscale = scale if kind == 1 else scale * LOG2E
        qt, kt, vt, zt = _in_proj(xt, acol[i], shcol[i], w_in.T.astype(_BF16), cos_t, sin_t,
                                  _col(q_g * qscale), _col(k_g), **dims)
        if kind == 0:
            ot = _gqa_attention(qt, kt, vt)
        elif kind == 1:
            ot = _nbr_attention(qt, kt, vt, _nbr_bias(b_rpb[j]))
        else:
            lamv = jnp.zeros((8, LANES), _F32).at[0:4, :HEAD_DIM].set(
                jnp.stack([c_lam_q1[j], c_lam_k1[j], c_lam_q2[j], c_lam_k2[j]]).astype(_F32))
            ot = _diff_attention(qt, kt, vt, lamv, _col(c_subln_g[j]), _lambda_init(i))
        xt = _out_proj(ot, zt, w_out.T.astype(_BF16), gtcol[i], xt)

    return jnp.swapaxes(xt[:, :, CTX_LEN:], 1, 2)
```

```python
import functools
import math

import jax
import jax.numpy as jnp
import numpy as np
from jax import lax
from jax.experimental import pallas as pl
from jax.experimental.pallas import tpu as pltpu

D_MODEL = 1024
DEPTH = 4
GRID_W = 64
CTX_LEN = 256
HEAD_DIM = 64
MIX_WIDTH = 1024
A_HEADS = 16
A_KV_HEADS = 4
B_HEADS = 16
C_HEADS = 8
C_V_DIM = 128
WIN_H = 8
WIN_W = 16
ROPE_BASE = 10000.0
EPS = 1e-6
NEG_INF = -1e30
LOG2E = 1.4426950408889634

LANES = 128
TOK = 256
SLOT = 4
VMEM_LIMIT = 56 * 1024 * 1024

_BF16 = jnp.bfloat16
_F32 = jnp.float32


def _cparams(*sem):
    return pltpu.CompilerParams(dimension_semantics=sem, vmem_limit_bytes=VMEM_LIMIT)


def _mod_kernel(c_ref, w_ref, b_ref, o_ref):
    cv = c_ref[...]
    s = cv * jax.nn.sigmoid(cv)
    o_ref[0] = jnp.dot(s, w_ref[0], preferred_element_type=_F32) + b_ref[0]


def _modulation(cvec, ada_w, ada_b):
    n_col = 512
    return pl.pallas_call(
        _mod_kernel,
        grid=(DEPTH, 3 * D_MODEL // n_col),
        in_specs=[
            pl.BlockSpec((16, D_MODEL), lambda l, n: (0, 0)),
            pl.BlockSpec((1, D_MODEL, n_col), lambda l, n: (l, 0, n)),
            pl.BlockSpec((1, 1, n_col), lambda l, n: (l, 0, n)),
        ],
        out_specs=pl.BlockSpec((1, 16, n_col), lambda l, n: (l, 0, n)),
        out_shape=jax.ShapeDtypeStruct((DEPTH, 16, 3 * D_MODEL), _F32),
        compiler_params=_cparams("arbitrary", "arbitrary"),
        name="adaln_mod",
    )(cvec, ada_w, ada_b.reshape(DEPTH, 1, 3 * D_MODEL))


def _lane_tile(col_ref_val):
    return jnp.concatenate([col_ref_val] * (TOK // LANES), axis=1)


def _head_norm_rope(u, gain, cos, sin, rope):
    ms = jnp.sum(u * u, axis=0, keepdims=True) * (1.0 / HEAD_DIM)
    y = u * lax.rsqrt(ms + EPS) * gain
    if not rope:
        return y
    x1r, x2r, x1c, x2c = y[0:16], y[16:32], y[32:48], y[48:64]
    cr, cc = cos[0:16], cos[16:32]
    sr, sc = sin[0:16], sin[16:32]
    return jnp.concatenate(
        [x1r * cr - x2r * sr, x2r * cr + x1r * sr, x1c * cc - x2c * sc, x2c * cc + x1c * sc], axis=0)


def _proj_kernel(x_ref, a_ref, sh_ref, w_ref, cos_ref, sin_ref, qg_ref, kg_ref,
                 q_ref, k_ref, v_ref, g_ref, *, n_q, n_k, n_v, rope):
    x = x_ref[0]
    ms = jnp.sum(x * x, axis=0, keepdims=True) * (1.0 / D_MODEL)
    h = x * lax.rsqrt(ms + EPS) * _lane_tile(a_ref[0]) + _lane_tile(sh_ref[0])
    hb = h.astype(_BF16)
    cos, sin = cos_ref[...], sin_ref[...]
    qg, kg = _lane_tile(qg_ref[...]), _lane_tile(kg_ref[...])

    uq = jnp.dot(w_ref[0:n_q, :], hb, preferred_element_type=_F32)
    for hd in range(n_q // HEAD_DIM):
        r = slice(hd * HEAD_DIM, (hd + 1) * HEAD_DIM)
        q_ref[0, r, :] = _head_norm_rope(uq[r], qg, cos, sin, rope).astype(_BF16)

    uk = jnp.dot(w_ref[n_q:n_q + n_k, :], hb, preferred_element_type=_F32)
    for blk in range(n_k // TOK):
        heads = [_head_norm_rope(uk[blk * TOK + s * HEAD_DIM: blk * TOK + (s + 1) * HEAD_DIM],
                                 kg, cos, sin, rope) for s in range(SLOT)]
        kt = jnp.concatenate(heads, axis=0)
        k_ref[0, 0, blk] = kt.T.astype(_BF16)

    o = n_q + n_k
    uv = jnp.dot(w_ref[o:o + n_v, :], hb, preferred_element_type=_F32)
    v_ref[0, 0] = uv.astype(_BF16)

    o = o + n_v
    uz = jnp.dot(w_ref[o:o + MIX_WIDTH, :], hb, preferred_element_type=_F32)
    g_ref[0] = (uz * jax.nn.sigmoid(uz)).astype(_BF16)


def _in_proj(xt, acol, shcol, w_t, cos_t, sin_t, qg, kg, *, n_q, n_k, n_v, rope):
    bsz, _, t = xt.shape
    nt = t // TOK
    n_in = w_t.shape[0]
    col = lambda b, j: (jnp.where(j == 0, bsz, b), 0, 0)
    return pl.pallas_call(
        functools.partial(_proj_kernel, n_q=n_q, n_k=n_k, n_v=n_v, rope=rope),
        grid=(bsz, nt),
        in_specs=[
            pl.BlockSpec((1, D_MODEL, TOK), lambda b, j: (b, 0, j)),
            pl.BlockSpec((1, D_MODEL, LANES), col),
            pl.BlockSpec((1, D_MODEL, LANES), col),
            pl.BlockSpec((n_in, D_MODEL), lambda b, j: (0, 0)),
            pl.BlockSpec((32, TOK), lambda b, j: (0, j)),
            pl.BlockSpec((32, TOK), lambda b, j: (0, j)),
            pl.BlockSpec((HEAD_DIM, LANES), lambda b, j: (0, 0)),
            pl.BlockSpec((HEAD_DIM, LANES), lambda b, j: (0, 0)),
        ],
        out_specs=[
            pl.BlockSpec((1, n_q, TOK), lambda b, j: (b, 0, j)),
            pl.BlockSpec((1, 1, n_k // TOK, TOK, TOK), lambda b, j: (b, j, 0, 0, 0)),
            pl.BlockSpec((1, 1, n_v, TOK), lambda b, j: (b, j, 0, 0)),
            pl.BlockSpec((1, MIX_WIDTH, TOK), lambda b, j: (b, 0, j)),
        ],
        out_shape=[
            jax.ShapeDtypeStruct((bsz, n_q, t), _BF16),
            jax.ShapeDtypeStruct((bsz, nt, n_k // TOK, TOK, TOK), _BF16),
            jax.ShapeDtypeStruct((bsz, nt, n_v, TOK), _BF16),
            jax.ShapeDtypeStruct((bsz, MIX_WIDTH, t), _BF16),
        ],
        compiler_params=_cparams("arbitrary", "arbitrary"),
        name="in_proj",
    )(xt, acol, shcol, w_t, cos_t, sin_t, qg, kg)


def _pad_slot(qh, slot):
    z = jnp.zeros_like(qh)
    return jnp.concatenate([qh if s == slot else z for s in range(SLOT)], axis=0)


NCOL = SLOT * TOK
PIPE_UNROLL = 16
ONES_ROWS = 16


def _score_stage(k_tile, qcat_ref, s_ref, mx_ref, sb, bias=None, keep=None):
    s = jnp.dot(k_tile, qcat_ref[...], preferred_element_type=_F32)
    if bias is not None:
        s = jnp.where(keep, s + bias, NEG_INF)
    s_ref[sb] = s
    mx_ref[sb] = jnp.max(s, axis=0, keepdims=True)


def _softmax_value_stage(v_tile, s_ref, mx_ref, m_ref, acc_ref, sb):
    m_old = m_ref[...]
    m_new = jnp.maximum(m_old, mx_ref[sb])
    alpha = jnp.exp2(m_old - m_new)
    m_ref[...] = m_new
    for j in range(SLOT):
        cols = slice(j * TOK, (j + 1) * TOK)
        v = v_tile(j)
        v_aug = jnp.concatenate([v, jnp.ones((ONES_ROWS, v.shape[1]), _BF16)], axis=0)
        p = jnp.exp2(s_ref[sb, :, cols] - m_new[:, cols]).astype(_BF16)
        acc_ref[:, cols] = (alpha[:, cols] * acc_ref[:, cols]
                            + jnp.dot(v_aug, p, preferred_element_type=_F32))


def _attend(n_kt, k_tile, v_tile, qcat_ref, s_ref, mx_ref, m_ref, acc_ref):
    m_ref[...] = jnp.full(m_ref.shape, NEG_INF, _F32)
    acc_ref[...] = jnp.zeros(acc_ref.shape, _F32)
    _score_stage(k_tile(0), qcat_ref, s_ref, mx_ref, 0)

    def slots(u, carry):
        for j in range(PIPE_UNROLL):
            k = PIPE_UNROLL * u + j
            par = j % 2
            _score_stage(k_tile(k + 1), qcat_ref, s_ref, mx_ref, 1 - par)
            _softmax_value_stage(functools.partial(v_tile, k), s_ref, mx_ref, m_ref, acc_ref, par)
        return carry

    lax.fori_loop(0, (n_kt - 1) // PIPE_UNROLL, slots, 0)
    _softmax_value_stage(functools.partial(v_tile, n_kt - 1), s_ref, mx_ref, m_ref, acc_ref, 0)


def _attend_scratch(dv):
    return [
        pltpu.VMEM((SLOT * HEAD_DIM, NCOL), _BF16),
        pltpu.VMEM((2, TOK, NCOL), _F32),
        pltpu.VMEM((2, 1, NCOL), _F32),
        pltpu.VMEM((1, NCOL), _F32),
        pltpu.VMEM((dv + ONES_ROWS, NCOL), _F32),
    ]


def _gqa_kernel(q_ref, k_ref, v_ref, o_ref, qcat_ref, s_ref, mx_ref, m_ref, acc_ref):
    kvh = pl.program_id(1)
    n_kt = jnp.where(pl.program_id(2) == 0, 1, k_ref.shape[1])
    row_slot = lax.broadcasted_iota(jnp.int32, (SLOT * HEAD_DIM, TOK), 0) // HEAD_DIM
    for g in range(SLOT):
        qh = q_ref[0, g * HEAD_DIM:(g + 1) * HEAD_DIM, :].astype(_F32)
        qrep = jnp.concatenate([qh] * SLOT, axis=0)
        qcat_ref[:, g * TOK:(g + 1) * TOK] = jnp.where(row_slot == kvh, qrep, 0.0).astype(_BF16)

    _attend(n_kt, lambda t: k_ref[0, t, 0], lambda t, j: v_ref[0, t],
            qcat_ref, s_ref, mx_ref, m_ref, acc_ref)
    o = acc_ref[0:HEAD_DIM, :] / acc_ref[HEAD_DIM:HEAD_DIM + 1, :]
    for g in range(SLOT):
        o_ref[0, g * HEAD_DIM:(g + 1) * HEAD_DIM, :] = o[:, g * TOK:(g + 1) * TOK].astype(_BF16)


def _gqa_attention(qt, kt, vt):
    bsz, _, t = qt.shape
    nt = t // TOK
    assert nt % 2 == 1
    return pl.pallas_call(
        _gqa_kernel,
        grid=(bsz, A_KV_HEADS, nt),
        in_specs=[
            pl.BlockSpec((1, SLOT * HEAD_DIM, TOK), lambda b, h, j: (b, h, j)),
            pl.BlockSpec((1, nt, 1, TOK, TOK), lambda b, h, j: (b, 0, 0, 0, 0)),
            pl.BlockSpec((1, nt, HEAD_DIM, TOK), lambda b, h, j: (b, 0, h, 0)),
        ],
        out_specs=pl.BlockSpec((1, SLOT * HEAD_DIM, TOK), lambda b, h, j: (b, h, j)),
        out_shape=jax.ShapeDtypeStruct((bsz, MIX_WIDTH, t), _BF16),
        scratch_shapes=_attend_scratch(HEAD_DIM),
        compiler_params=_cparams("arbitrary", "arbitrary", "arbitrary"),
        name="gqa_attn",
    )(qt, kt, vt)


def _diff_kernel(q_ref, k_ref, v_ref, lam_ref, sg_ref, o_ref, qcat_ref, s_ref, mx_ref, m_ref, acc_ref,
                 *, out_scale, lam_init):
    n_kt = jnp.where(pl.program_id(2) == 0, 1, k_ref.shape[1])
    for j in range(SLOT):
        qcat_ref[:, j * TOK:(j + 1) * TOK] = _pad_slot(q_ref[0, j * HEAD_DIM:(j + 1) * HEAD_DIM, :], j)

    _attend(n_kt, lambda t: k_ref[0, t, 0],
            lambda t, j: v_ref[0, t, (j // 2) * C_V_DIM:(j // 2 + 1) * C_V_DIM, :],
            qcat_ref, s_ref, mx_ref, m_ref, acc_ref)
    lv = lam_ref[...]
    lam = (jnp.exp(jnp.sum(lv[0:1] * lv[1:2], axis=1, keepdims=True))
           - jnp.exp(jnp.sum(lv[2:3] * lv[3:4], axis=1, keepdims=True)) + lam_init)
    sg = _lane_tile(sg_ref[...])
    attn = acc_ref[0:C_V_DIM, :] / acc_ref[C_V_DIM:C_V_DIM + 1, :]
    for hh in range(2):
        o = attn[:, 2 * hh * TOK:(2 * hh + 1) * TOK] - lam * attn[:, (2 * hh + 1) * TOK:(2 * hh + 2) * TOK]
        ms = jnp.sum(o * o, axis=0, keepdims=True) * (1.0 / C_V_DIM)
        y = o * lax.rsqrt(ms + EPS) * sg * out_scale
        o_ref[0, hh * C_V_DIM:(hh + 1) * C_V_DIM, :] = y.astype(_BF16)


def _diff_attention(qt, kt, vt, lamv, sg, lam_init):
    bsz, _, t = qt.shape
    nt = t // TOK
    n_blk = kt.shape[2]
    return pl.pallas_call(
        functools.partial(_diff_kernel, out_scale=1.0 - lam_init, lam_init=lam_init),
        grid=(bsz, n_blk, nt),
        in_specs=[
            pl.BlockSpec((1, SLOT * HEAD_DIM, TOK), lambda b, c, j: (b, c, j)),
            pl.BlockSpec((1, nt, 1, TOK, TOK), lambda b, c, j: (b, 0, c, 0, 0)),
            pl.BlockSpec((1, nt, 2 * C_V_DIM, TOK), lambda b, c, j: (b, 0, c, 0)),
            pl.BlockSpec((8, LANES), lambda b, c, j: (0, 0)),
            pl.BlockSpec((C_V_DIM, LANES), lambda b, c, j: (0, 0)),
        ],
        out_specs=pl.BlockSpec((1, 2 * C_V_DIM, TOK), lambda b, c, j: (b, c, j)),
        out_shape=jax.ShapeDtypeStruct((bsz, MIX_WIDTH, t), _BF16),
        scratch_shapes=_attend_scratch(C_V_DIM),
        compiler_params=_cparams("arbitrary", "arbitrary", "arbitrary"),
        name="diff_attn",
    )(qt, kt, vt, lamv, sg)


N_BAND = 3


def _nbr_kernel(q_ref, kc_ref, k0_ref, k1_ref, k2_ref, vc_ref, v0_ref, v1_ref, v2_ref, bias_ref, o_ref,
                qcat_ref, s_ref, mx_ref, m_ref, acc_ref):
    is_lat = pl.program_id(0) > 0
    k_refs = (kc_ref, k0_ref, k1_ref, k2_ref)
    v_refs = (vc_ref, v0_ref, v1_ref, v2_ref)
    n_tiles = len(k_refs)

    def quad(c, carry):
        row0 = pl.multiple_of(c * (SLOT * HEAD_DIM), SLOT * HEAD_DIM)
        hrow = [pl.ds(pl.multiple_of(row0 + g * HEAD_DIM, HEAD_DIM), HEAD_DIM) for g in range(SLOT)]
        for g in range(SLOT):
            qcat_ref[:, g * TOK:(g + 1) * TOK] = _pad_slot(q_ref[0, hrow[g], :], g)
        m_ref[...] = jnp.full(m_ref.shape, NEG_INF, _F32)
        acc_ref[...] = jnp.zeros(acc_ref.shape, _F32)

        def score(t):
            bias = None if t == 0 else bias_ref[0, c, t - 1]
            _score_stage(k_refs[t][0, 0, c], qcat_ref, s_ref, mx_ref, t % 2, bias=bias, keep=is_lat)

        score(0)
        for t in range(n_tiles):
            if t + 1 < n_tiles:
                score(t + 1)
            _softmax_value_stage(lambda g, t=t: v_refs[t][0, 0, hrow[g], :], s_ref, mx_ref, m_ref, acc_ref, t % 2)
        o = acc_ref[0:HEAD_DIM, :] / acc_ref[HEAD_DIM:HEAD_DIM + 1, :]
        for g in range(SLOT):
            o_ref[0, hrow[g], :] = o[:, g * TOK:(g + 1) * TOK].astype(_BF16)
        return carry

    lax.fori_loop(0, B_HEADS // SLOT, quad, 0)


def _nbr_attention(qt, kt, vt, bias):
    bsz, _, t = qt.shape
    nt = t // TOK
    n_blk = kt.shape[2]
    n_lat = nt - 1

    def band(d):
        return lambda j, b: (b, 1 + jnp.clip(j - 2, 0, n_lat - N_BAND) + d, 0, 0, 0)

    def vband(d):
        return lambda j, b: (b, 1 + jnp.clip(j - 2, 0, n_lat - N_BAND) + d, 0, 0)

    case = lambda j, b: (jnp.where(j <= 1, 0, jnp.where(j == nt - 1, 2, 1)), 0, 0, 0, 0)
    kspec = lambda im: pl.BlockSpec((1, 1, n_blk, TOK, TOK), im)
    vspec = lambda im: pl.BlockSpec((1, 1, MIX_WIDTH, TOK), im)
    return pl.pallas_call(
        _nbr_kernel,
        grid=(nt, bsz),
        in_specs=[
            pl.BlockSpec((1, MIX_WIDTH, TOK), lambda j, b: (b, 0, j)),
            kspec(lambda j, b: (b, 0, 0, 0, 0)), kspec(band(0)), kspec(band(1)), kspec(band(2)),
            vspec(lambda j, b: (b, 0, 0, 0)), vspec(vband(0)), vspec(vband(1)), vspec(vband(2)),
            pl.BlockSpec((1, B_HEADS // SLOT, N_BAND, TOK, NCOL), case),
        ],
        out_specs=pl.BlockSpec((1, MIX_WIDTH, TOK), lambda j, b: (b, 0, j)),
        out_shape=jax.ShapeDtypeStruct((bsz, MIX_WIDTH, t), _BF16),
        scratch_shapes=_attend_scratch(HEAD_DIM),
        compiler_params=_cparams("arbitrary", "arbitrary"),
        name="nbr_attn",
    )(qt, kt, kt, kt, kt, vt, vt, vt, vt, bias)


def _nbr_bias(rpb):
    rows = 4096 // GRID_W
    q_per = TOK // GRID_W
    hi = lax.Precision.HIGHEST
    kc = np.arange(GRID_W)[:, None]
    qc = np.arange(GRID_W)[None, :]
    d_col = np.clip(kc - qc + WIN_W - 1, 0, 2 * WIN_W - 2)
    cs = np.clip(qc - WIN_W // 2, 0, GRID_W - WIN_W)
    col_in = (kc >= cs) & (kc < cs + WIN_W)
    e_col = (d_col[None] == np.arange(2 * WIN_W - 1)[:, None, None]).astype(np.float32)
    by_col = jnp.einsum("hdj,jkq->hdkq", rpb.astype(_F32), e_col, precision=hi)
    tables = []
    for i in (0, 1, rows // q_per - 1):
        s = min(max(i - 1, 0), rows // q_per - N_BAND)
        kr = (q_per * s + np.arange(N_BAND * q_per))[:, None]
        qr = (q_per * i + np.arange(q_per))[None, :]
        d_row = np.clip(kr - qr + WIN_H - 1, 0, 2 * WIN_H - 2)
        rs = np.clip(qr - WIN_H // 2, 0, rows - WIN_H)
        row_in = (kr >= rs) & (kr < rs + WIN_H)
        e_row = (d_row[None] == np.arange(2 * WIN_H - 1)[:, None, None]).astype(np.float32)
        tab = jnp.einsum("dab,hdkq->hakbq", e_row, by_col, precision=hi)
        inside = row_in[:, None, :, None] & col_in[None, :, None, :]
        tab = jnp.where(inside[None], tab * LOG2E, NEG_INF)
        tab = tab.reshape(B_HEADS // SLOT, SLOT, N_BAND, q_per, GRID_W, q_per, GRID_W)
        tables.append(tab.transpose(0, 2, 3, 4, 1, 5, 6).reshape(B_HEADS // SLOT, N_BAND, TOK, NCOL))
    return jnp.stack(tables, axis=0)


def _out_kernel(o_ref, g_ref, w_ref, gt_ref, x_ref, y_ref):
    og = (o_ref[0].astype(_F32) * g_ref[0].astype(_F32)).astype(_BF16)
    y = jnp.dot(w_ref[...], og, preferred_element_type=_F32)
    y_ref[0] = x_ref[0] + _lane_tile(gt_ref[0]) * y


def _out_proj(ot, gt, w_t, gtcol, xt):
    bsz, _, t = xt.shape
    nt = t // TOK
    col = lambda b, j: (jnp.where(j == 0, bsz, b), 0, 0)
    tile = lambda b, j: (b, 0, j)
    return pl.pallas_call(
        _out_kernel,
        grid=(bsz, nt),
        in_specs=[
            pl.BlockSpec((1, MIX_WIDTH, TOK), tile),
            pl.BlockSpec((1, MIX_WIDTH, TOK), tile),
            pl.BlockSpec((D_MODEL, MIX_WIDTH), lambda b, j: (0, 0)),
            pl.BlockSpec((1, D_MODEL, LANES), col),
            pl.BlockSpec((1, D_MODEL, TOK), tile),
        ],
        out_specs=pl.BlockSpec((1, D_MODEL, TOK), tile),
        out_shape=jax.ShapeDtypeStruct(xt.shape, _F32),
        input_output_aliases={4: 0},
        compiler_params=_cparams("arbitrary", "arbitrary"),
        name="out_proj",
    )(ot, gt, w_t, gtcol, xt)


def _rope_tables(s):
    tok = jnp.arange(s, dtype=jnp.int32)
    rows = (tok // GRID_W).astype(_F32)
    cols = (tok % GRID_W).astype(_F32)
    n_freq = HEAD_DIM // 4
    inv_freq = ROPE_BASE ** (-jnp.arange(n_freq, dtype=_F32) / n_freq)
    ang = jnp.concatenate([rows[:, None] * inv_freq, cols[:, None] * inv_freq], axis=1)
    ang = jnp.concatenate([jnp.zeros((CTX_LEN, 32), _F32), ang], axis=0)
    return jnp.cos(ang).T, jnp.sin(ang).T


def _col(v):
    return jnp.broadcast_to(v[..., None], v.shape + (LANES,))


def _lambda_init(layer):
    return 0.8 - 0.6 * math.exp(-0.3 * layer)


def kernel(x, c, ctx, c_ctx, norm_g, ada_w, ada_b, a_w_in, a_q_g, a_k_g, a_w_out, b_w_in, b_q_g, b_k_g, b_rpb, b_w_out, c_w_in, c_q_g, c_k_g, c_lam_q1, c_lam_k1, c_lam_q2, c_lam_k2, c_subln_g, c_w_out):
    bsz, s, _ = x.shape
    assert c.shape[0] + 1 <= 16
    scale = HEAD_DIM ** -0.5

    cvec = jnp.zeros((16, D_MODEL), _F32).at[:bsz].set(c).at[bsz].set(c_ctx)
    mod = _modulation(cvec, ada_w, ada_b)[:, :bsz + 1]
    sh, sc, gt = jnp.split(mod, 3, axis=-1)
    acol = _col(norm_g[:, None, :] * (1.0 + sc))
    shcol = _col(sh)
    gtcol = _col(gt)

    cos_t, sin_t = _rope_tables(s)
    xt = jnp.swapaxes(jnp.concatenate([ctx, x], axis=1), 1, 2)

    for i in range(DEPTH):
        kind, j = i % 3, i // 3
        if kind == 0:
            w_in, w_out, q_g, k_g = a_w_in[j], a_w_out[j], a_q_g[j], a_k_g[j]
            dims = dict(n_q=A_HEADS * HEAD_DIM, n_k=A_KV_HEADS * HEAD_DIM, n_v=A_KV_HEADS * HEAD_DIM, rope=True)
        elif kind == 1:
            w_in, w_out, q_g, k_g = b_w_in[j], b_w_out[j], b_q_g[j], b_k_g[j]
            dims = dict(n_q=MIX_WIDTH, n_k=MIX_WIDTH, n_v=MIX_WIDTH, rope=False)
        else:
            w_in, w_out, q_g, k_g = c_w_in[j], c_w_out[j], c_q_g[j], c_k_g[j]
            dims = dict(n_q=MIX_WIDTH, n_k=MIX_WIDTH, n_v=MIX_WIDTH, rope=True)
        qscale = scale * LOG2E
        qt, kt, vt, zt = _in_proj(xt, acol[i], shcol[i], w_in.T.astype(_BF16), cos_t, sin_t,
                                  _col(q_g * qscale), _col(k_g), **dims)
        if kind == 0:
            ot = _gqa_attention(qt, kt, vt)
        elif kind == 1:
            ot = _nbr_attention(qt, kt, vt, _nbr_bias(b_rpb[j]))
        else:
            lamv = jnp.zeros((8, LANES), _F32).at[0:4, :HEAD_DIM].set(
                jnp.stack([c_lam_q1[j], c_lam_k1[j], c_lam_q2[j], c_lam_k2[j]]).astype(_F32))
            ot = _diff_attention(qt, kt, vt, lamv, _col(c_subln_g[j]), _lambda_init(i))
        xt = _out_proj(ot, zt, w_out.T.astype(_BF16), gtcol[i], xt)

    return jnp.swapaxes(xt[:, :, CTX_LEN:], 1, 2)
```

```python
import functools
import math

import jax
import jax.numpy as jnp
import numpy as np
from jax import lax
from jax.experimental import pallas as pl
from jax.experimental.pallas import tpu as pltpu

D_MODEL = 1024
DEPTH = 4
GRID_W = 64
CTX_LEN = 256
HEAD_DIM = 64
MIX_WIDTH = 1024
A_HEADS = 16
A_KV_HEADS = 4
B_HEADS = 16
C_HEADS = 8
C_V_DIM = 128
WIN_H = 8
WIN_W = 16
ROPE_BASE = 10000.0
EPS = 1e-6
NEG_INF = -1e30
LOG2E = 1.4426950408889634

LANES = 128
TOK = 256
SLOT = 4
VMEM_LIMIT = 56 * 1024 * 1024

_BF16 = jnp.bfloat16
_F32 = jnp.float32


def _cparams(*sem):
    return pltpu.CompilerParams(dimension_semantics=sem, vmem_limit_bytes=VMEM_LIMIT)


def _mod_kernel(c_ref, w_ref, b_ref, o_ref):
    cv = c_ref[...]
    s = cv * jax.nn.sigmoid(cv)
    o_ref[0] = jnp.dot(s, w_ref[0], preferred_element_type=_F32) + b_ref[0]


def _modulation(cvec, ada_w, ada_b):
    n_col = 512
    return pl.pallas_call(
        _mod_kernel,
        grid=(DEPTH, 3 * D_MODEL // n_col),
        in_specs=[
            pl.BlockSpec((16, D_MODEL), lambda l, n: (0, 0)),
            pl.BlockSpec((1, D_MODEL, n_col), lambda l, n: (l, 0, n)),
            pl.BlockSpec((1, 1, n_col), lambda l, n: (l, 0, n)),
        ],
        out_specs=pl.BlockSpec((1, 16, n_col), lambda l, n: (l, 0, n)),
        out_shape=jax.ShapeDtypeStruct((DEPTH, 16, 3 * D_MODEL), _F32),
        compiler_params=_cparams("arbitrary", "arbitrary"),
        name="adaln_mod",
    )(cvec, ada_w, ada_b.reshape(DEPTH, 1, 3 * D_MODEL))


def _lane_tile(col_ref_val):
    return jnp.concatenate([col_ref_val] * (TOK // LANES), axis=1)


def _head_norm_rope(u, gain, cos, sin, rope):
    ms = jnp.sum(u * u, axis=0, keepdims=True) * (1.0 / HEAD_DIM)
    y = u * lax.rsqrt(ms + EPS) * gain
    if not rope:
        return y
    x1r, x2r, x1c, x2c = y[0:16], y[16:32], y[32:48], y[48:64]
    cr, cc = cos[0:16], cos[16:32]
    sr, sc = sin[0:16], sin[16:32]
    return jnp.concatenate(
        [x1r * cr - x2r * sr, x2r * cr + x1r * sr, x1c * cc - x2c * sc, x2c * cc + x1c * sc], axis=0)


def _proj_kernel(x_ref, a_ref, sh_ref, w_ref, cos_ref, sin_ref, qg_ref, kg_ref,
                 q_ref, k_ref, v_ref, g_ref, *, n_q, n_k, n_v, rope):
    x = x_ref[0]
    ms = jnp.sum(x * x, axis=0, keepdims=True) * (1.0 / D_MODEL)
    h = x * lax.rsqrt(ms + EPS) * _lane_tile(a_ref[0]) + _lane_tile(sh_ref[0])
    hb = h.astype(_BF16)
    cos, sin = cos_ref[...], sin_ref[...]
    qg, kg = _lane_tile(qg_ref[...]), _lane_tile(kg_ref[...])

    uq = jnp.dot(w_ref[0:n_q, :], hb, preferred_element_type=_F32)
    for hd in range(n_q // HEAD_DIM):
        r = slice(hd * HEAD_DIM, (hd + 1) * HEAD_DIM)
        q_ref[0, 0, r, :] = _head_norm_rope(uq[r], qg, cos, sin, rope).astype(_BF16)

    uk = jnp.dot(w_ref[n_q:n_q + n_k, :], hb, preferred_element_type=_F32)
    for blk in range(n_k // TOK):
        heads = [_head_norm_rope(uk[blk * TOK + s * HEAD_DIM: blk * TOK + (s + 1) * HEAD_DIM],
                                 kg, cos, sin, rope) for s in range(SLOT)]
        kt = jnp.concatenate(heads, axis=0)
        k_ref[0, 0, blk] = kt.T.astype(_BF16)

    o = n_q + n_k
    uv = jnp.dot(w_ref[o:o + n_v, :], hb, preferred_element_type=_F32)
    v_ref[0, 0] = uv.astype(_BF16)

    o = o + n_v
    uz = jnp.dot(w_ref[o:o + MIX_WIDTH, :], hb, preferred_element_type=_F32)
    g_ref[0] = (uz * jax.nn.sigmoid(uz)).astype(_BF16)


def _in_proj(xt, acol, shcol, w_t, cos_t, sin_t, qg, kg, *, n_q, n_k, n_v, rope):
    bsz, _, t = xt.shape
    nt = t // TOK
    n_in = w_t.shape[0]
    col = lambda b, j: (jnp.where(j == 0, bsz, b), 0, 0)
    return pl.pallas_call(
        functools.partial(_proj_kernel, n_q=n_q, n_k=n_k, n_v=n_v, rope=rope),
        grid=(bsz, nt),
        in_specs=[
            pl.BlockSpec((1, D_MODEL, TOK), lambda b, j: (b, 0, j)),
            pl.BlockSpec((1, D_MODEL, LANES), col),
            pl.BlockSpec((1, D_MODEL, LANES), col),
            pl.BlockSpec((n_in, D_MODEL), lambda b, j: (0, 0)),
            pl.BlockSpec((32, TOK), lambda b, j: (0, j)),
            pl.BlockSpec((32, TOK), lambda b, j: (0, j)),
            pl.BlockSpec((HEAD_DIM, LANES), lambda b, j: (0, 0)),
            pl.BlockSpec((HEAD_DIM, LANES), lambda b, j: (0, 0)),
        ],
        out_specs=[
            pl.BlockSpec((1, 1, n_q, TOK), lambda b, j: (b, j, 0, 0)),
            pl.BlockSpec((1, 1, n_k // TOK, TOK, TOK), lambda b, j: (b, j, 0, 0, 0)),
            pl.BlockSpec((1, 1, n_v, TOK), lambda b, j: (b, j, 0, 0)),
            pl.BlockSpec((1, MIX_WIDTH, TOK), lambda b, j: (b, 0, j)),
        ],
        out_shape=[
            jax.ShapeDtypeStruct((bsz, nt, n_q, TOK), _BF16),
            jax.ShapeDtypeStruct((bsz, nt, n_k // TOK, TOK, TOK), _BF16),
            jax.ShapeDtypeStruct((bsz, nt, n_v, TOK), _BF16),
            jax.ShapeDtypeStruct((bsz, MIX_WIDTH, t), _BF16),
        ],
        compiler_params=_cparams("arbitrary", "arbitrary"),
        name="in_proj",
    )(xt, acol, shcol, w_t, cos_t, sin_t, qg, kg)


def _pad_slot(qh, slot):
    z = jnp.zeros_like(qh)
    return jnp.concatenate([qh if s == slot else z for s in range(SLOT)], axis=0)


NCOL = SLOT * TOK
ONES_ROWS = 16


def _score_stage(k_tile, qcat_ref, s_ref, mx_ref, sb, bias=None, keep=None):
    s = jnp.dot(k_tile, qcat_ref[...], preferred_element_type=_F32)
    if bias is not None:
        s = jnp.where(keep, s + bias, NEG_INF)
    s_ref[sb] = s
    mx_ref[sb] = jnp.max(s, axis=0, keepdims=True)


def _softmax_value_stage(v_tile, s_ref, mx_ref, m_ref, acc_ref, sb):
    m_old = m_ref[...]
    m_new = jnp.maximum(m_old, mx_ref[sb])
    alpha = jnp.exp2(m_old - m_new)
    m_ref[...] = m_new
    for j in range(SLOT):
        cols = slice(j * TOK, (j + 1) * TOK)
        v = v_tile(j)
        v_aug = jnp.concatenate([v, jnp.ones((ONES_ROWS, v.shape[1]), _BF16)], axis=0)
        p = jnp.exp2(s_ref[sb, :, cols] - m_new[:, cols]).astype(_BF16)
        acc_ref[:, cols] = (alpha[:, cols] * acc_ref[:, cols]
                            + jnp.dot(v_aug, p, preferred_element_type=_F32))


LAST_BUF = 2


def _attend_tiles(n_t, build_qcat, k_tile, v_tile, finalize, qcat_ref, s_ref, mx_ref, m_ref, acc_ref):
    stage = functools.partial(_softmax_value_stage, s_ref=s_ref, mx_ref=mx_ref, m_ref=m_ref, acc_ref=acc_ref)
    score = functools.partial(_score_stage, qcat_ref=qcat_ref, s_ref=s_ref, mx_ref=mx_ref)

    def reset():
        m_ref[...] = jnp.full(m_ref.shape, NEG_INF, _F32)
        acc_ref[...] = jnp.zeros(acc_ref.shape, _F32)

    def finish(j, last):
        stage(functools.partial(v_tile, last), sb=LAST_BUF)
        finalize(j)
        reset()

    reset()
    build_qcat(0)
    score(k_tile(0), sb=LAST_BUF)

    def tile(j, carry):
        finish(j - 1, jnp.where(j == 1, 0, n_t - 1))
        build_qcat(j)
        score(k_tile(0), sb=0)
        for k in range(n_t - 1):
            score(k_tile(k + 1), sb=(k + 1) % 2 if k + 2 < n_t else LAST_BUF)
            stage(functools.partial(v_tile, k), sb=k % 2)
        return carry

    lax.fori_loop(1, n_t, tile, 0)
    finish(n_t - 1, n_t - 1)


def _attend_scratch(dv):
    return [
        pltpu.VMEM((SLOT * HEAD_DIM, NCOL), _BF16),
        pltpu.VMEM((3, TOK, NCOL), _F32),
        pltpu.VMEM((3, 1, NCOL), _F32),
        pltpu.VMEM((1, NCOL), _F32),
        pltpu.VMEM((dv + ONES_ROWS, NCOL), _F32),
    ]


def _gqa_kernel(q_ref, k_ref, v_ref, o_ref, qcat_ref, s_ref, mx_ref, m_ref, acc_ref):
    kvh = pl.program_id(1)
    row_slot = lax.broadcasted_iota(jnp.int32, (SLOT * HEAD_DIM, TOK), 0) // HEAD_DIM

    def build_qcat(j):
        for g in range(SLOT):
            qh = q_ref[0, j, g * HEAD_DIM:(g + 1) * HEAD_DIM, :].astype(_F32)
            qrep = jnp.concatenate([qh] * SLOT, axis=0)
            qcat_ref[:, g * TOK:(g + 1) * TOK] = jnp.where(row_slot == kvh, qrep, 0.0).astype(_BF16)

    def finalize(j):
        o = acc_ref[0:HEAD_DIM, :] / acc_ref[HEAD_DIM:HEAD_DIM + 1, :]
        for g in range(SLOT):
            o_ref[0, j, g * HEAD_DIM:(g + 1) * HEAD_DIM, :] = o[:, g * TOK:(g + 1) * TOK].astype(_BF16)

    _attend_tiles(k_ref.shape[1], build_qcat, lambda t: k_ref[0, t, 0], lambda t, g: v_ref[0, t], finalize,
                  qcat_ref, s_ref, mx_ref, m_ref, acc_ref)


def _gqa_attention(qt, kt, vt):
    bsz, nt = qt.shape[:2]
    grp = SLOT * HEAD_DIM
    return pl.pallas_call(
        _gqa_kernel,
        grid=(bsz, A_KV_HEADS),
        in_specs=[
            pl.BlockSpec((1, nt, grp, TOK), lambda b, h: (b, 0, h, 0)),
            pl.BlockSpec((1, nt, 1, TOK, TOK), lambda b, h: (b, 0, 0, 0, 0)),
            pl.BlockSpec((1, nt, HEAD_DIM, TOK), lambda b, h: (b, 0, h, 0)),
        ],
        out_specs=pl.BlockSpec((1, nt, grp, TOK), lambda b, h: (b, 0, h, 0)),
        out_shape=jax.ShapeDtypeStruct((bsz, nt, MIX_WIDTH, TOK), _BF16),
        scratch_shapes=_attend_scratch(HEAD_DIM),
        compiler_params=_cparams("arbitrary", "arbitrary"),
        name="gqa_attn",
    )(qt, kt, vt)


def _diff_kernel(q_ref, k_ref, v_ref, lam_ref, sg_ref, o_ref, qcat_ref, s_ref, mx_ref, m_ref, acc_ref,
                 *, out_scale, lam_init):
    def build_qcat(j):
        for s in range(SLOT):
            qcat_ref[:, s * TOK:(s + 1) * TOK] = _pad_slot(q_ref[0, j, s * HEAD_DIM:(s + 1) * HEAD_DIM, :], s)

    def finalize(j):
        lv = lam_ref[...]
        lam = (jnp.exp(jnp.sum(lv[0:1] * lv[1:2], axis=1, keepdims=True))
               - jnp.exp(jnp.sum(lv[2:3] * lv[3:4], axis=1, keepdims=True)) + lam_init)
        sg = _lane_tile(sg_ref[...])
        attn = acc_ref[0:C_V_DIM, :] / acc_ref[C_V_DIM:C_V_DIM + 1, :]
        for hh in range(2):
            o = attn[:, 2 * hh * TOK:(2 * hh + 1) * TOK] - lam * attn[:, (2 * hh + 1) * TOK:(2 * hh + 2) * TOK]
            ms = jnp.sum(o * o, axis=0, keepdims=True) * (1.0 / C_V_DIM)
            y = o * lax.rsqrt(ms + EPS) * sg * out_scale
            o_ref[0, j, hh * C_V_DIM:(hh + 1) * C_V_DIM, :] = y.astype(_BF16)

    _attend_tiles(k_ref.shape[1], build_qcat, lambda t: k_ref[0, t, 0],
                  lambda t, s: v_ref[0, t, (s // 2) * C_V_DIM:(s // 2 + 1) * C_V_DIM, :], finalize,
                  qcat_ref, s_ref, mx_ref, m_ref, acc_ref)


def _diff_attention(qt, kt, vt, lamv, sg, lam_init):
    bsz, nt = qt.shape[:2]
    n_blk = kt.shape[2]
    grp = SLOT * HEAD_DIM
    return pl.pallas_call(
        functools.partial(_diff_kernel, out_scale=1.0 - lam_init, lam_init=lam_init),
        grid=(bsz, n_blk),
        in_specs=[
            pl.BlockSpec((1, nt, grp, TOK), lambda b, c: (b, 0, c, 0)),
            pl.BlockSpec((1, nt, 1, TOK, TOK), lambda b, c: (b, 0, c, 0, 0)),
            pl.BlockSpec((1, nt, 2 * C_V_DIM, TOK), lambda b, c: (b, 0, c, 0)),
            pl.BlockSpec((8, LANES), lambda b, c: (0, 0)),
            pl.BlockSpec((C_V_DIM, LANES), lambda b, c: (0, 0)),
        ],
        out_specs=pl.BlockSpec((1, nt, 2 * C_V_DIM, TOK), lambda b, c: (b, 0, c, 0)),
        out_shape=jax.ShapeDtypeStruct((bsz, nt, MIX_WIDTH, TOK), _BF16),
        scratch_shapes=_attend_scratch(C_V_DIM),
        compiler_params=_cparams("arbitrary", "arbitrary"),
        name="diff_attn",
    )(qt, kt, vt, lamv, sg)


N_BAND = 3


def _nbr_kernel(q_ref, kc_ref, k0_ref, k1_ref, k2_ref, vc_ref, v0_ref, v1_ref, v2_ref, bias_ref, o_ref,
                qcat_ref, s_ref, mx_ref, m_ref, acc_ref):
    is_lat = pl.program_id(0) > 0
    k_refs = (kc_ref, k0_ref, k1_ref, k2_ref)
    v_refs = (vc_ref, v0_ref, v1_ref, v2_ref)
    n_tiles = len(k_refs)
    n_quad = B_HEADS // SLOT
    head = lambda c, g: slice((c * SLOT + g) * HEAD_DIM, (c * SLOT + g + 1) * HEAD_DIM)

    def begin(c):
        for g in range(SLOT):
            qcat_ref[c % 2, :, g * TOK:(g + 1) * TOK] = _pad_slot(q_ref[0, 0, head(c, g), :], g)
        m_ref[c % 2] = jnp.full(m_ref.shape[1:], NEG_INF, _F32)
        acc_ref[c % 2] = jnp.zeros(acc_ref.shape[1:], _F32)

    def score(i):
        c, t = divmod(i, n_tiles)
        if t == 0:
            begin(c)
        bias = None if t == 0 else bias_ref[0, c, t - 1]
        _score_stage(k_refs[t][0, 0, c], qcat_ref.at[c % 2], s_ref, mx_ref, i % 2, bias=bias, keep=is_lat)

    score(0)
    for i in range(n_quad * n_tiles):
        c, t = divmod(i, n_tiles)
        if i + 1 < n_quad * n_tiles:
            score(i + 1)
        acc = acc_ref.at[c % 2]
        _softmax_value_stage(lambda g, c=c, t=t: v_refs[t][0, 0, head(c, g), :],
                             s_ref, mx_ref, m_ref.at[c % 2], acc, i % 2)
        if t == n_tiles - 1:
            o = acc[0:HEAD_DIM, :] / acc[HEAD_DIM:HEAD_DIM + 1, :]
            for g in range(SLOT):
                o_ref[0, 0, head(c, g), :] = o[:, g * TOK:(g + 1) * TOK].astype(_BF16)


def _nbr_attention(qt, kt, vt, bias):
    bsz, nt = qt.shape[:2]
    n_blk = kt.shape[2]
    n_lat = nt - 1
    tile = lambda j, b: (b, j, 0, 0)

    def band(d):
        return lambda j, b: (b, 1 + jnp.clip(j - 2, 0, n_lat - N_BAND) + d, 0, 0, 0)

    def vband(d):
        return lambda j, b: (b, 1 + jnp.clip(j - 2, 0, n_lat - N_BAND) + d, 0, 0)

    case = lambda j, b: (jnp.where(j <= 1, 0, jnp.where(j == nt - 1, 2, 1)), 0, 0, 0, 0)
    kspec = lambda im: pl.BlockSpec((1, 1, n_blk, TOK, TOK), im)
    vspec = lambda im: pl.BlockSpec((1, 1, MIX_WIDTH, TOK), im)
    return pl.pallas_call(
        _nbr_kernel,
        grid=(nt, bsz),
        in_specs=[
            pl.BlockSpec((1, 1, MIX_WIDTH, TOK), tile),
            kspec(lambda j, b: (b, 0, 0, 0, 0)), kspec(band(0)), kspec(band(1)), kspec(band(2)),
            vspec(lambda j, b: (b, 0, 0, 0)), vspec(vband(0)), vspec(vband(1)), vspec(vband(2)),
            pl.BlockSpec((1, B_HEADS // SLOT, N_BAND, TOK, NCOL), case),
        ],
        out_specs=pl.BlockSpec((1, 1, MIX_WIDTH, TOK), tile),
        out_shape=jax.ShapeDtypeStruct((bsz, nt, MIX_WIDTH, TOK), _BF16),
        scratch_shapes=[
            pltpu.VMEM((2, SLOT * HEAD_DIM, NCOL), _BF16),
            pltpu.VMEM((2, TOK, NCOL), _F32),
            pltpu.VMEM((2, 1, NCOL), _F32),
            pltpu.VMEM((2, 1, NCOL), _F32),
            pltpu.VMEM((2, HEAD_DIM + ONES_ROWS, NCOL), _F32),
        ],
        compiler_params=_cparams("arbitrary", "arbitrary"),
        name="nbr_attn",
    )(qt, kt, kt, kt, kt, vt, vt, vt, vt, bias)


def _nbr_bias(rpb):
    rows = 4096 // GRID_W
    q_per = TOK // GRID_W
    hi = lax.Precision.HIGHEST
    kc = np.arange(GRID_W)[:, None]
    qc = np.arange(GRID_W)[None, :]
    d_col = np.clip(kc - qc + WIN_W - 1, 0, 2 * WIN_W - 2)
    cs = np.clip(qc - WIN_W // 2, 0, GRID_W - WIN_W)
    col_in = (kc >= cs) & (kc < cs + WIN_W)
    e_col = (d_col[None] == np.arange(2 * WIN_W - 1)[:, None, None]).astype(np.float32)
    by_col = jnp.einsum("hdj,jkq->hdkq", rpb.astype(_F32), e_col, precision=hi)
    tables = []
    for i in (0, 1, rows // q_per - 1):
        s = min(max(i - 1, 0), rows // q_per - N_BAND)
        kr = (q_per * s + np.arange(N_BAND * q_per))[:, None]
        qr = (q_per * i + np.arange(q_per))[None, :]
        d_row = np.clip(kr - qr + WIN_H - 1, 0, 2 * WIN_H - 2)
        rs = np.clip(qr - WIN_H // 2, 0, rows - WIN_H)
        row_in = (kr >= rs) & (kr < rs + WIN_H)
        e_row = (d_row[None] == np.arange(2 * WIN_H - 1)[:, None, None]).astype(np.float32)
        tab = jnp.einsum("dab,hdkq->hakbq", e_row, by_col, precision=hi)
        inside = row_in[:, None, :, None] & col_in[None, :, None, :]
        tab = jnp.where(inside[None], tab * LOG2E, NEG_INF)
        tab = tab.reshape(B_HEADS // SLOT, SLOT, N_BAND, q_per, GRID_W, q_per, GRID_W)
        tables.append(tab.transpose(0, 2, 3, 4, 1, 5, 6).reshape(B_HEADS // SLOT, N_BAND, TOK, NCOL))
    return jnp.stack(tables, axis=0)


def _out_kernel(o_ref, g_ref, w_ref, gt_ref, x_ref, y_ref):
    og = (o_ref[0, 0].astype(_F32) * g_ref[0].astype(_F32)).astype(_BF16)
    y = jnp.dot(w_ref[...], og, preferred_element_type=_F32)
    y_ref[0] = x_ref[0] + _lane_tile(gt_ref[0]) * y


def _out_proj(ot, gt, w_t, gtcol, xt):
    bsz, _, t = xt.shape
    nt = t // TOK
    col = lambda b, j: (jnp.where(j == 0, bsz, b), 0, 0)
    tile = lambda b, j: (b, 0, j)
    return pl.pallas_call(
        _out_kernel,
        grid=(bsz, nt),
        in_specs=[
            pl.BlockSpec((1, 1, MIX_WIDTH, TOK), lambda b, j: (b, j, 0, 0)),
            pl.BlockSpec((1, MIX_WIDTH, TOK), tile),
            pl.BlockSpec((D_MODEL, MIX_WIDTH), lambda b, j: (0, 0)),
            pl.BlockSpec((1, D_MODEL, LANES), col),
            pl.BlockSpec((1, D_MODEL, TOK), tile),
        ],
        out_specs=pl.BlockSpec((1, D_MODEL, TOK), tile),
        out_shape=jax.ShapeDtypeStruct(xt.shape, _F32),
        input_output_aliases={4: 0},
        compiler_params=_cparams("arbitrary", "arbitrary"),
        name="out_proj",
    )(ot, gt, w_t, gtcol, xt)


def _rope_tables(s):
    tok = jnp.arange(s, dtype=jnp.int32)
    rows = (tok // GRID_W).astype(_F32)
    cols = (tok % GRID_W).astype(_F32)
    n_freq = HEAD_DIM // 4
    inv_freq = ROPE_BASE ** (-jnp.arange(n_freq, dtype=_F32) / n_freq)
    ang = jnp.concatenate([rows[:, None] * inv_freq, cols[:, None] * inv_freq], axis=1)
    ang = jnp.concatenate([jnp.zeros((CTX_LEN, 32), _F32), ang], axis=0)
    return jnp.cos(ang).T, jnp.sin(ang).T


def _col(v):
    return jnp.broadcast_to(v[..., None], v.shape + (LANES,))


def _lambda_init(layer):
    return 0.8 - 0.6 * math.exp(-0.3 * layer)


def kernel(x, c, ctx, c_ctx, norm_g, ada_w, ada_b, a_w_in, a_q_g, a_k_g, a_w_out, b_w_in, b_q_g, b_k_g, b_rpb, b_w_out, c_w_in, c_q_g, c_k_g, c_lam_q1, c_lam_k1, c_lam_q2, c_lam_k2, c_subln_g, c_w_out):
    bsz, s, _ = x.shape
    assert c.shape[0] + 1 <= 16
    scale = HEAD_DIM ** -0.5

    cvec = jnp.zeros((16, D_MODEL), _F32).at[:bsz].set(c).at[bsz].set(c_ctx)
    mod = _modulation(cvec, ada_w, ada_b)[:, :bsz + 1]
    sh, sc, gt = jnp.split(mod, 3, axis=-1)
    acol = _col(norm_g[:, None, :] * (1.0 + sc))
    shcol = _col(sh)
    gtcol = _col(gt)

    cos_t, sin_t = _rope_tables(s)
    xt = jnp.swapaxes(jnp.concatenate([ctx, x], axis=1), 1, 2)

    for i in range(DEPTH):
        kind, j = i % 3, i // 3
        if kind == 0:
            w_in, w_out, q_g, k_g = a_w_in[j], a_w_out[j], a_q_g[j], a_k_g[j]
            dims = dict(n_q=A_HEADS * HEAD_DIM, n_k=A_KV_HEADS * HEAD_DIM, n_v=A_KV_HEADS * HEAD_DIM, rope=True)
        elif kind == 1:
            w_in, w_out, q_g, k_g = b_w_in[j], b_w_out[j], b_q_g[j], b_k_g[j]
            dims = dict(n_q=MIX_WIDTH, n_k=MIX_WIDTH, n_v=MIX_WIDTH, rope=False)
        else:
            w_in, w_out, q_g, k_g = c_w_in[j], c_w_out[j], c_q_g[j], c_k_g[j]
            dims = dict(n_q=MIX_WIDTH, n_k=MIX_WIDTH, n_v=MIX_WIDTH, rope=True)
        qscale = scale * LOG2E
        qt, kt, vt, zt = _in_proj(xt, acol[i], shcol[i], w_in.T.astype(_BF16), cos_t, sin_t,
                                  _col(q_g * qscale), _col(k_g), **dims)
        if kind == 0:
            ot = _gqa_attention(qt, kt, vt)
        elif kind == 1:
            ot = _nbr_attention(qt, kt, vt, _nbr_bias(b_rpb[j]))
        else:
            lamv = jnp.zeros((8, LANES), _F32).at[0:4, :HEAD_DIM].set(
                jnp.stack([c_lam_q1[j], c_lam_k1[j], c_lam_q2[j], c_lam_k2[j]]).astype(_F32))
            ot = _diff_attention(qt, kt, vt, lamv, _col(c_subln_g[j]), _lambda_init(i))
        xt = _out_proj(ot, zt, w_out.T.astype(_BF16), gtcol[i], xt)

    return jnp.swapaxes(xt[:, :, CTX_LEN:], 1, 2)
```

```python
import functools
import math

import jax
import jax.numpy as jnp
import numpy as np
from jax import lax
from jax.experimental import pallas as pl
from jax.experimental.pallas import tpu as pltpu

D_MODEL = 1024
DEPTH = 4
GRID_W = 64
CTX_LEN = 256
HEAD_DIM = 64
MIX_WIDTH = 1024
A_HEADS = 16
A_KV_HEADS = 4
B_HEADS = 16
C_HEADS = 8
C_V_DIM = 128
WIN_H = 8
WIN_W = 16
ROPE_BASE = 10000.0
EPS = 1e-6
NEG_INF = -1e30
LOG2E = 1.4426950408889634

LANES = 128
TOK = 256
SLOT = 4
VMEM_LIMIT = 56 * 1024 * 1024

_BF16 = jnp.bfloat16
_F32 = jnp.float32


def _cparams(*sem):
    return pltpu.CompilerParams(dimension_semantics=sem, vmem_limit_bytes=VMEM_LIMIT)


def _mod_kernel(c_ref, w_ref, b_ref, o_ref):
    cv = c_ref[...]
    s = cv * jax.nn.sigmoid(cv)
    o_ref[0] = jnp.dot(s, w_ref[0], preferred_element_type=_F32) + b_ref[0]


def _modulation(cvec, ada_w, ada_b):
    n_col = 512
    return pl.pallas_call(
        _mod_kernel,
        grid=(DEPTH, 3 * D_MODEL // n_col),
        in_specs=[
            pl.BlockSpec((16, D_MODEL), lambda l, n: (0, 0)),
            pl.BlockSpec((1, D_MODEL, n_col), lambda l, n: (l, 0, n)),
            pl.BlockSpec((1, 1, n_col), lambda l, n: (l, 0, n)),
        ],
        out_specs=pl.BlockSpec((1, 16, n_col), lambda l, n: (l, 0, n)),
        out_shape=jax.ShapeDtypeStruct((DEPTH, 16, 3 * D_MODEL), _F32),
        compiler_params=_cparams("arbitrary", "arbitrary"),
        name="adaln_mod",
    )(cvec, ada_w, ada_b.reshape(DEPTH, 1, 3 * D_MODEL))


def _lane_tile(col_ref_val):
    return jnp.concatenate([col_ref_val] * (TOK // LANES), axis=1)


def _head_norm_rope(u, gain, cos, sin, rope):
    ms = jnp.sum(u * u, axis=0, keepdims=True) * (1.0 / HEAD_DIM)
    y = u * lax.rsqrt(ms + EPS) * gain
    if not rope:
        return y
    x1r, x2r, x1c, x2c = y[0:16], y[16:32], y[32:48], y[48:64]
    cr, cc = cos[0:16], cos[16:32]
    sr, sc = sin[0:16], sin[16:32]
    return jnp.concatenate(
        [x1r * cr - x2r * sr, x2r * cr + x1r * sr, x1c * cc - x2c * sc, x2c * cc + x1c * sc], axis=0)


def _proj_body(x, a_ref, sh_ref, w_ref, cos_ref, sin_ref, qg_ref, kg_ref,
               q_ref, k_ref, v_ref, g_ref, *, n_q, n_k, n_v, rope):
    ms = jnp.sum(x * x, axis=0, keepdims=True) * (1.0 / D_MODEL)
    h = x * lax.rsqrt(ms + EPS) * _lane_tile(a_ref[0]) + _lane_tile(sh_ref[0])
    hb = h.astype(_BF16)
    cos, sin = cos_ref[...], sin_ref[...]
    qg, kg = _lane_tile(qg_ref[...]), _lane_tile(kg_ref[...])

    uq = jnp.dot(w_ref[0:n_q, :], hb, preferred_element_type=_F32)
    for hd in range(n_q // HEAD_DIM):
        r = slice(hd * HEAD_DIM, (hd + 1) * HEAD_DIM)
        q_ref[0, 0, r, :] = _head_norm_rope(uq[r], qg, cos, sin, rope).astype(_BF16)

    uk = jnp.dot(w_ref[n_q:n_q + n_k, :], hb, preferred_element_type=_F32)
    for blk in range(n_k // TOK):
        heads = [_head_norm_rope(uk[blk * TOK + s * HEAD_DIM: blk * TOK + (s + 1) * HEAD_DIM],
                                 kg, cos, sin, rope) for s in range(SLOT)]
        kt = jnp.concatenate(heads, axis=0)
        k_ref[0, 0, blk] = kt.T.astype(_BF16)

    o = n_q + n_k
    uv = jnp.dot(w_ref[o:o + n_v, :], hb, preferred_element_type=_F32)
    v_ref[0, 0] = uv.astype(_BF16)

    o = o + n_v
    uz = jnp.dot(w_ref[o:o + MIX_WIDTH, :], hb, preferred_element_type=_F32)
    g_ref[0] = (uz * jax.nn.sigmoid(uz)).astype(_BF16)


def _residual_update(o_ref, g_ref, w_ref, gt_ref, x_ref):
    og = (o_ref[0, 0].astype(_F32) * g_ref[0].astype(_F32)).astype(_BF16)
    y = jnp.dot(w_ref[...], og, preferred_element_type=_F32)
    return x_ref[0] + _lane_tile(gt_ref[0]) * y


def _first_kernel(ctx_ref, x_ref, *refs, **dims):
    xt_ref, proj_out = refs[7], refs[8:]
    x = jnp.where(pl.program_id(1) == 0, ctx_ref[0], x_ref[0]).T
    xt_ref[0] = x
    _proj_body(x, *refs[:7], *proj_out, **dims)


def _mid_kernel(o_ref, g_ref, wo_ref, gt_ref, x_ref, *refs, **dims):
    xt_ref, proj_out = refs[7], refs[8:]
    x = _residual_update(o_ref, g_ref, wo_ref, gt_ref, x_ref)
    xt_ref[0] = x
    _proj_body(x, *refs[:7], *proj_out, **dims)


def _last_kernel(o_ref, g_ref, wo_ref, gt_ref, x_ref, out_ref):
    out_ref[0] = _residual_update(o_ref, g_ref, wo_ref, gt_ref, x_ref).T


def _proj_specs(bsz, nt, n_in, n_q, n_k, n_v):
    col = lambda b, j: (jnp.where(j == 0, bsz, b), 0, 0)
    t = nt * TOK
    in_specs = [
        pl.BlockSpec((1, D_MODEL, LANES), col),
        pl.BlockSpec((1, D_MODEL, LANES), col),
        pl.BlockSpec((n_in, D_MODEL), lambda b, j: (0, 0)),
        pl.BlockSpec((32, TOK), lambda b, j: (0, j)),
        pl.BlockSpec((32, TOK), lambda b, j: (0, j)),
        pl.BlockSpec((HEAD_DIM, LANES), lambda b, j: (0, 0)),
        pl.BlockSpec((HEAD_DIM, LANES), lambda b, j: (0, 0)),
    ]
    out_specs = [
        pl.BlockSpec((1, D_MODEL, TOK), lambda b, j: (b, 0, j)),
        pl.BlockSpec((1, 1, n_q, TOK), lambda b, j: (b, j, 0, 0)),
        pl.BlockSpec((1, 1, n_k // TOK, TOK, TOK), lambda b, j: (b, j, 0, 0, 0)),
        pl.BlockSpec((1, 1, n_v, TOK), lambda b, j: (b, j, 0, 0)),
        pl.BlockSpec((1, MIX_WIDTH, TOK), lambda b, j: (b, 0, j)),
    ]
    out_shape = [
        jax.ShapeDtypeStruct((bsz, D_MODEL, t), _F32),
        jax.ShapeDtypeStruct((bsz, nt, n_q, TOK), _BF16),
        jax.ShapeDtypeStruct((bsz, nt, n_k // TOK, TOK, TOK), _BF16),
        jax.ShapeDtypeStruct((bsz, nt, n_v, TOK), _BF16),
        jax.ShapeDtypeStruct((bsz, MIX_WIDTH, t), _BF16),
    ]
    return in_specs, out_specs, out_shape


def _first_proj(ctx, x, proj_args, *, n_q, n_k, n_v, rope):
    bsz, s, _ = x.shape
    nt = (ctx.shape[1] + s) // TOK
    in_specs, out_specs, out_shape = _proj_specs(bsz, nt, proj_args[2].shape[0], n_q, n_k, n_v)
    return pl.pallas_call(
        functools.partial(_first_kernel, n_q=n_q, n_k=n_k, n_v=n_v, rope=rope),
        grid=(bsz, nt),
        in_specs=[
            pl.BlockSpec((1, TOK, D_MODEL), lambda b, j: (b, 0, 0)),
            pl.BlockSpec((1, TOK, D_MODEL), lambda b, j: (b, jnp.maximum(j - 1, 0), 0)),
        ] + in_specs,
        out_specs=out_specs,
        out_shape=out_shape,
        compiler_params=_cparams("arbitrary", "arbitrary"),
        name="first_proj",
    )(ctx, x, *proj_args)


def _out_in_proj(ot, gz, wo_t, gtcol, xt, proj_args, *, n_q, n_k, n_v, rope):
    bsz, _, t = xt.shape
    nt = t // TOK
    in_specs, out_specs, out_shape = _proj_specs(bsz, nt, proj_args[2].shape[0], n_q, n_k, n_v)
    col = lambda b, j: (jnp.where(j == 0, bsz, b), 0, 0)
    tile = lambda b, j: (b, 0, j)
    return pl.pallas_call(
        functools.partial(_mid_kernel, n_q=n_q, n_k=n_k, n_v=n_v, rope=rope),
        grid=(bsz, nt),
        in_specs=[
            pl.BlockSpec((1, 1, MIX_WIDTH, TOK), lambda b, j: (b, j, 0, 0)),
            pl.BlockSpec((1, MIX_WIDTH, TOK), tile),
            pl.BlockSpec((D_MODEL, MIX_WIDTH), lambda b, j: (0, 0)),
            pl.BlockSpec((1, D_MODEL, LANES), col),
            pl.BlockSpec((1, D_MODEL, TOK), tile),
        ] + in_specs,
        out_specs=out_specs,
        out_shape=out_shape,
        input_output_aliases={4: 0},
        compiler_params=_cparams("arbitrary", "arbitrary"),
        name="out_in_proj",
    )(ot, gz, wo_t, gtcol, xt, *proj_args)


def _last_proj(ot, gz, wo_t, gtcol, xt):
    bsz, _, t = xt.shape
    n_lat = t // TOK - 1
    tile = lambda b, j: (b, 0, j + 1)
    return pl.pallas_call(
        _last_kernel,
        grid=(bsz, n_lat),
        in_specs=[
            pl.BlockSpec((1, 1, MIX_WIDTH, TOK), lambda b, j: (b, j + 1, 0, 0)),
            pl.BlockSpec((1, MIX_WIDTH, TOK), tile),
            pl.BlockSpec((D_MODEL, MIX_WIDTH), lambda b, j: (0, 0)),
            pl.BlockSpec((1, D_MODEL, LANES), lambda b, j: (b, 0, 0)),
            pl.BlockSpec((1, D_MODEL, TOK), tile),
        ],
        out_specs=pl.BlockSpec((1, TOK, D_MODEL), lambda b, j: (b, j, 0)),
        out_shape=jax.ShapeDtypeStruct((bsz, n_lat * TOK, D_MODEL), _F32),
        compiler_params=_cparams("arbitrary", "arbitrary"),
        name="last_proj",
    )(ot, gz, wo_t, gtcol, xt)


def _pad_slot(qh, slot):
    z = jnp.zeros_like(qh)
    return jnp.concatenate([qh if s == slot else z for s in range(SLOT)], axis=0)


NCOL = SLOT * TOK
ONES_ROWS = 16


def _score_stage(k_tile, qcat_ref, s_ref, mx_ref, sb, bias=None, keep=None):
    s = jnp.dot(k_tile, qcat_ref[...], preferred_element_type=_F32)
    if bias is not None:
        s = jnp.where(keep, s + bias, NEG_INF)
    s_ref[sb] = s
    mx_ref[sb] = jnp.max(s, axis=0, keepdims=True)


def _softmax_value_stage(v_tile, s_ref, mx_ref, m_ref, acc_ref, sb):
    m_old = m_ref[...]
    m_new = jnp.maximum(m_old, mx_ref[sb])
    alpha = jnp.exp2(m_old - m_new)
    m_ref[...] = m_new
    for j in range(SLOT):
        cols = slice(j * TOK, (j + 1) * TOK)
        v = v_tile(j)
        v_aug = jnp.concatenate([v, jnp.ones((ONES_ROWS, v.shape[1]), _BF16)], axis=0)
        p = jnp.exp2(s_ref[sb, :, cols] - m_new[:, cols]).astype(_BF16)
        acc_ref[:, cols] = (alpha[:, cols] * acc_ref[:, cols]
                            + jnp.dot(v_aug, p, preferred_element_type=_F32))


LAST_BUF = 2


def _attend_tiles(n_t, build_qcat, k_tile, v_tile, finalize, qcat_ref, s_ref, mx_ref, m_ref, acc_ref):
    stage = functools.partial(_softmax_value_stage, s_ref=s_ref, mx_ref=mx_ref, m_ref=m_ref, acc_ref=acc_ref)
    score = functools.partial(_score_stage, qcat_ref=qcat_ref, s_ref=s_ref, mx_ref=mx_ref)

    def reset():
        m_ref[...] = jnp.full(m_ref.shape, NEG_INF, _F32)
        acc_ref[...] = jnp.zeros(acc_ref.shape, _F32)

    def finish(j, last):
        stage(functools.partial(v_tile, last), sb=LAST_BUF)
        finalize(j)
        reset()

    reset()
    build_qcat(0)
    score(k_tile(0), sb=LAST_BUF)

    def tile(j, carry):
        finish(j - 1, jnp.where(j == 1, 0, n_t - 1))
        build_qcat(j)
        score(k_tile(0), sb=0)
        for k in range(n_t - 1):
            score(k_tile(k + 1), sb=(k + 1) % 2 if k + 2 < n_t else LAST_BUF)
            stage(functools.partial(v_tile, k), sb=k % 2)
        return carry

    lax.fori_loop(1, n_t, tile, 0)
    finish(n_t - 1, n_t - 1)


def _attend_scratch(dv):
    return [
        pltpu.VMEM((SLOT * HEAD_DIM, NCOL), _BF16),
        pltpu.VMEM((3, TOK, NCOL), _F32),
        pltpu.VMEM((3, 1, NCOL), _F32),
        pltpu.VMEM((1, NCOL), _F32),
        pltpu.VMEM((dv + ONES_ROWS, NCOL), _F32),
    ]


def _gqa_kernel(q_ref, k_ref, v_ref, o_ref, qcat_ref, s_ref, mx_ref, m_ref, acc_ref):
    kvh = pl.program_id(1)
    row_slot = lax.broadcasted_iota(jnp.int32, (SLOT * HEAD_DIM, TOK), 0) // HEAD_DIM

    def build_qcat(j):
        for g in range(SLOT):
            qh = q_ref[0, j, g * HEAD_DIM:(g + 1) * HEAD_DIM, :].astype(_F32)
            qrep = jnp.concatenate([qh] * SLOT, axis=0)
            qcat_ref[:, g * TOK:(g + 1) * TOK] = jnp.where(row_slot == kvh, qrep, 0.0).astype(_BF16)

    def finalize(j):
        o = acc_ref[0:HEAD_DIM, :] / acc_ref[HEAD_DIM:HEAD_DIM + 1, :]
        for g in range(SLOT):
            o_ref[0, j, g * HEAD_DIM:(g + 1) * HEAD_DIM, :] = o[:, g * TOK:(g + 1) * TOK].astype(_BF16)

    _attend_tiles(k_ref.shape[1], build_qcat, lambda t: k_ref[0, t, 0], lambda t, g: v_ref[0, t], finalize,
                  qcat_ref, s_ref, mx_ref, m_ref, acc_ref)


def _gqa_attention(qt, kt, vt):
    bsz, nt = qt.shape[:2]
    grp = SLOT * HEAD_DIM
    return pl.pallas_call(
        _gqa_kernel,
        grid=(bsz, A_KV_HEADS),
        in_specs=[
            pl.BlockSpec((1, nt, grp, TOK), lambda b, h: (b, 0, h, 0)),
            pl.BlockSpec((1, nt, 1, TOK, TOK), lambda b, h: (b, 0, 0, 0, 0)),
            pl.BlockSpec((1, nt, HEAD_DIM, TOK), lambda b, h: (b, 0, h, 0)),
        ],
        out_specs=pl.BlockSpec((1, nt, grp, TOK), lambda b, h: (b, 0, h, 0)),
        out_shape=jax.ShapeDtypeStruct((bsz, nt, MIX_WIDTH, TOK), _BF16),
        scratch_shapes=_attend_scratch(HEAD_DIM),
        compiler_params=_cparams("arbitrary", "arbitrary"),
        name="gqa_attn",
    )(qt, kt, vt)


def _diff_kernel(q_ref, k_ref, v_ref, lam_ref, sg_ref, o_ref, qcat_ref, s_ref, mx_ref, m_ref, acc_ref,
                 *, out_scale, lam_init):
    def build_qcat(j):
        for s in range(SLOT):
            qcat_ref[:, s * TOK:(s + 1) * TOK] = _pad_slot(q_ref[0, j, s * HEAD_DIM:(s + 1) * HEAD_DIM, :], s)

    def finalize(j):
        lv = lam_ref[...]
        lam = (jnp.exp(jnp.sum(lv[0:1] * lv[1:2], axis=1, keepdims=True))
               - jnp.exp(jnp.sum(lv[2:3] * lv[3:4], axis=1, keepdims=True)) + lam_init)
        sg = _lane_tile(sg_ref[...])
        attn = acc_ref[0:C_V_DIM, :] / acc_ref[C_V_DIM:C_V_DIM + 1, :]
        for hh in range(2):
            o = attn[:, 2 * hh * TOK:(2 * hh + 1) * TOK] - lam * attn[:, (2 * hh + 1) * TOK:(2 * hh + 2) * TOK]
            ms = jnp.sum(o * o, axis=0, keepdims=True) * (1.0 / C_V_DIM)
            y = o * lax.rsqrt(ms + EPS) * sg * out_scale
            o_ref[0, j, hh * C_V_DIM:(hh + 1) * C_V_DIM, :] = y.astype(_BF16)

    _attend_tiles(k_ref.shape[1], build_qcat, lambda t: k_ref[0, t, 0],
                  lambda t, s: v_ref[0, t, (s // 2) * C_V_DIM:(s // 2 + 1) * C_V_DIM, :], finalize,
                  qcat_ref, s_ref, mx_ref, m_ref, acc_ref)


def _diff_attention(qt, kt, vt, lamv, sg, lam_init):
    bsz, nt = qt.shape[:2]
    n_blk = kt.shape[2]
    grp = SLOT * HEAD_DIM
    return pl.pallas_call(
        functools.partial(_diff_kernel, out_scale=1.0 - lam_init, lam_init=lam_init),
        grid=(bsz, n_blk),
        in_specs=[
            pl.BlockSpec((1, nt, grp, TOK), lambda b, c: (b, 0, c, 0)),
            pl.BlockSpec((1, nt, 1, TOK, TOK), lambda b, c: (b, 0, c, 0, 0)),
            pl.BlockSpec((1, nt, 2 * C_V_DIM, TOK), lambda b, c: (b, 0, c, 0)),
            pl.BlockSpec((8, LANES), lambda b, c: (0, 0)),
            pl.BlockSpec((C_V_DIM, LANES), lambda b, c: (0, 0)),
        ],
        out_specs=pl.BlockSpec((1, nt, 2 * C_V_DIM, TOK), lambda b, c: (b, 0, c, 0)),
        out_shape=jax.ShapeDtypeStruct((bsz, nt, MIX_WIDTH, TOK), _BF16),
        scratch_shapes=_attend_scratch(C_V_DIM),
        compiler_params=_cparams("arbitrary", "arbitrary"),
        name="diff_attn",
    )(qt, kt, vt, lamv, sg)


N_BAND = 3


def _nbr_kernel(q_ref, kc_ref, k0_ref, k1_ref, k2_ref, vc_ref, v0_ref, v1_ref, v2_ref, bias_ref, o_ref,
                qcat_ref, s_ref, mx_ref, m_ref, acc_ref):
    is_lat = pl.program_id(0) > 0
    k_refs = (kc_ref, k0_ref, k1_ref, k2_ref)
    v_refs = (vc_ref, v0_ref, v1_ref, v2_ref)
    n_tiles = len(k_refs)
    n_quad = B_HEADS // SLOT
    head = lambda c, g: slice((c * SLOT + g) * HEAD_DIM, (c * SLOT + g + 1) * HEAD_DIM)

    def begin(c):
        for g in range(SLOT):
            qcat_ref[c % 2, :, g * TOK:(g + 1) * TOK] = _pad_slot(q_ref[0, 0, head(c, g), :], g)
        m_ref[c % 2] = jnp.full(m_ref.shape[1:], NEG_INF, _F32)
        acc_ref[c % 2] = jnp.zeros(acc_ref.shape[1:], _F32)

    def score(i):
        c, t = divmod(i, n_tiles)
        if t == 0:
            begin(c)
        bias = None if t == 0 else bias_ref[0, c, t - 1]
        _score_stage(k_refs[t][0, 0, c], qcat_ref.at[c % 2], s_ref, mx_ref, i % 2, bias=bias, keep=is_lat)

    score(0)
    for i in range(n_quad * n_tiles):
        c, t = divmod(i, n_tiles)
        if i + 1 < n_quad * n_tiles:
            score(i + 1)
        acc = acc_ref.at[c % 2]
        _softmax_value_stage(lambda g, c=c, t=t: v_refs[t][0, 0, head(c, g), :],
                             s_ref, mx_ref, m_ref.at[c % 2], acc, i % 2)
        if t == n_tiles - 1:
            o = acc[0:HEAD_DIM, :] / acc[HEAD_DIM:HEAD_DIM + 1, :]
            for g in range(SLOT):
                o_ref[0, 0, head(c, g), :] = o[:, g * TOK:(g + 1) * TOK].astype(_BF16)


def _nbr_attention(qt, kt, vt, bias):
    bsz, nt = qt.shape[:2]
    n_blk = kt.shape[2]
    n_lat = nt - 1
    tile = lambda j, b: (b, j, 0, 0)

    def band(d):
        return lambda j, b: (b, 1 + jnp.clip(j - 2, 0, n_lat - N_BAND) + d, 0, 0, 0)

    def vband(d):
        return lambda j, b: (b, 1 + jnp.clip(j - 2, 0, n_lat - N_BAND) + d, 0, 0)

    case = lambda j, b: (jnp.where(j <= 1, 0, jnp.where(j == nt - 1, 2, 1)), 0, 0, 0, 0)
    kspec = lambda im: pl.BlockSpec((1, 1, n_blk, TOK, TOK), im)
    vspec = lambda im: pl.BlockSpec((1, 1, MIX_WIDTH, TOK), im)
    return pl.pallas_call(
        _nbr_kernel,
        grid=(nt, bsz),
        in_specs=[
            pl.BlockSpec((1, 1, MIX_WIDTH, TOK), tile),
            kspec(lambda j, b: (b, 0, 0, 0, 0)), kspec(band(0)), kspec(band(1)), kspec(band(2)),
            vspec(lambda j, b: (b, 0, 0, 0)), vspec(vband(0)), vspec(vband(1)), vspec(vband(2)),
            pl.BlockSpec((1, B_HEADS // SLOT, N_BAND, TOK, NCOL), case),
        ],
        out_specs=pl.BlockSpec((1, 1, MIX_WIDTH, TOK), tile),
        out_shape=jax.ShapeDtypeStruct((bsz, nt, MIX_WIDTH, TOK), _BF16),
        scratch_shapes=[
            pltpu.VMEM((2, SLOT * HEAD_DIM, NCOL), _BF16),
            pltpu.VMEM((2, TOK, NCOL), _F32),
            pltpu.VMEM((2, 1, NCOL), _F32),
            pltpu.VMEM((2, 1, NCOL), _F32),
            pltpu.VMEM((2, HEAD_DIM + ONES_ROWS, NCOL), _F32),
        ],
        compiler_params=_cparams("arbitrary", "arbitrary"),
        name="nbr_attn",
    )(qt, kt, kt, kt, kt, vt, vt, vt, vt, bias)


def _nbr_bias(rpb):
    rows = 4096 // GRID_W
    q_per = TOK // GRID_W
    hi = lax.Precision.HIGHEST
    kc = np.arange(GRID_W)[:, None]
    qc = np.arange(GRID_W)[None, :]
    d_col = np.clip(kc - qc + WIN_W - 1, 0, 2 * WIN_W - 2)
    cs = np.clip(qc - WIN_W // 2, 0, GRID_W - WIN_W)
    col_in = (kc >= cs) & (kc < cs + WIN_W)
    e_col = (d_col[None] == np.arange(2 * WIN_W - 1)[:, None, None]).astype(np.float32)
    by_col = jnp.einsum("hdj,jkq->hdkq", rpb.astype(_F32), e_col, precision=hi)
    tables = []
    for i in (0, 1, rows // q_per - 1):
        s = min(max(i - 1, 0), rows // q_per - N_BAND)
        kr = (q_per * s + np.arange(N_BAND * q_per))[:, None]
        qr = (q_per * i + np.arange(q_per))[None, :]
        d_row = np.clip(kr - qr + WIN_H - 1, 0, 2 * WIN_H - 2)
        rs = np.clip(qr - WIN_H // 2, 0, rows - WIN_H)
        row_in = (kr >= rs) & (kr < rs + WIN_H)
        e_row = (d_row[None] == np.arange(2 * WIN_H - 1)[:, None, None]).astype(np.float32)
        tab = jnp.einsum("dab,hdkq->hakbq", e_row, by_col, precision=hi)
        inside = row_in[:, None, :, None] & col_in[None, :, None, :]
        tab = jnp.where(inside[None], tab * LOG2E, NEG_INF)
        tab = tab.reshape(B_HEADS // SLOT, SLOT, N_BAND, q_per, GRID_W, q_per, GRID_W)
        tables.append(tab.transpose(0, 2, 3, 4, 1, 5, 6).reshape(B_HEADS // SLOT, N_BAND, TOK, NCOL))
    return jnp.stack(tables, axis=0)


def _rope_tables(s):
    tok = jnp.arange(s, dtype=jnp.int32)
    rows = (tok // GRID_W).astype(_F32)
    cols = (tok % GRID_W).astype(_F32)
    n_freq = HEAD_DIM // 4
    inv_freq = ROPE_BASE ** (-jnp.arange(n_freq, dtype=_F32) / n_freq)
    ang = jnp.concatenate([rows[:, None] * inv_freq, cols[:, None] * inv_freq], axis=1)
    ang = jnp.concatenate([jnp.zeros((CTX_LEN, 32), _F32), ang], axis=0)
    return jnp.cos(ang).T, jnp.sin(ang).T


def _col(v):
    return jnp.broadcast_to(v[..., None], v.shape + (LANES,))


def _lambda_init(layer):
    return 0.8 - 0.6 * math.exp(-0.3 * layer)


def kernel(x, c, ctx, c_ctx, norm_g, ada_w, ada_b, a_w_in, a_q_g, a_k_g, a_w_out, b_w_in, b_q_g, b_k_g, b_rpb, b_w_out, c_w_in, c_q_g, c_k_g, c_lam_q1, c_lam_k1, c_lam_q2, c_lam_k2, c_subln_g, c_w_out):
    bsz, s, _ = x.shape
    assert c.shape[0] + 1 <= 16
    scale = HEAD_DIM ** -0.5

    cvec = jnp.zeros((16, D_MODEL), _F32).at[:bsz].set(c).at[bsz].set(c_ctx)
    mod = _modulation(cvec, ada_w, ada_b)[:, :bsz + 1]
    sh, sc, gt = jnp.split(mod, 3, axis=-1)
    acol = _col(norm_g[:, None, :] * (1.0 + sc))
    shcol = _col(sh)
    gtcol = _col(gt)

    cos_t, sin_t = _rope_tables(s)
    w_ins, w_outs, q_gs, k_gs = (a_w_in, b_w_in, c_w_in), (a_w_out, b_w_out, c_w_out), \
        (a_q_g, b_q_g, c_q_g), (a_k_g, b_k_g, c_k_g)
    n_kv = (A_KV_HEADS * HEAD_DIM, MIX_WIDTH, MIX_WIDTH)

    ot = zt = xt = wo_t = None
    for i in range(DEPTH):
        kind, j = i % 3, i // 3
        dims = dict(n_q=MIX_WIDTH, n_k=n_kv[kind], n_v=n_kv[kind], rope=kind != 1)
        proj_args = (acol[i], shcol[i], w_ins[kind][j].T.astype(_BF16), cos_t, sin_t,
                     _col(q_gs[kind][j] * (scale * LOG2E)), _col(k_gs[kind][j]))
        if i == 0:
            xt, qt, kt, vt, zt = _first_proj(ctx, x, proj_args, **dims)
        else:
            xt, qt, kt, vt, zt = _out_in_proj(ot, zt, wo_t, gtcol[i - 1], xt, proj_args, **dims)
        if kind == 0:
            ot = _gqa_attention(qt, kt, vt)
        elif kind == 1:
            ot = _nbr_attention(qt, kt, vt, _nbr_bias(b_rpb[j]))
        else:
            lamv = jnp.zeros((8, LANES), _F32).at[0:4, :HEAD_DIM].set(
                jnp.stack([c_lam_q1[j], c_lam_k1[j], c_lam_q2[j], c_lam_k2[j]]).astype(_F32))
            ot = _diff_attention(qt, kt, vt, lamv, _col(c_subln_g[j]), _lambda_init(i))
        wo_t = w_outs[kind][j].T.astype(_BF16)

    return _last_proj(ot, zt, wo_t, gtcol[DEPTH - 1], xt)
```

```python
import functools
import math

import jax
import jax.numpy as jnp
import numpy as np
from jax import lax
from jax.experimental import pallas as pl
from jax.experimental.pallas import tpu as pltpu

D_MODEL = 1024
DEPTH = 4
GRID_W = 64
CTX_LEN = 256
HEAD_DIM = 64
MIX_WIDTH = 1024
A_HEADS = 16
A_KV_HEADS = 4
B_HEADS = 16
C_HEADS = 8
C_V_DIM = 128
WIN_H = 8
WIN_W = 16
ROPE_BASE = 10000.0
EPS = 1e-6
NEG_INF = -1e30
LOG2E = 1.4426950408889634

LANES = 128
TOK = 256
SLOT = 4
VMEM_LIMIT = 56 * 1024 * 1024

_BF16 = jnp.bfloat16
_F32 = jnp.float32


def _cparams(*sem):
    return pltpu.CompilerParams(dimension_semantics=sem, vmem_limit_bytes=VMEM_LIMIT)


def _mod_kernel(c_ref, w_ref, b_ref, o_ref):
    cv = c_ref[...]
    s = cv * jax.nn.sigmoid(cv)
    o_ref[0] = jnp.dot(s, w_ref[0], preferred_element_type=_F32) + b_ref[0]


def _modulation(cvec, ada_w, ada_b):
    n_col = 512
    return pl.pallas_call(
        _mod_kernel,
        grid=(DEPTH, 3 * D_MODEL // n_col),
        in_specs=[
            pl.BlockSpec((16, D_MODEL), lambda l, n: (0, 0)),
            pl.BlockSpec((1, D_MODEL, n_col), lambda l, n: (l, 0, n)),
            pl.BlockSpec((1, 1, n_col), lambda l, n: (l, 0, n)),
        ],
        out_specs=pl.BlockSpec((1, 16, n_col), lambda l, n: (l, 0, n)),
        out_shape=jax.ShapeDtypeStruct((DEPTH, 16, 3 * D_MODEL), _F32),
        compiler_params=_cparams("arbitrary", "arbitrary"),
        name="adaln_mod",
    )(cvec, ada_w, ada_b.reshape(DEPTH, 1, 3 * D_MODEL))


def _lane_tile(col_ref_val):
    return jnp.concatenate([col_ref_val] * (TOK // LANES), axis=1)


def _head_norm_rope(u, gain, cos, sin, rope):
    ms = jnp.sum(u * u, axis=0, keepdims=True) * (1.0 / HEAD_DIM)
    y = u * lax.rsqrt(ms + EPS) * gain
    if not rope:
        return y
    x1r, x2r, x1c, x2c = y[0:16], y[16:32], y[32:48], y[48:64]
    cr, cc = cos[0:16], cos[16:32]
    sr, sc = sin[0:16], sin[16:32]
    return jnp.concatenate(
        [x1r * cr - x2r * sr, x2r * cr + x1r * sr, x1c * cc - x2c * sc, x2c * cc + x1c * sc], axis=0)


def _proj_body(x, a_ref, sh_ref, w_ref, cos_ref, sin_ref, qg_ref, kg_ref,
               q_ref, k_ref, v_ref, g_ref, *, n_q, n_k, n_v, rope):
    ms = jnp.sum(x * x, axis=0, keepdims=True) * (1.0 / D_MODEL)
    h = x * lax.rsqrt(ms + EPS) * _lane_tile(a_ref[0]) + _lane_tile(sh_ref[0])
    hb = h.astype(_BF16)
    cos, sin = cos_ref[...], sin_ref[...]
    qg, kg = _lane_tile(qg_ref[...]), _lane_tile(kg_ref[...])

    uq = jnp.dot(w_ref[0:n_q, :], hb, preferred_element_type=_F32)
    for hd in range(n_q // HEAD_DIM):
        r = slice(hd * HEAD_DIM, (hd + 1) * HEAD_DIM)
        q_ref[0, 0, r, :] = _head_norm_rope(uq[r], qg, cos, sin, rope).astype(_BF16)

    uk = jnp.dot(w_ref[n_q:n_q + n_k, :], hb, preferred_element_type=_F32)
    for blk in range(n_k // TOK):
        heads = [_head_norm_rope(uk[blk * TOK + s * HEAD_DIM: blk * TOK + (s + 1) * HEAD_DIM],
                                 kg, cos, sin, rope) for s in range(SLOT)]
        kt = jnp.concatenate(heads, axis=0)
        k_ref[0, 0, blk] = kt.T.astype(_BF16)

    o = n_q + n_k
    uv = jnp.dot(w_ref[o:o + n_v, :], hb, preferred_element_type=_F32)
    v_ref[0, 0] = uv.astype(_BF16)

    o = o + n_v
    uz = jnp.dot(w_ref[o:o + MIX_WIDTH, :], hb, preferred_element_type=_F32)
    g_ref[0] = (uz * jax.nn.sigmoid(uz)).astype(_BF16)


def _residual_update(o_ref, g_ref, w_ref, gt_ref, x_ref):
    og = (o_ref[0, 0].astype(_F32) * g_ref[0].astype(_F32)).astype(_BF16)
    y = jnp.dot(w_ref[...], og, preferred_element_type=_F32)
    return x_ref[0] + _lane_tile(gt_ref[0]) * y


def _first_kernel(ctx_ref, x_ref, *refs, **dims):
    xt_ref, proj_out = refs[7], refs[8:]
    x = jnp.where(pl.program_id(1) == 0, ctx_ref[0], x_ref[0]).T
    xt_ref[0] = x
    _proj_body(x, *refs[:7], *proj_out, **dims)


def _mid_kernel(o_ref, g_ref, wo_ref, gt_ref, x_ref, *refs, **dims):
    xt_ref, proj_out = refs[7], refs[8:]
    x = _residual_update(o_ref, g_ref, wo_ref, gt_ref, x_ref)
    xt_ref[0] = x
    _proj_body(x, *refs[:7], *proj_out, **dims)


def _last_kernel(o_ref, g_ref, wo_ref, gt_ref, x_ref, out_ref):
    out_ref[0] = _residual_update(o_ref, g_ref, wo_ref, gt_ref, x_ref).T


def _proj_specs(bsz, nt, n_in, n_q, n_k, n_v):
    col = lambda b, j: (jnp.where(j == 0, bsz, b), 0, 0)
    t = nt * TOK
    in_specs = [
        pl.BlockSpec((1, D_MODEL, LANES), col),
        pl.BlockSpec((1, D_MODEL, LANES), col),
        pl.BlockSpec((n_in, D_MODEL), lambda b, j: (0, 0)),
        pl.BlockSpec((32, TOK), lambda b, j: (0, j)),
        pl.BlockSpec((32, TOK), lambda b, j: (0, j)),
        pl.BlockSpec((HEAD_DIM, LANES), lambda b, j: (0, 0)),
        pl.BlockSpec((HEAD_DIM, LANES), lambda b, j: (0, 0)),
    ]
    out_specs = [
        pl.BlockSpec((1, D_MODEL, TOK), lambda b, j: (b, 0, j)),
        pl.BlockSpec((1, 1, n_q, TOK), lambda b, j: (b, j, 0, 0)),
        pl.BlockSpec((1, 1, n_k // TOK, TOK, TOK), lambda b, j: (b, j, 0, 0, 0)),
        pl.BlockSpec((1, 1, n_v, TOK), lambda b, j: (b, j, 0, 0)),
        pl.BlockSpec((1, MIX_WIDTH, TOK), lambda b, j: (b, 0, j)),
    ]
    out_shape = [
        jax.ShapeDtypeStruct((bsz, D_MODEL, t), _F32),
        jax.ShapeDtypeStruct((bsz, nt, n_q, TOK), _BF16),
        jax.ShapeDtypeStruct((bsz, nt, n_k // TOK, TOK, TOK), _BF16),
        jax.ShapeDtypeStruct((bsz, nt, n_v, TOK), _BF16),
        jax.ShapeDtypeStruct((bsz, MIX_WIDTH, t), _BF16),
    ]
    return in_specs, out_specs, out_shape


def _first_proj(ctx, x, proj_args, *, n_q, n_k, n_v, rope):
    bsz, s, _ = x.shape
    nt = (ctx.shape[1] + s) // TOK
    in_specs, out_specs, out_shape = _proj_specs(bsz, nt, proj_args[2].shape[0], n_q, n_k, n_v)
    return pl.pallas_call(
        functools.partial(_first_kernel, n_q=n_q, n_k=n_k, n_v=n_v, rope=rope),
        grid=(bsz, nt),
        in_specs=[
            pl.BlockSpec((1, TOK, D_MODEL), lambda b, j: (b, 0, 0)),
            pl.BlockSpec((1, TOK, D_MODEL), lambda b, j: (b, jnp.maximum(j - 1, 0), 0)),
        ] + in_specs,
        out_specs=out_specs,
        out_shape=out_shape,
        compiler_params=_cparams("arbitrary", "arbitrary"),
        name="first_proj",
    )(ctx, x, *proj_args)


def _out_in_proj(ot, gz, wo_t, gtcol, xt, proj_args, *, n_q, n_k, n_v, rope):
    bsz, _, t = xt.shape
    nt = t // TOK
    in_specs, out_specs, out_shape = _proj_specs(bsz, nt, proj_args[2].shape[0], n_q, n_k, n_v)
    col = lambda b, j: (jnp.where(j == 0, bsz, b), 0, 0)
    tile = lambda b, j: (b, 0, j)
    return pl.pallas_call(
        functools.partial(_mid_kernel, n_q=n_q, n_k=n_k, n_v=n_v, rope=rope),
        grid=(bsz, nt),
        in_specs=[
            pl.BlockSpec((1, 1, MIX_WIDTH, TOK), lambda b, j: (b, j, 0, 0)),
            pl.BlockSpec((1, MIX_WIDTH, TOK), tile),
            pl.BlockSpec((D_MODEL, MIX_WIDTH), lambda b, j: (0, 0)),
            pl.BlockSpec((1, D_MODEL, LANES), col),
            pl.BlockSpec((1, D_MODEL, TOK), tile),
        ] + in_specs,
        out_specs=out_specs,
        out_shape=out_shape,
        input_output_aliases={4: 0},
        compiler_params=_cparams("arbitrary", "arbitrary"),
        name="out_in_proj",
    )(ot, gz, wo_t, gtcol, xt, *proj_args)


def _last_proj(ot, gz, wo_t, gtcol, xt):
    bsz, _, t = xt.shape
    n_lat = t // TOK - 1
    tile = lambda b, j: (b, 0, j + 1)
    return pl.pallas_call(
        _last_kernel,
        grid=(bsz, n_lat),
        in_specs=[
            pl.BlockSpec((1, 1, MIX_WIDTH, TOK), lambda b, j: (b, j + 1, 0, 0)),
            pl.BlockSpec((1, MIX_WIDTH, TOK), tile),
            pl.BlockSpec((D_MODEL, MIX_WIDTH), lambda b, j: (0, 0)),
            pl.BlockSpec((1, D_MODEL, LANES), lambda b, j: (b, 0, 0)),
            pl.BlockSpec((1, D_MODEL, TOK), tile),
        ],
        out_specs=pl.BlockSpec((1, TOK, D_MODEL), lambda b, j: (b, j, 0)),
        out_shape=jax.ShapeDtypeStruct((bsz, n_lat * TOK, D_MODEL), _F32),
        compiler_params=_cparams("arbitrary", "arbitrary"),
        name="last_proj",
    )(ot, gz, wo_t, gtcol, xt)


def _pad_slot(qh, slot):
    z = jnp.zeros_like(qh)
    return jnp.concatenate([qh if s == slot else z for s in range(SLOT)], axis=0)


NCOL = SLOT * TOK
ONES_ROWS = 16


def _score_stage(k_tile, qcat_ref, s_ref, mx_ref, sb, rows=TOK, bias=None, keep=None):
    s = jnp.dot(k_tile, qcat_ref[...], preferred_element_type=_F32)
    if bias is not None:
        s = jnp.where(keep, s + bias, NEG_INF)
    s_ref[sb, 0:rows] = s
    mx_ref[sb] = jnp.max(s, axis=0, keepdims=True)


def _softmax_value_stage(v_tile, s_ref, mx_ref, m_ref, acc_ref, sb, rows=TOK):
    m_old = m_ref[...]
    m_new = jnp.maximum(m_old, mx_ref[sb])
    alpha = jnp.exp2(m_old - m_new)
    m_ref[...] = m_new
    for j in range(SLOT):
        cols = slice(j * TOK, (j + 1) * TOK)
        v = v_tile(j)
        v_aug = jnp.concatenate([v, jnp.ones((ONES_ROWS, v.shape[1]), _BF16)], axis=0)
        p = jnp.exp2(s_ref[sb, 0:rows, cols] - m_new[:, cols]).astype(_BF16)
        acc_ref[:, cols] = (alpha[:, cols] * acc_ref[:, cols]
                            + jnp.dot(v_aug, p, preferred_element_type=_F32))


LAST_BUF = 2
KEY_GROUP = 2


def _attend_tiles(n_t, build_qcat, k_tile, v_tile, finalize, qcat_ref, s_ref, mx_ref, m_ref, acc_ref):
    groups = [(0, 1)] + [(t, KEY_GROUP) for t in range(1, n_t, KEY_GROUP)]
    assert (n_t - 1) % KEY_GROUP == 0
    n_g = len(groups)
    buf = lambda g: LAST_BUF if g == n_g - 1 else g % 2

    def score(g):
        t, n = groups[g]
        _score_stage(k_tile(t, n), qcat_ref, s_ref, mx_ref, buf(g), rows=n * TOK)

    def stage(g):
        t, n = groups[g]
        _softmax_value_stage(functools.partial(v_tile, t, n), s_ref, mx_ref, m_ref, acc_ref, buf(g), rows=n * TOK)

    def reset():
        m_ref[...] = jnp.full(m_ref.shape, NEG_INF, _F32)
        acc_ref[...] = jnp.zeros(acc_ref.shape, _F32)

    def start(j):
        build_qcat(j)
        score(0)

    reset()
    start(0)
    stage(0)
    finalize(0)
    reset()
    start(1)

    def tile(j, carry):
        for g in range(n_g - 1):
            score(g + 1)
            stage(g)
        stage(n_g - 1)
        finalize(j)
        reset()
        start(jnp.minimum(j + 1, n_t - 1))
        return carry

    lax.fori_loop(1, n_t, tile, 0)


def _attend_scratch(dv):
    return [
        pltpu.VMEM((SLOT * HEAD_DIM, NCOL), _BF16),
        pltpu.VMEM((3, KEY_GROUP * TOK, NCOL), _F32),
        pltpu.VMEM((3, 1, NCOL), _F32),
        pltpu.VMEM((1, NCOL), _F32),
        pltpu.VMEM((dv + ONES_ROWS, NCOL), _F32),
    ]


def _gqa_kernel(q_ref, k_ref, v_ref, o_ref, qcat_ref, s_ref, mx_ref, m_ref, acc_ref):
    kvh = pl.program_id(1)
    row_slot = lax.broadcasted_iota(jnp.int32, (SLOT * HEAD_DIM, TOK), 0) // HEAD_DIM

    def build_qcat(j):
        for g in range(SLOT):
            qh = q_ref[0, j, g * HEAD_DIM:(g + 1) * HEAD_DIM, :].astype(_F32)
            qrep = jnp.concatenate([qh] * SLOT, axis=0)
            qcat_ref[:, g * TOK:(g + 1) * TOK] = jnp.where(row_slot == kvh, qrep, 0.0).astype(_BF16)

    def finalize(j):
        o = acc_ref[0:HEAD_DIM, :] / acc_ref[HEAD_DIM:HEAD_DIM + 1, :]
        for g in range(SLOT):
            o_ref[0, j, g * HEAD_DIM:(g + 1) * HEAD_DIM, :] = o[:, g * TOK:(g + 1) * TOK].astype(_BF16)

    _attend_tiles(k_ref.shape[1], build_qcat,
                  lambda t, n: k_ref[0, t:t + n, 0].reshape(n * TOK, TOK),
                  lambda t, n, g: jnp.concatenate([v_ref[0, t + i] for i in range(n)], axis=1),
                  finalize, qcat_ref, s_ref, mx_ref, m_ref, acc_ref)


def _gqa_attention(qt, kt, vt):
    bsz, nt = qt.shape[:2]
    grp = SLOT * HEAD_DIM
    return pl.pallas_call(
        _gqa_kernel,
        grid=(bsz, A_KV_HEADS),
        in_specs=[
            pl.BlockSpec((1, nt, grp, TOK), lambda b, h: (b, 0, h, 0)),
            pl.BlockSpec((1, nt, 1, TOK, TOK), lambda b, h: (b, 0, 0, 0, 0)),
            pl.BlockSpec((1, nt, HEAD_DIM, TOK), lambda b, h: (b, 0, h, 0)),
        ],
        out_specs=pl.BlockSpec((1, nt, grp, TOK), lambda b, h: (b, 0, h, 0)),
        out_shape=jax.ShapeDtypeStruct((bsz, nt, MIX_WIDTH, TOK), _BF16),
        scratch_shapes=_attend_scratch(HEAD_DIM),
        compiler_params=_cparams("arbitrary", "arbitrary"),
        name="gqa_attn",
    )(qt, kt, vt)


def _diff_kernel(q_ref, k_ref, v_ref, lam_ref, sg_ref, o_ref, qcat_ref, s_ref, mx_ref, m_ref, acc_ref,
                 *, out_scale, lam_init):
    def build_qcat(j):
        for s in range(SLOT):
            qcat_ref[:, s * TOK:(s + 1) * TOK] = _pad_slot(q_ref[0, j, s * HEAD_DIM:(s + 1) * HEAD_DIM, :], s)

    def finalize(j):
        lv = lam_ref[...]
        lam = (jnp.exp(jnp.sum(lv[0:1] * lv[1:2], axis=1, keepdims=True))
               - jnp.exp(jnp.sum(lv[2:3] * lv[3:4], axis=1, keepdims=True)) + lam_init)
        sg = _lane_tile(sg_ref[...])
        attn = acc_ref[0:C_V_DIM, :] / acc_ref[C_V_DIM:C_V_DIM + 1, :]
        for hh in range(2):
            o = attn[:, 2 * hh * TOK:(2 * hh + 1) * TOK] - lam * attn[:, (2 * hh + 1) * TOK:(2 * hh + 2) * TOK]
            ms = jnp.sum(o * o, axis=0, keepdims=True) * (1.0 / C_V_DIM)
            y = o * lax.rsqrt(ms + EPS) * sg * out_scale
            o_ref[0, j, hh * C_V_DIM:(hh + 1) * C_V_DIM, :] = y.astype(_BF16)

    head_rows = lambda s: slice((s // 2) * C_V_DIM, (s // 2 + 1) * C_V_DIM)
    _attend_tiles(k_ref.shape[1], build_qcat,
                  lambda t, n: k_ref[0, t:t + n, 0].reshape(n * TOK, TOK),
                  lambda t, n, s: jnp.concatenate([v_ref[0, t + i, head_rows(s), :] for i in range(n)], axis=1),
                  finalize, qcat_ref, s_ref, mx_ref, m_ref, acc_ref)


def _diff_attention(qt, kt, vt, lamv, sg, lam_init):
    bsz, nt = qt.shape[:2]
    n_blk = kt.shape[2]
    grp = SLOT * HEAD_DIM
    return pl.pallas_call(
        functools.partial(_diff_kernel, out_scale=1.0 - lam_init, lam_init=lam_init),
        grid=(bsz, n_blk),
        in_specs=[
            pl.BlockSpec((1, nt, grp, TOK), lambda b, c: (b, 0, c, 0)),
            pl.BlockSpec((1, nt, 1, TOK, TOK), lambda b, c: (b, 0, c, 0, 0)),
            pl.BlockSpec((1, nt, 2 * C_V_DIM, TOK), lambda b, c: (b, 0, c, 0)),
            pl.BlockSpec((8, LANES), lambda b, c: (0, 0)),
            pl.BlockSpec((C_V_DIM, LANES), lambda b, c: (0, 0)),
        ],
        out_specs=pl.BlockSpec((1, nt, 2 * C_V_DIM, TOK), lambda b, c: (b, 0, c, 0)),
        out_shape=jax.ShapeDtypeStruct((bsz, nt, MIX_WIDTH, TOK), _BF16),
        scratch_shapes=_attend_scratch(C_V_DIM),
        compiler_params=_cparams("arbitrary", "arbitrary"),
        name="diff_attn",
    )(qt, kt, vt, lamv, sg)


N_BAND = 3


def _nbr_kernel(q_ref, kc_ref, k0_ref, k1_ref, k2_ref, vc_ref, v0_ref, v1_ref, v2_ref, bias_ref, o_ref,
                qcat_ref, s_ref, mx_ref, m_ref, acc_ref):
    is_lat = pl.program_id(0) > 0
    k_refs = (kc_ref, k0_ref, k1_ref, k2_ref)
    v_refs = (vc_ref, v0_ref, v1_ref, v2_ref)
    n_tiles = len(k_refs)
    n_quad = B_HEADS // SLOT
    head = lambda c, g: slice((c * SLOT + g) * HEAD_DIM, (c * SLOT + g + 1) * HEAD_DIM)

    def begin(c):
        for g in range(SLOT):
            qcat_ref[c % 2, :, g * TOK:(g + 1) * TOK] = _pad_slot(q_ref[0, 0, head(c, g), :], g)
        m_ref[c % 2] = jnp.full(m_ref.shape[1:], NEG_INF, _F32)
        acc_ref[c % 2] = jnp.zeros(acc_ref.shape[1:], _F32)

    def score(i):
        c, t = divmod(i, n_tiles)
        if t == 0:
            begin(c)
        bias = None if t == 0 else bias_ref[0, c, t - 1]
        _score_stage(k_refs[t][0, 0, c], qcat_ref.at[c % 2], s_ref, mx_ref, i % 2, bias=bias, keep=is_lat)

    score(0)
    for i in range(n_quad * n_tiles):
        c, t = divmod(i, n_tiles)
        if i + 1 < n_quad * n_tiles:
            score(i + 1)
        acc = acc_ref.at[c % 2]
        _softmax_value_stage(lambda g, c=c, t=t: v_refs[t][0, 0, head(c, g), :],
                             s_ref, mx_ref, m_ref.at[c % 2], acc, i % 2)
        if t == n_tiles - 1:
            o = acc[0:HEAD_DIM, :] / acc[HEAD_DIM:HEAD_DIM + 1, :]
            for g in range(SLOT):
                o_ref[0, 0, head(c, g), :] = o[:, g * TOK:(g + 1) * TOK].astype(_BF16)


def _nbr_attention(qt, kt, vt, bias):
    bsz, nt = qt.shape[:2]
    n_blk = kt.shape[2]
    n_lat = nt - 1
    tile = lambda j, b: (b, j, 0, 0)

    def band(d):
        return lambda j, b: (b, 1 + jnp.clip(j - 2, 0, n_lat - N_BAND) + d, 0, 0, 0)

    def vband(d):
        return lambda j, b: (b, 1 + jnp.clip(j - 2, 0, n_lat - N_BAND) + d, 0, 0)

    case = lambda j, b: (jnp.where(j <= 1, 0, jnp.where(j == nt - 1, 2, 1)), 0, 0, 0, 0)
    kspec = lambda im: pl.BlockSpec((1, 1, n_blk, TOK, TOK), im)
    vspec = lambda im: pl.BlockSpec((1, 1, MIX_WIDTH, TOK), im)
    return pl.pallas_call(
        _nbr_kernel,
        grid=(nt, bsz),
        in_specs=[
            pl.BlockSpec((1, 1, MIX_WIDTH, TOK), tile),
            kspec(lambda j, b: (b, 0, 0, 0, 0)), kspec(band(0)), kspec(band(1)), kspec(band(2)),
            vspec(lambda j, b: (b, 0, 0, 0)), vspec(vband(0)), vspec(vband(1)), vspec(vband(2)),
            pl.BlockSpec((1, B_HEADS // SLOT, N_BAND, TOK, NCOL), case),
        ],
        out_specs=pl.BlockSpec((1, 1, MIX_WIDTH, TOK), tile),
        out_shape=jax.ShapeDtypeStruct((bsz, nt, MIX_WIDTH, TOK), _BF16),
        scratch_shapes=[
            pltpu.VMEM((2, SLOT * HEAD_DIM, NCOL), _BF16),
            pltpu.VMEM((2, TOK, NCOL), _F32),
            pltpu.VMEM((2, 1, NCOL), _F32),
            pltpu.VMEM((2, 1, NCOL), _F32),
            pltpu.VMEM((2, HEAD_DIM + ONES_ROWS, NCOL), _F32),
        ],
        compiler_params=_cparams("arbitrary", "arbitrary"),
        name="nbr_attn",
    )(qt, kt, kt, kt, kt, vt, vt, vt, vt, bias)


def _nbr_bias(rpb):
    rows = 4096 // GRID_W
    q_per = TOK // GRID_W
    hi = lax.Precision.HIGHEST
    kc = np.arange(GRID_W)[:, None]
    qc = np.arange(GRID_W)[None, :]
    d_col = np.clip(kc - qc + WIN_W - 1, 0, 2 * WIN_W - 2)
    cs = np.clip(qc - WIN_W // 2, 0, GRID_W - WIN_W)
    col_in = (kc >= cs) & (kc < cs + WIN_W)
    e_col = (d_col[None] == np.arange(2 * WIN_W - 1)[:, None, None]).astype(np.float32)
    by_col = jnp.einsum("hdj,jkq->hdkq", rpb.astype(_F32), e_col, precision=hi)
    tables = []
    for i in (0, 1, rows // q_per - 1):
        s = min(max(i - 1, 0), rows // q_per - N_BAND)
        kr = (q_per * s + np.arange(N_BAND * q_per))[:, None]
        qr = (q_per * i + np.arange(q_per))[None, :]
        d_row = np.clip(kr - qr + WIN_H - 1, 0, 2 * WIN_H - 2)
        rs = np.clip(qr - WIN_H // 2, 0, rows - WIN_H)
        row_in = (kr >= rs) & (kr < rs + WIN_H)
        e_row = (d_row[None] == np.arange(2 * WIN_H - 1)[:, None, None]).astype(np.float32)
        tab = jnp.einsum("dab,hdkq->hakbq", e_row, by_col, precision=hi)
        inside = row_in[:, None, :, None] & col_in[None, :, None, :]
        tab = jnp.where(inside[None], tab * LOG2E, NEG_INF)
        tab = tab.reshape(B_HEADS // SLOT, SLOT, N_BAND, q_per, GRID_W, q_per, GRID_W)
        tables.append(tab.transpose(0, 2, 3, 4, 1, 5, 6).reshape(B_HEADS // SLOT, N_BAND, TOK, NCOL))
    return jnp.stack(tables, axis=0)


def _rope_tables(s):
    tok = jnp.arange(s, dtype=jnp.int32)
    rows = (tok // GRID_W).astype(_F32)
    cols = (tok % GRID_W).astype(_F32)
    n_freq = HEAD_DIM // 4
    inv_freq = ROPE_BASE ** (-jnp.arange(n_freq, dtype=_F32) / n_freq)
    ang = jnp.concatenate([rows[:, None] * inv_freq, cols[:, None] * inv_freq], axis=1)
    ang = jnp.concatenate([jnp.zeros((CTX_LEN, 32), _F32), ang], axis=0)
    return jnp.cos(ang).T, jnp.sin(ang).T


def _col(v):
    return jnp.broadcast_to(v[..., None], v.shape + (LANES,))


def _lambda_init(layer):
    return 0.8 - 0.6 * math.exp(-0.3 * layer)


def kernel(x, c, ctx, c_ctx, norm_g, ada_w, ada_b, a_w_in, a_q_g, a_k_g, a_w_out, b_w_in, b_q_g, b_k_g, b_rpb, b_w_out, c_w_in, c_q_g, c_k_g, c_lam_q1, c_lam_k1, c_lam_q2, c_lam_k2, c_subln_g, c_w_out):
    bsz, s, _ = x.shape
    assert c.shape[0] + 1 <= 16
    scale = HEAD_DIM ** -0.5

    cvec = jnp.zeros((16, D_MODEL), _F32).at[:bsz].set(c).at[bsz].set(c_ctx)
    mod = _modulation(cvec, ada_w, ada_b)[:, :bsz + 1]
    sh, sc, gt = jnp.split(mod, 3, axis=-1)
    acol = _col(norm_g[:, None, :] * (1.0 + sc))
    shcol = _col(sh)
    gtcol = _col(gt)

    cos_t, sin_t = _rope_tables(s)
    w_ins, w_outs, q_gs, k_gs = (a_w_in, b_w_in, c_w_in), (a_w_out, b_w_out, c_w_out), \
        (a_q_g, b_q_g, c_q_g), (a_k_g, b_k_g, c_k_g)
    n_kv = (A_KV_HEADS * HEAD_DIM, MIX_WIDTH, MIX_WIDTH)

    ot = zt = xt = wo_t = None
    for i in range(DEPTH):
        kind, j = i % 3, i // 3
        dims = dict(n_q=MIX_WIDTH, n_k=n_kv[kind], n_v=n_kv[kind], rope=kind != 1)
        proj_args = (acol[i], shcol[i], w_ins[kind][j].T.astype(_BF16), cos_t, sin_t,
                     _col(q_gs[kind][j] * (scale * LOG2E)), _col(k_gs[kind][j]))
        if i == 0:
            xt, qt, kt, vt, zt = _first_proj(ctx, x, proj_args, **dims)
        else:
            xt, qt, kt, vt, zt = _out_in_proj(ot, zt, wo_t, gtcol[i - 1], xt, proj_args, **dims)
        if kind == 0:
            ot = _gqa_attention(qt, kt, vt)
        elif kind == 1:
            ot = _nbr_attention(qt, kt, vt, _nbr_bias(b_rpb[j]))
        else:
            lamv = jnp.zeros((8, LANES), _F32).at[0:4, :HEAD_DIM].set(
                jnp.stack([c_lam_q1[j], c_lam_k1[j], c_lam_q2[j], c_lam_k2[j]]).astype(_F32))
            ot = _diff_attention(qt, kt, vt, lamv, _col(c_subln_g[j]), _lambda_init(i))
        wo_t = w_outs[kind][j].T.astype(_BF16)

    return _last_proj(ot, zt, wo_t, gtcol[DEPTH - 1], xt)
```

```python
import functools
import math

import jax
import jax.numpy as jnp
import numpy as np
from jax import lax
from jax.experimental import pallas as pl
from jax.experimental.pallas import tpu as pltpu

D_MODEL = 1024
DEPTH = 4
GRID_W = 64
CTX_LEN = 256
HEAD_DIM = 64
MIX_WIDTH = 1024
A_HEADS = 16
A_KV_HEADS = 4
B_HEADS = 16
C_HEADS = 8
C_V_DIM = 128
WIN_H = 8
WIN_W = 16
ROPE_BASE = 10000.0
EPS = 1e-6
NEG_INF = -1e30
LOG2E = 1.4426950408889634

LANES = 128
TOK = 256
SLOT = 4
VMEM_LIMIT = 56 * 1024 * 1024

_BF16 = jnp.bfloat16
_F32 = jnp.float32


def _cparams(*sem):
    return pltpu.CompilerParams(dimension_semantics=sem, vmem_limit_bytes=VMEM_LIMIT)


def _mod_kernel(c_ref, w_ref, b_ref, o_ref):
    cv = c_ref[...]
    s = cv * jax.nn.sigmoid(cv)
    o_ref[0] = jnp.dot(s, w_ref[0], preferred_element_type=_F32) + b_ref[0]


def _modulation(cvec, ada_w, ada_b):
    n_col = 512
    return pl.pallas_call(
        _mod_kernel,
        grid=(DEPTH, 3 * D_MODEL // n_col),
        in_specs=[
            pl.BlockSpec((16, D_MODEL), lambda l, n: (0, 0)),
            pl.BlockSpec((1, D_MODEL, n_col), lambda l, n: (l, 0, n)),
            pl.BlockSpec((1, 1, n_col), lambda l, n: (l, 0, n)),
        ],
        out_specs=pl.BlockSpec((1, 16, n_col), lambda l, n: (l, 0, n)),
        out_shape=jax.ShapeDtypeStruct((DEPTH, 16, 3 * D_MODEL), _F32),
        compiler_params=_cparams("arbitrary", "arbitrary"),
        name="adaln_mod",
    )(cvec, ada_w, ada_b.reshape(DEPTH, 1, 3 * D_MODEL))


def _lane_tile(col_ref_val):
    return jnp.concatenate([col_ref_val] * (TOK // LANES), axis=1)


def _head_norm_rope(u, gain, cos, sin, rope):
    ms = jnp.sum(u * u, axis=0, keepdims=True) * (1.0 / HEAD_DIM)
    y = u * lax.rsqrt(ms + EPS) * gain
    if not rope:
        return y
    x1r, x2r, x1c, x2c = y[0:16], y[16:32], y[32:48], y[48:64]
    cr, cc = cos[0:16], cos[16:32]
    sr, sc = sin[0:16], sin[16:32]
    return jnp.concatenate(
        [x1r * cr - x2r * sr, x2r * cr + x1r * sr, x1c * cc - x2c * sc, x2c * cc + x1c * sc], axis=0)


def _proj_body(x, a_ref, sh_ref, w_ref, cos_ref, sin_ref, qg_ref, kg_ref,
               q_ref, k_ref, v_ref, g_ref, *, n_q, n_k, n_v, rope):
    ms = jnp.sum(x * x, axis=0, keepdims=True) * (1.0 / D_MODEL)
    h = x * lax.rsqrt(ms + EPS) * _lane_tile(a_ref[0]) + _lane_tile(sh_ref[0])
    hb = h.astype(_BF16)
    cos, sin = cos_ref[...], sin_ref[...]
    qg, kg = _lane_tile(qg_ref[...]), _lane_tile(kg_ref[...])

    uq = jnp.dot(w_ref[0:n_q, :], hb, preferred_element_type=_F32)
    for hd in range(n_q // HEAD_DIM):
        r = slice(hd * HEAD_DIM, (hd + 1) * HEAD_DIM)
        q_ref[0, 0, r, :] = _head_norm_rope(uq[r], qg, cos, sin, rope).astype(_BF16)

    uk = jnp.dot(w_ref[n_q:n_q + n_k, :], hb, preferred_element_type=_F32)
    for blk in range(n_k // TOK):
        heads = [_head_norm_rope(uk[blk * TOK + s * HEAD_DIM: blk * TOK + (s + 1) * HEAD_DIM],
                                 kg, cos, sin, rope) for s in range(SLOT)]
        kt = jnp.concatenate(heads, axis=0)
        k_ref[0, 0, blk] = kt.T.astype(_BF16)

    o = n_q + n_k
    uv = jnp.dot(w_ref[o:o + n_v, :], hb, preferred_element_type=_F32)
    v_ref[0, 0] = uv.astype(_BF16)

    o = o + n_v
    uz = jnp.dot(w_ref[o:o + MIX_WIDTH, :], hb, preferred_element_type=_F32)
    g_ref[0] = (uz * jax.nn.sigmoid(uz)).astype(_BF16)


def _residual_update(o_ref, g_ref, w_ref, gt_ref, x_ref):
    og = (o_ref[0, 0].astype(_F32) * g_ref[0].astype(_F32)).astype(_BF16)
    y = jnp.dot(w_ref[...], og, preferred_element_type=_F32)
    return x_ref[0] + _lane_tile(gt_ref[0]) * y


def _first_kernel(ctx_ref, x_ref, *refs, **dims):
    xt_ref, proj_out = refs[7], refs[8:]
    x = jnp.where(pl.program_id(1) == 0, ctx_ref[0], x_ref[0]).T
    xt_ref[0] = x
    _proj_body(x, *refs[:7], *proj_out, **dims)


def _mid_kernel(o_ref, g_ref, wo_ref, gt_ref, x_ref, *refs, **dims):
    xt_ref, proj_out = refs[7], refs[8:]
    x = _residual_update(o_ref, g_ref, wo_ref, gt_ref, x_ref)
    xt_ref[0] = x
    _proj_body(x, *refs[:7], *proj_out, **dims)


def _last_kernel(o_ref, g_ref, wo_ref, gt_ref, x_ref, out_ref):
    out_ref[0] = _residual_update(o_ref, g_ref, wo_ref, gt_ref, x_ref).T


def _proj_specs(bsz, nt, n_in, n_q, n_k, n_v):
    col = lambda b, j: (jnp.where(j == 0, bsz, b), 0, 0)
    t = nt * TOK
    in_specs = [
        pl.BlockSpec((1, D_MODEL, LANES), col),
        pl.BlockSpec((1, D_MODEL, LANES), col),
        pl.BlockSpec((n_in, D_MODEL), lambda b, j: (0, 0)),
        pl.BlockSpec((32, TOK), lambda b, j: (0, j)),
        pl.BlockSpec((32, TOK), lambda b, j: (0, j)),
        pl.BlockSpec((HEAD_DIM, LANES), lambda b, j: (0, 0)),
        pl.BlockSpec((HEAD_DIM, LANES), lambda b, j: (0, 0)),
    ]
    out_specs = [
        pl.BlockSpec((1, D_MODEL, TOK), lambda b, j: (b, 0, j)),
        pl.BlockSpec((1, 1, n_q, TOK), lambda b, j: (b, j, 0, 0)),
        pl.BlockSpec((1, 1, n_k // TOK, TOK, TOK), lambda b, j: (b, j, 0, 0, 0)),
        pl.BlockSpec((1, 1, n_v, TOK), lambda b, j: (b, j, 0, 0)),
        pl.BlockSpec((1, MIX_WIDTH, TOK), lambda b, j: (b, 0, j)),
    ]
    out_shape = [
        jax.ShapeDtypeStruct((bsz, D_MODEL, t), _F32),
        jax.ShapeDtypeStruct((bsz, nt, n_q, TOK), _BF16),
        jax.ShapeDtypeStruct((bsz, nt, n_k // TOK, TOK, TOK), _BF16),
        jax.ShapeDtypeStruct((bsz, nt, n_v, TOK), _BF16),
        jax.ShapeDtypeStruct((bsz, MIX_WIDTH, t), _BF16),
    ]
    return in_specs, out_specs, out_shape


def _first_proj(ctx, x, proj_args, *, n_q, n_k, n_v, rope):
    bsz, s, _ = x.shape
    nt = (ctx.shape[1] + s) // TOK
    in_specs, out_specs, out_shape = _proj_specs(bsz, nt, proj_args[2].shape[0], n_q, n_k, n_v)
    return pl.pallas_call(
        functools.partial(_first_kernel, n_q=n_q, n_k=n_k, n_v=n_v, rope=rope),
        grid=(bsz, nt),
        in_specs=[
            pl.BlockSpec((1, TOK, D_MODEL), lambda b, j: (b, 0, 0)),
            pl.BlockSpec((1, TOK, D_MODEL), lambda b, j: (b, jnp.maximum(j - 1, 0), 0)),
        ] + in_specs,
        out_specs=out_specs,
        out_shape=out_shape,
        compiler_params=_cparams("arbitrary", "arbitrary"),
        name="first_proj",
    )(ctx, x, *proj_args)


def _out_in_proj(ot, gz, wo_t, gtcol, xt, proj_args, *, n_q, n_k, n_v, rope):
    bsz, _, t = xt.shape
    nt = t // TOK
    in_specs, out_specs, out_shape = _proj_specs(bsz, nt, proj_args[2].shape[0], n_q, n_k, n_v)
    col = lambda b, j: (jnp.where(j == 0, bsz, b), 0, 0)
    tile = lambda b, j: (b, 0, j)
    return pl.pallas_call(
        functools.partial(_mid_kernel, n_q=n_q, n_k=n_k, n_v=n_v, rope=rope),
        grid=(bsz, nt),
        in_specs=[
            pl.BlockSpec((1, 1, MIX_WIDTH, TOK), lambda b, j: (b, j, 0, 0)),
            pl.BlockSpec((1, MIX_WIDTH, TOK), tile),
            pl.BlockSpec((D_MODEL, MIX_WIDTH), lambda b, j: (0, 0)),
            pl.BlockSpec((1, D_MODEL, LANES), col),
            pl.BlockSpec((1, D_MODEL, TOK), tile),
        ] + in_specs,
        out_specs=out_specs,
        out_shape=out_shape,
        input_output_aliases={4: 0},
        compiler_params=_cparams("arbitrary", "arbitrary"),
        name="out_in_proj",
    )(ot, gz, wo_t, gtcol, xt, *proj_args)


def _last_proj(ot, gz, wo_t, gtcol, xt):
    bsz, _, t = xt.shape
    n_lat = t // TOK - 1
    tile = lambda b, j: (b, 0, j + 1)
    return pl.pallas_call(
        _last_kernel,
        grid=(bsz, n_lat),
        in_specs=[
            pl.BlockSpec((1, 1, MIX_WIDTH, TOK), lambda b, j: (b, j + 1, 0, 0)),
            pl.BlockSpec((1, MIX_WIDTH, TOK), tile),
            pl.BlockSpec((D_MODEL, MIX_WIDTH), lambda b, j: (0, 0)),
            pl.BlockSpec((1, D_MODEL, LANES), lambda b, j: (b, 0, 0)),
            pl.BlockSpec((1, D_MODEL, TOK), tile),
        ],
        out_specs=pl.BlockSpec((1, TOK, D_MODEL), lambda b, j: (b, j, 0)),
        out_shape=jax.ShapeDtypeStruct((bsz, n_lat * TOK, D_MODEL), _F32),
        compiler_params=_cparams("arbitrary", "arbitrary"),
        name="last_proj",
    )(ot, gz, wo_t, gtcol, xt)


def _pad_slot(qh, slot):
    z = jnp.zeros_like(qh)
    return jnp.concatenate([qh if s == slot else z for s in range(SLOT)], axis=0)


NCOL = SLOT * TOK
ONES_ROWS = 16


def _score_stage(k_tile, qcat_ref, s_ref, mx_ref, sb, rows=TOK, bias=None):
    s = jnp.dot(k_tile, qcat_ref[...], preferred_element_type=_F32)
    if bias is not None:
        s = s + bias
    s_ref[sb, 0:rows] = s
    mx_ref[sb] = jnp.max(s, axis=0, keepdims=True)


def _softmax_value_stage(v_tile, s_ref, mx_ref, m_ref, acc_ref, sb, rows=TOK):
    m_old = m_ref[...]
    m_new = jnp.maximum(m_old, mx_ref[sb])
    alpha = jnp.exp2(m_old - m_new)
    m_ref[...] = m_new
    for j in range(SLOT):
        cols = slice(j * TOK, (j + 1) * TOK)
        v = v_tile(j)
        v_aug = jnp.concatenate([v, jnp.ones((ONES_ROWS, v.shape[1]), _BF16)], axis=0)
        p = jnp.exp2(s_ref[sb, 0:rows, cols] - m_new[:, cols]).astype(_BF16)
        acc_ref[:, cols] = (alpha[:, cols] * acc_ref[:, cols]
                            + jnp.dot(v_aug, p, preferred_element_type=_F32))


LAST_BUF = 2
GQA_KEY_GROUP = 1
DIFF_KEY_GROUP = 2


def _attend_tiles(n_t, build_qcat, k_tile, v_tile, finalize, qcat_ref, s_ref, mx_ref, m_ref, acc_ref):
    key_group = s_ref.shape[1] // TOK
    groups = [(0, 1)] + [(t, key_group) for t in range(1, n_t, key_group)]
    assert (n_t - 1) % key_group == 0
    n_g = len(groups)
    buf = lambda g: LAST_BUF if g == n_g - 1 else g % 2

    def score(g):
        t, n = groups[g]
        _score_stage(k_tile(t, n), qcat_ref, s_ref, mx_ref, buf(g), rows=n * TOK)

    def stage(g):
        t, n = groups[g]
        _softmax_value_stage(functools.partial(v_tile, t, n), s_ref, mx_ref, m_ref, acc_ref, buf(g), rows=n * TOK)

    def reset():
        m_ref[...] = jnp.full(m_ref.shape, NEG_INF, _F32)
        acc_ref[...] = jnp.zeros(acc_ref.shape, _F32)

    def start(j):
        build_qcat(j)
        score(0)

    reset()
    start(0)
    stage(0)
    finalize(0)
    reset()
    start(1)

    def tile(j, carry):
        for g in range(n_g - 1):
            score(g + 1)
            stage(g)
        stage(n_g - 1)
        finalize(j)
        reset()
        start(jnp.minimum(j + 1, n_t - 1))
        return carry

    lax.fori_loop(1, n_t, tile, 0)


def _attend_scratch(dv, key_group):
    return [
        pltpu.VMEM((SLOT * HEAD_DIM, NCOL), _BF16),
        pltpu.VMEM((3, key_group * TOK, NCOL), _F32),
        pltpu.VMEM((3, 1, NCOL), _F32),
        pltpu.VMEM((1, NCOL), _F32),
        pltpu.VMEM((dv + ONES_ROWS, NCOL), _F32),
    ]


def _gqa_kernel(q_ref, k_ref, v_ref, o_ref, qcat_ref, s_ref, mx_ref, m_ref, acc_ref):
    kvh = pl.program_id(1)
    row_slot = lax.broadcasted_iota(jnp.int32, (SLOT * HEAD_DIM, TOK), 0) // HEAD_DIM

    def build_qcat(j):
        for g in range(SLOT):
            qh = q_ref[0, j, g * HEAD_DIM:(g + 1) * HEAD_DIM, :].astype(_F32)
            qrep = jnp.concatenate([qh] * SLOT, axis=0)
            qcat_ref[:, g * TOK:(g + 1) * TOK] = jnp.where(row_slot == kvh, qrep, 0.0).astype(_BF16)

    def finalize(j):
        o = acc_ref[0:HEAD_DIM, :] / acc_ref[HEAD_DIM:HEAD_DIM + 1, :]
        for g in range(SLOT):
            o_ref[0, j, g * HEAD_DIM:(g + 1) * HEAD_DIM, :] = o[:, g * TOK:(g + 1) * TOK].astype(_BF16)

    _attend_tiles(k_ref.shape[1], build_qcat,
                  lambda t, n: k_ref[0, t:t + n, 0].reshape(n * TOK, TOK),
                  lambda t, n, g: jnp.concatenate([v_ref[0, t + i] for i in range(n)], axis=1),
                  finalize, qcat_ref, s_ref, mx_ref, m_ref, acc_ref)


def _gqa_attention(qt, kt, vt):
    bsz, nt = qt.shape[:2]
    grp = SLOT * HEAD_DIM
    return pl.pallas_call(
        _gqa_kernel,
        grid=(bsz, A_KV_HEADS),
        in_specs=[
            pl.BlockSpec((1, nt, grp, TOK), lambda b, h: (b, 0, h, 0)),
            pl.BlockSpec((1, nt, 1, TOK, TOK), lambda b, h: (b, 0, 0, 0, 0)),
            pl.BlockSpec((1, nt, HEAD_DIM, TOK), lambda b, h: (b, 0, h, 0)),
        ],
        out_specs=pl.BlockSpec((1, nt, grp, TOK), lambda b, h: (b, 0, h, 0)),
        out_shape=jax.ShapeDtypeStruct((bsz, nt, MIX_WIDTH, TOK), _BF16),
        scratch_shapes=_attend_scratch(HEAD_DIM, GQA_KEY_GROUP),
        compiler_params=_cparams("arbitrary", "arbitrary"),
        name="gqa_attn",
    )(qt, kt, vt)


def _diff_kernel(q_ref, k_ref, v_ref, lam_ref, sg_ref, o_ref, qcat_ref, s_ref, mx_ref, m_ref, acc_ref,
                 *, out_scale, lam_init):
    def build_qcat(j):
        for s in range(SLOT):
            qcat_ref[:, s * TOK:(s + 1) * TOK] = _pad_slot(q_ref[0, j, s * HEAD_DIM:(s + 1) * HEAD_DIM, :], s)

    def finalize(j):
        lv = lam_ref[...]
        lam = (jnp.exp(jnp.sum(lv[0:1] * lv[1:2], axis=1, keepdims=True))
               - jnp.exp(jnp.sum(lv[2:3] * lv[3:4], axis=1, keepdims=True)) + lam_init)
        sg = _lane_tile(sg_ref[...])
        attn = acc_ref[0:C_V_DIM, :] / acc_ref[C_V_DIM:C_V_DIM + 1, :]
        for hh in range(2):
            o = attn[:, 2 * hh * TOK:(2 * hh + 1) * TOK] - lam * attn[:, (2 * hh + 1) * TOK:(2 * hh + 2) * TOK]
            ms = jnp.sum(o * o, axis=0, keepdims=True) * (1.0 / C_V_DIM)
            y = o * lax.rsqrt(ms + EPS) * sg * out_scale
            o_ref[0, j, hh * C_V_DIM:(hh + 1) * C_V_DIM, :] = y.astype(_BF16)

    head_rows = lambda s: slice((s // 2) * C_V_DIM, (s // 2 + 1) * C_V_DIM)
    _attend_tiles(k_ref.shape[1], build_qcat,
                  lambda t, n: k_ref[0, t:t + n, 0].reshape(n * TOK, TOK),
                  lambda t, n, s: jnp.concatenate([v_ref[0, t + i, head_rows(s), :] for i in range(n)], axis=1),
                  finalize, qcat_ref, s_ref, mx_ref, m_ref, acc_ref)


def _diff_attention(qt, kt, vt, lamv, sg, lam_init):
    bsz, nt = qt.shape[:2]
    n_blk = kt.shape[2]
    grp = SLOT * HEAD_DIM
    return pl.pallas_call(
        functools.partial(_diff_kernel, out_scale=1.0 - lam_init, lam_init=lam_init),
        grid=(bsz, n_blk),
        in_specs=[
            pl.BlockSpec((1, nt, grp, TOK), lambda b, c: (b, 0, c, 0)),
            pl.BlockSpec((1, nt, 1, TOK, TOK), lambda b, c: (b, 0, c, 0, 0)),
            pl.BlockSpec((1, nt, 2 * C_V_DIM, TOK), lambda b, c: (b, 0, c, 0)),
            pl.BlockSpec((8, LANES), lambda b, c: (0, 0)),
            pl.BlockSpec((C_V_DIM, LANES), lambda b, c: (0, 0)),
        ],
        out_specs=pl.BlockSpec((1, nt, 2 * C_V_DIM, TOK), lambda b, c: (b, 0, c, 0)),
        out_shape=jax.ShapeDtypeStruct((bsz, nt, MIX_WIDTH, TOK), _BF16),
        scratch_shapes=_attend_scratch(C_V_DIM, DIFF_KEY_GROUP),
        compiler_params=_cparams("arbitrary", "arbitrary"),
        name="diff_attn",
    )(qt, kt, vt, lamv, sg)


N_BAND = 3
NBR_QUADS_PER_TRIP = 4


def _nbr_kernel(q_ref, kc_ref, k0_ref, k1_ref, k2_ref, vc_ref, v0_ref, v1_ref, v2_ref, bias_ref, o_ref,
                qcat_ref, s_ref, mx_ref, m_ref, acc_ref):
    k_refs = (kc_ref, k0_ref, k1_ref, k2_ref)
    v_refs = (vc_ref, v0_ref, v1_ref, v2_ref)
    n_tiles = len(k_refs)
    n_quad = B_HEADS // SLOT
    n_slot = NBR_QUADS_PER_TRIP * n_tiles

    def trip(u, carry):
        quad = lambda i: NBR_QUADS_PER_TRIP * u + i // n_tiles
        par = lambda i: (i // n_tiles) % 2

        def head(i, g):
            return pl.ds(pl.multiple_of((quad(i) * SLOT + g) * HEAD_DIM, HEAD_DIM), HEAD_DIM)

        def score(i):
            c, t = quad(i), i % n_tiles
            if t == 0:
                for g in range(SLOT):
                    qcat_ref[par(i), :, g * TOK:(g + 1) * TOK] = _pad_slot(q_ref[0, 0, head(i, g), :], g)
                m_ref[par(i)] = jnp.full(m_ref.shape[1:], NEG_INF, _F32)
                acc_ref[par(i)] = jnp.zeros(acc_ref.shape[1:], _F32)
            bias = None if t == 0 else bias_ref[0, c, t - 1]
            _score_stage(k_refs[t][0, 0, c], qcat_ref.at[par(i)], s_ref, mx_ref, i % 2, bias=bias)

        score(0)
        for i in range(n_slot):
            t = i % n_tiles
            if i + 1 < n_slot:
                score(i + 1)
            acc = acc_ref.at[par(i)]
            _softmax_value_stage(lambda g, i=i, t=t: v_refs[t][0, 0, head(i, g), :],
                                 s_ref, mx_ref, m_ref.at[par(i)], acc, i % 2)
            if t == n_tiles - 1:
                o = acc[0:HEAD_DIM, :] / acc[HEAD_DIM:HEAD_DIM + 1, :]
                for g in range(SLOT):
                    o_ref[0, 0, head(i, g), :] = o[:, g * TOK:(g + 1) * TOK].astype(_BF16)
        return carry

    lax.fori_loop(0, n_quad // NBR_QUADS_PER_TRIP, trip, 0)


def _nbr_attention(qt, kt, vt, bias):
    bsz, nt = qt.shape[:2]
    n_blk = kt.shape[2]
    n_lat = nt - 1
    tile = lambda j, b: (b, j, 0, 0)

    def band(d):
        return lambda j, b: (b, 1 + jnp.clip(j - 2, 0, n_lat - N_BAND) + d, 0, 0, 0)

    def vband(d):
        return lambda j, b: (b, 1 + jnp.clip(j - 2, 0, n_lat - N_BAND) + d, 0, 0)

    case = lambda j, b: (jnp.where(j == 0, 3, jnp.where(j == 1, 0, jnp.where(j == nt - 1, 2, 1))), 0, 0, 0, 0)
    kspec = lambda im: pl.BlockSpec((1, 1, n_blk, TOK, TOK), im)
    vspec = lambda im: pl.BlockSpec((1, 1, MIX_WIDTH, TOK), im)
    return pl.pallas_call(
        _nbr_kernel,
        grid=(nt, bsz),
        in_specs=[
            pl.BlockSpec((1, 1, MIX_WIDTH, TOK), tile),
            kspec(lambda j, b: (b, 0, 0, 0, 0)), kspec(band(0)), kspec(band(1)), kspec(band(2)),
            vspec(lambda j, b: (b, 0, 0, 0)), vspec(vband(0)), vspec(vband(1)), vspec(vband(2)),
            pl.BlockSpec((1, B_HEADS // SLOT, N_BAND, TOK, NCOL), case),
        ],
        out_specs=pl.BlockSpec((1, 1, MIX_WIDTH, TOK), tile),
        out_shape=jax.ShapeDtypeStruct((bsz, nt, MIX_WIDTH, TOK), _BF16),
        scratch_shapes=[
            pltpu.VMEM((2, SLOT * HEAD_DIM, NCOL), _BF16),
            pltpu.VMEM((2, TOK, NCOL), _F32),
            pltpu.VMEM((2, 1, NCOL), _F32),
            pltpu.VMEM((2, 1, NCOL), _F32),
            pltpu.VMEM((2, HEAD_DIM + ONES_ROWS, NCOL), _F32),
        ],
        compiler_params=_cparams("arbitrary", "arbitrary"),
        name="nbr_attn",
    )(qt, kt, kt, kt, kt, vt, vt, vt, vt, bias)


def _nbr_bias(rpb):
    rows = 4096 // GRID_W
    q_per = TOK // GRID_W
    hi = lax.Precision.HIGHEST
    kc = np.arange(GRID_W)[:, None]
    qc = np.arange(GRID_W)[None, :]
    d_col = np.clip(kc - qc + WIN_W - 1, 0, 2 * WIN_W - 2)
    cs = np.clip(qc - WIN_W // 2, 0, GRID_W - WIN_W)
    col_in = (kc >= cs) & (kc < cs + WIN_W)
    e_col = (d_col[None] == np.arange(2 * WIN_W - 1)[:, None, None]).astype(np.float32)
    by_col = jnp.einsum("hdj,jkq->hdkq", rpb.astype(_F32), e_col, precision=hi)
    by_col = by_col.reshape(B_HEADS // SLOT, SLOT, 2 * WIN_H - 1, GRID_W, GRID_W)
    tables = []
    for i in (0, 1, rows // q_per - 1):
        s = min(max(i - 1, 0), rows // q_per - N_BAND)
        kr = (q_per * s + np.arange(N_BAND * q_per))[:, None]
        qr = (q_per * i + np.arange(q_per))[None, :]
        d_row = np.clip(kr - qr + WIN_H - 1, 0, 2 * WIN_H - 2)
        rs = np.clip(qr - WIN_H // 2, 0, rows - WIN_H)
        row_in = (kr >= rs) & (kr < rs + WIN_H)
        e_row = (d_row[None] == np.arange(2 * WIN_H - 1)[:, None, None]).astype(np.float32)
        tab = jnp.einsum("dab,cgdkq->cakgbq", e_row, by_col, precision=hi)
        inside = row_in[:, None, None, :, None] & col_in[None, :, None, None, :]
        tab = jnp.where(inside[None], tab * LOG2E, NEG_INF)
        tables.append(tab.reshape(B_HEADS // SLOT, N_BAND, TOK, NCOL))
    tables.append(jnp.full_like(tables[0], NEG_INF))
    return jnp.stack(tables, axis=0)


def _rope_tables(s):
    tok = jnp.arange(s, dtype=jnp.int32)
    rows = (tok // GRID_W).astype(_F32)
    cols = (tok % GRID_W).astype(_F32)
    n_freq = HEAD_DIM // 4
    inv_freq = ROPE_BASE ** (-jnp.arange(n_freq, dtype=_F32) / n_freq)
    ang = jnp.concatenate([rows[:, None] * inv_freq, cols[:, None] * inv_freq], axis=1)
    ang = jnp.concatenate([jnp.zeros((CTX_LEN, 32), _F32), ang], axis=0)
    return jnp.cos(ang).T, jnp.sin(ang).T


def _col(v):
    return jnp.broadcast_to(v[..., None], v.shape + (LANES,))


def _lambda_init(layer):
    return 0.8 - 0.6 * math.exp(-0.3 * layer)


def kernel(x, c, ctx, c_ctx, norm_g, ada_w, ada_b, a_w_in, a_q_g, a_k_g, a_w_out, b_w_in, b_q_g, b_k_g, b_rpb, b_w_out, c_w_in, c_q_g, c_k_g, c_lam_q1, c_lam_k1, c_lam_q2, c_lam_k2, c_subln_g, c_w_out):
    bsz, s, _ = x.shape
    assert c.shape[0] + 1 <= 16
    scale = HEAD_DIM ** -0.5

    cvec = jnp.zeros((16, D_MODEL), _F32).at[:bsz].set(c).at[bsz].set(c_ctx)
    mod = _modulation(cvec, ada_w, ada_b)[:, :bsz + 1]
    sh, sc, gt = jnp.split(mod, 3, axis=-1)
    acol = _col(norm_g[:, None, :] * (1.0 + sc))
    shcol = _col(sh)
    gtcol = _col(gt)

    cos_t, sin_t = _rope_tables(s)
    w_ins, w_outs, q_gs, k_gs = (a_w_in, b_w_in, c_w_in), (a_w_out, b_w_out, c_w_out), \
        (a_q_g, b_q_g, c_q_g), (a_k_g, b_k_g, c_k_g)
    n_kv = (A_KV_HEADS * HEAD_DIM, MIX_WIDTH, MIX_WIDTH)

    ot = zt = xt = wo_t = None
    for i in range(DEPTH):
        kind, j = i % 3, i // 3
        dims = dict(n_q=MIX_WIDTH, n_k=n_kv[kind], n_v=n_kv[kind], rope=kind != 1)
        proj_args = (acol[i], shcol[i], w_ins[kind][j].T.astype(_BF16), cos_t, sin_t,
                     _col(q_gs[kind][j] * (scale * LOG2E)), _col(k_gs[kind][j]))
        if i == 0:
            xt, qt, kt, vt, zt = _first_proj(ctx, x, proj_args, **dims)
        else:
            xt, qt, kt, vt, zt = _out_in_proj(ot, zt, wo_t, gtcol[i - 1], xt, proj_args, **dims)
        if kind == 0:
            ot = _gqa_attention(qt, kt, vt)
        elif kind == 1:
            ot = _nbr_attention(qt, kt, vt, _nbr_bias(b_rpb[j]))
        else:
            lamv = jnp.zeros((8, LANES), _F32).at[0:4, :HEAD_DIM].set(
                jnp.stack([c_lam_q1[j], c_lam_k1[j], c_lam_q2[j], c_lam_k2[j]]).astype(_F32))
            ot = _diff_attention(qt, kt, vt, lamv, _col(c_subln_g[j]), _lambda_init(i))
        wo_t = w_outs[kind][j].T.astype(_BF16)

    return _last_proj(ot, zt, wo_t, gtcol[DEPTH - 1], xt)
```

```python
import functools
import math

import jax
import jax.numpy as jnp
import numpy as np
from jax import lax
from jax.experimental import pallas as pl
from jax.experimental.pallas import tpu as pltpu

D_MODEL = 1024
DEPTH = 4
GRID_W = 64
CTX_LEN = 256
HEAD_DIM = 64
MIX_WIDTH = 1024
A_HEADS = 16
A_KV_HEADS = 4
B_HEADS = 16
C_HEADS = 8
C_V_DIM = 128
WIN_H = 8
WIN_W = 16
ROPE_BASE = 10000.0
EPS = 1e-6
NEG_INF = -1e30
LOG2E = 1.4426950408889634

LANES = 128
TOK = 256
SLOT = 4
VMEM_LIMIT = 56 * 1024 * 1024

_BF16 = jnp.bfloat16
_F32 = jnp.float32


def _cparams(*sem):
    return pltpu.CompilerParams(dimension_semantics=sem, vmem_limit_bytes=VMEM_LIMIT)


def _mod_kernel(c_ref, w_ref, b_ref, o_ref):
    cv = c_ref[...]
    s = cv * jax.nn.sigmoid(cv)
    o_ref[0] = jnp.dot(s, w_ref[0], preferred_element_type=_F32) + b_ref[0]


def _modulation(cvec, ada_w, ada_b):
    n_col = 512
    return pl.pallas_call(
        _mod_kernel,
        grid=(DEPTH, 3 * D_MODEL // n_col),
        in_specs=[
            pl.BlockSpec((16, D_MODEL), lambda l, n: (0, 0)),
            pl.BlockSpec((1, D_MODEL, n_col), lambda l, n: (l, 0, n)),
            pl.BlockSpec((1, 1, n_col), lambda l, n: (l, 0, n)),
        ],
        out_specs=pl.BlockSpec((1, 16, n_col), lambda l, n: (l, 0, n)),
        out_shape=jax.ShapeDtypeStruct((DEPTH, 16, 3 * D_MODEL), _F32),
        compiler_params=_cparams("arbitrary", "arbitrary"),
        name="adaln_mod",
    )(cvec, ada_w, ada_b.reshape(DEPTH, 1, 3 * D_MODEL))


def _lane_tile(col_ref_val):
    return jnp.concatenate([col_ref_val] * (TOK // LANES), axis=1)


def _head_norm_rope(u, gain, cos, sin, rope):
    ms = jnp.sum(u * u, axis=0, keepdims=True) * (1.0 / HEAD_DIM)
    y = u * lax.rsqrt(ms + EPS) * gain
    if not rope:
        return y
    x1r, x2r, x1c, x2c = y[0:16], y[16:32], y[32:48], y[48:64]
    cr, cc = cos[0:16], cos[16:32]
    sr, sc = sin[0:16], sin[16:32]
    return jnp.concatenate(
        [x1r * cr - x2r * sr, x2r * cr + x1r * sr, x1c * cc - x2c * sc, x2c * cc + x1c * sc], axis=0)


def _key_block(n_k):
    return (1, LANES) if n_k == A_KV_HEADS * HEAD_DIM else (SLOT, SLOT * HEAD_DIM)


def _proj_body(x, a_ref, sh_ref, w_ref, cos_ref, sin_ref, qg_ref, kg_ref,
               q_ref, k_ref, v_ref, g_ref, *, n_q, n_k, n_v, rope):
    ms = jnp.sum(x * x, axis=0, keepdims=True) * (1.0 / D_MODEL)
    h = x * lax.rsqrt(ms + EPS) * _lane_tile(a_ref[0]) + _lane_tile(sh_ref[0])
    hb = h.astype(_BF16)
    cos, sin = cos_ref[...], sin_ref[...]
    qg, kg = _lane_tile(qg_ref[...]), _lane_tile(kg_ref[...])

    uq = jnp.dot(w_ref[0:n_q, :], hb, preferred_element_type=_F32)
    for hd in range(n_q // HEAD_DIM):
        r = slice(hd * HEAD_DIM, (hd + 1) * HEAD_DIM)
        q_ref[0, 0, r, :] = _head_norm_rope(uq[r], qg, cos, sin, rope).astype(_BF16)

    uk = jnp.dot(w_ref[n_q:n_q + n_k, :], hb, preferred_element_type=_F32)
    hpb, lanes = _key_block(n_k)
    for blk in range(n_k // (hpb * HEAD_DIM)):
        heads = [_head_norm_rope(uk[(blk * hpb + s) * HEAD_DIM:(blk * hpb + s + 1) * HEAD_DIM],
                                 kg, cos, sin, rope) for s in range(hpb)]
        if hpb * HEAD_DIM < lanes:
            heads.append(jnp.zeros((lanes - hpb * HEAD_DIM, TOK), _F32))
        kt = jnp.concatenate(heads, axis=0)
        k_ref[0, 0, blk] = kt.T.astype(_BF16)

    o = n_q + n_k
    uv = jnp.dot(w_ref[o:o + n_v, :], hb, preferred_element_type=_F32)
    v_ref[0, 0] = uv.astype(_BF16)

    o = o + n_v
    uz = jnp.dot(w_ref[o:o + MIX_WIDTH, :], hb, preferred_element_type=_F32)
    g_ref[0] = (uz * jax.nn.sigmoid(uz)).astype(_BF16)


def _residual_update(o_ref, g_ref, w_ref, gt_ref, x_ref):
    og = (o_ref[0, 0].astype(_F32) * g_ref[0].astype(_F32)).astype(_BF16)
    y = jnp.dot(w_ref[...], og, preferred_element_type=_F32)
    return x_ref[0] + _lane_tile(gt_ref[0]) * y


def _first_kernel(ctx_ref, x_ref, *refs, **dims):
    xt_ref, proj_out = refs[7], refs[8:]
    x = jnp.where(pl.program_id(1) == 0, ctx_ref[0], x_ref[0]).T
    xt_ref[0] = x
    _proj_body(x, *refs[:7], *proj_out, **dims)


def _mid_kernel(o_ref, g_ref, wo_ref, gt_ref, x_ref, *refs, **dims):
    xt_ref, proj_out = refs[7], refs[8:]
    x = _residual_update(o_ref, g_ref, wo_ref, gt_ref, x_ref)
    xt_ref[0] = x
    _proj_body(x, *refs[:7], *proj_out, **dims)


def _last_kernel(o_ref, g_ref, wo_ref, gt_ref, x_ref, out_ref):
    out_ref[0] = _residual_update(o_ref, g_ref, wo_ref, gt_ref, x_ref).T


def _proj_specs(bsz, nt, n_in, n_q, n_k, n_v):
    col = lambda b, j: (jnp.where(j == 0, bsz, b), 0, 0)
    t = nt * TOK
    hpb, k_lanes = _key_block(n_k)
    n_kb = n_k // (hpb * HEAD_DIM)
    in_specs = [
        pl.BlockSpec((1, D_MODEL, LANES), col),
        pl.BlockSpec((1, D_MODEL, LANES), col),
        pl.BlockSpec((n_in, D_MODEL), lambda b, j: (0, 0)),
        pl.BlockSpec((32, TOK), lambda b, j: (0, j)),
        pl.BlockSpec((32, TOK), lambda b, j: (0, j)),
        pl.BlockSpec((HEAD_DIM, LANES), lambda b, j: (0, 0)),
        pl.BlockSpec((HEAD_DIM, LANES), lambda b, j: (0, 0)),
    ]
    out_specs = [
        pl.BlockSpec((1, D_MODEL, TOK), lambda b, j: (b, 0, j)),
        pl.BlockSpec((1, 1, n_q, TOK), lambda b, j: (b, j, 0, 0)),
        pl.BlockSpec((1, 1, n_kb, TOK, k_lanes), lambda b, j: (b, j, 0, 0, 0)),
        pl.BlockSpec((1, 1, n_v, TOK), lambda b, j: (b, j, 0, 0)),
        pl.BlockSpec((1, MIX_WIDTH, TOK), lambda b, j: (b, 0, j)),
    ]
    out_shape = [
        jax.ShapeDtypeStruct((bsz, D_MODEL, t), _F32),
        jax.ShapeDtypeStruct((bsz, nt, n_q, TOK), _BF16),
        jax.ShapeDtypeStruct((bsz, nt, n_kb, TOK, k_lanes), _BF16),
        jax.ShapeDtypeStruct((bsz, nt, n_v, TOK), _BF16),
        jax.ShapeDtypeStruct((bsz, MIX_WIDTH, t), _BF16),
    ]
    return in_specs, out_specs, out_shape


def _first_proj(ctx, x, proj_args, *, n_q, n_k, n_v, rope):
    bsz, s, _ = x.shape
    nt = (ctx.shape[1] + s) // TOK
    in_specs, out_specs, out_shape = _proj_specs(bsz, nt, proj_args[2].shape[0], n_q, n_k, n_v)
    return pl.pallas_call(
        functools.partial(_first_kernel, n_q=n_q, n_k=n_k, n_v=n_v, rope=rope),
        grid=(bsz, nt),
        in_specs=[
            pl.BlockSpec((1, TOK, D_MODEL), lambda b, j: (b, 0, 0)),
            pl.BlockSpec((1, TOK, D_MODEL), lambda b, j: (b, jnp.maximum(j - 1, 0), 0)),
        ] + in_specs,
        out_specs=out_specs,
        out_shape=out_shape,
        compiler_params=_cparams("arbitrary", "arbitrary"),
        name="first_proj",
    )(ctx, x, *proj_args)


def _out_in_proj(ot, gz, wo_t, gtcol, xt, proj_args, *, n_q, n_k, n_v, rope):
    bsz, _, t = xt.shape
    nt = t // TOK
    in_specs, out_specs, out_shape = _proj_specs(bsz, nt, proj_args[2].shape[0], n_q, n_k, n_v)
    col = lambda b, j: (jnp.where(j == 0, bsz, b), 0, 0)
    tile = lambda b, j: (b, 0, j)
    return pl.pallas_call(
        functools.partial(_mid_kernel, n_q=n_q, n_k=n_k, n_v=n_v, rope=rope),
        grid=(bsz, nt),
        in_specs=[
            pl.BlockSpec((1, 1, MIX_WIDTH, TOK), lambda b, j: (b, j, 0, 0)),
            pl.BlockSpec((1, MIX_WIDTH, TOK), tile),
            pl.BlockSpec((D_MODEL, MIX_WIDTH), lambda b, j: (0, 0)),
            pl.BlockSpec((1, D_MODEL, LANES), col),
            pl.BlockSpec((1, D_MODEL, TOK), tile),
        ] + in_specs,
        out_specs=out_specs,
        out_shape=out_shape,
        input_output_aliases={4: 0},
        compiler_params=_cparams("arbitrary", "arbitrary"),
        name="out_in_proj",
    )(ot, gz, wo_t, gtcol, xt, *proj_args)


def _last_proj(ot, gz, wo_t, gtcol, xt):
    bsz, _, t = xt.shape
    n_lat = t // TOK - 1
    tile = lambda b, j: (b, 0, j + 1)
    return pl.pallas_call(
        _last_kernel,
        grid=(bsz, n_lat),
        in_specs=[
            pl.BlockSpec((1, 1, MIX_WIDTH, TOK), lambda b, j: (b, j + 1, 0, 0)),
            pl.BlockSpec((1, MIX_WIDTH, TOK), tile),
            pl.BlockSpec((D_MODEL, MIX_WIDTH), lambda b, j: (0, 0)),
            pl.BlockSpec((1, D_MODEL, LANES), lambda b, j: (b, 0, 0)),
            pl.BlockSpec((1, D_MODEL, TOK), tile),
        ],
        out_specs=pl.BlockSpec((1, TOK, D_MODEL), lambda b, j: (b, j, 0)),
        out_shape=jax.ShapeDtypeStruct((bsz, n_lat * TOK, D_MODEL), _F32),
        compiler_params=_cparams("arbitrary", "arbitrary"),
        name="last_proj",
    )(ot, gz, wo_t, gtcol, xt)


def _pad_slot(qh, slot):
    z = jnp.zeros_like(qh)
    return jnp.concatenate([qh if s == slot else z for s in range(SLOT)], axis=0)


NCOL = SLOT * TOK
ONES_ROWS = 16


def _score_stage(k_tile, qcat_ref, s_ref, mx_ref, sb, rows=TOK, bias=None):
    s = jnp.dot(k_tile, qcat_ref[...], preferred_element_type=_F32)
    if bias is not None:
        s = s + bias
    s_ref[sb, 0:rows] = s
    mx_ref[sb] = jnp.max(s, axis=0, keepdims=True)


def _softmax_value_stage(v_tile, s_ref, mx_ref, m_ref, acc_ref, sb, rows=TOK):
    m_old = m_ref[...]
    m_new = jnp.maximum(m_old, mx_ref[sb])
    alpha = jnp.exp2(m_old - m_new)
    m_ref[...] = m_new
    for j in range(SLOT):
        cols = slice(j * TOK, (j + 1) * TOK)
        v = v_tile(j)
        v_aug = jnp.concatenate([v, jnp.ones((ONES_ROWS, v.shape[1]), _BF16)], axis=0)
        p = jnp.exp2(s_ref[sb, 0:rows, cols] - m_new[:, cols]).astype(_BF16)
        acc_ref[:, cols] = (alpha[:, cols] * acc_ref[:, cols]
                            + jnp.dot(v_aug, p, preferred_element_type=_F32))


LAST_BUF = 2
GQA_KEY_GROUP = 1
DIFF_KEY_GROUP = 2


def _attend_tiles(n_t, build_qcat, k_tile, v_tile, finalize, qcat_ref, s_ref, mx_ref, m_ref, acc_ref):
    key_group = s_ref.shape[1] // TOK
    groups = [(0, 1)] + [(t, key_group) for t in range(1, n_t, key_group)]
    assert (n_t - 1) % key_group == 0
    n_g = len(groups)
    buf = lambda g: LAST_BUF if g == n_g - 1 else g % 2

    def score(g):
        t, n = groups[g]
        _score_stage(k_tile(t, n), qcat_ref, s_ref, mx_ref, buf(g), rows=n * TOK)

    def stage(g):
        t, n = groups[g]
        _softmax_value_stage(functools.partial(v_tile, t, n), s_ref, mx_ref, m_ref, acc_ref, buf(g), rows=n * TOK)

    def reset():
        m_ref[...] = jnp.full(m_ref.shape, NEG_INF, _F32)
        acc_ref[...] = jnp.zeros(acc_ref.shape, _F32)

    def start(j):
        build_qcat(j)
        score(0)

    reset()
    start(0)
    stage(0)
    finalize(0)
    reset()
    start(1)

    def tile(j, carry):
        for g in range(n_g - 1):
            score(g + 1)
            stage(g)
        stage(n_g - 1)
        finalize(j)
        reset()
        start(jnp.minimum(j + 1, n_t - 1))
        return carry

    lax.fori_loop(1, n_t, tile, 0)


def _attend_scratch(dv, key_group, q_rows=SLOT * HEAD_DIM):
    return [
        pltpu.VMEM((q_rows, NCOL), _BF16),
        pltpu.VMEM((3, key_group * TOK, NCOL), _F32),
        pltpu.VMEM((3, 1, NCOL), _F32),
        pltpu.VMEM((1, NCOL), _F32),
        pltpu.VMEM((dv + ONES_ROWS, NCOL), _F32),
    ]


def _gqa_kernel(q_ref, k_ref, v_ref, o_ref, qcat_ref, s_ref, mx_ref, m_ref, acc_ref):
    def build_qcat(j):
        for g in range(SLOT):
            qh = q_ref[0, j, g * HEAD_DIM:(g + 1) * HEAD_DIM, :]
            qcat_ref[:, g * TOK:(g + 1) * TOK] = jnp.concatenate([qh, jnp.zeros_like(qh)], axis=0)

    def finalize(j):
        o = acc_ref[0:HEAD_DIM, :] / acc_ref[HEAD_DIM:HEAD_DIM + 1, :]
        for g in range(SLOT):
            o_ref[0, j, g * HEAD_DIM:(g + 1) * HEAD_DIM, :] = o[:, g * TOK:(g + 1) * TOK].astype(_BF16)

    _attend_tiles(k_ref.shape[1], build_qcat,
                  lambda t, n: k_ref[0, t:t + n, 0].reshape(n * TOK, LANES),
                  lambda t, n, g: jnp.concatenate([v_ref[0, t + i] for i in range(n)], axis=1),
                  finalize, qcat_ref, s_ref, mx_ref, m_ref, acc_ref)


def _gqa_attention(qt, kt, vt):
    bsz, nt = qt.shape[:2]
    grp = SLOT * HEAD_DIM
    return pl.pallas_call(
        _gqa_kernel,
        grid=(bsz, A_KV_HEADS),
        in_specs=[
            pl.BlockSpec((1, nt, grp, TOK), lambda b, h: (b, 0, h, 0)),
            pl.BlockSpec((1, nt, 1, TOK, LANES), lambda b, h: (b, 0, h, 0, 0)),
            pl.BlockSpec((1, nt, HEAD_DIM, TOK), lambda b, h: (b, 0, h, 0)),
        ],
        out_specs=pl.BlockSpec((1, nt, grp, TOK), lambda b, h: (b, 0, h, 0)),
        out_shape=jax.ShapeDtypeStruct((bsz, nt, MIX_WIDTH, TOK), _BF16),
        scratch_shapes=_attend_scratch(HEAD_DIM, GQA_KEY_GROUP, q_rows=LANES),
        compiler_params=_cparams("arbitrary", "arbitrary"),
        name="gqa_attn",
    )(qt, kt, vt)


def _diff_kernel(q_ref, k_ref, v_ref, lam_ref, sg_ref, o_ref, qcat_ref, s_ref, mx_ref, m_ref, acc_ref,
                 *, out_scale, lam_init):
    def build_qcat(j):
        for s in range(SLOT):
            qcat_ref[:, s * TOK:(s + 1) * TOK] = _pad_slot(q_ref[0, j, s * HEAD_DIM:(s + 1) * HEAD_DIM, :], s)

    def finalize(j):
        lv = lam_ref[...]
        lam = (jnp.exp(jnp.sum(lv[0:1] * lv[1:2], axis=1, keepdims=True))
               - jnp.exp(jnp.sum(lv[2:3] * lv[3:4], axis=1, keepdims=True)) + lam_init)
        sg = _lane_tile(sg_ref[...])
        attn = acc_ref[0:C_V_DIM, :] / acc_ref[C_V_DIM:C_V_DIM + 1, :]
        for hh in range(2):
            o = attn[:, 2 * hh * TOK:(2 * hh + 1) * TOK] - lam * attn[:, (2 * hh + 1) * TOK:(2 * hh + 2) * TOK]
            ms = jnp.sum(o * o, axis=0, keepdims=True) * (1.0 / C_V_DIM)
            y = o * lax.rsqrt(ms + EPS) * sg * out_scale
            o_ref[0, j, hh * C_V_DIM:(hh + 1) * C_V_DIM, :] = y.astype(_BF16)

    head_rows = lambda s: slice((s // 2) * C_V_DIM, (s // 2 + 1) * C_V_DIM)
    _attend_tiles(k_ref.shape[1], build_qcat,
                  lambda t, n: k_ref[0, t:t + n, 0].reshape(n * TOK, TOK),
                  lambda t, n, s: jnp.concatenate([v_ref[0, t + i, head_rows(s), :] for i in range(n)], axis=1),
                  finalize, qcat_ref, s_ref, mx_ref, m_ref, acc_ref)


def _diff_attention(qt, kt, vt, lamv, sg, lam_init):
    bsz, nt = qt.shape[:2]
    n_blk = kt.shape[2]
    grp = SLOT * HEAD_DIM
    return pl.pallas_call(
        functools.partial(_diff_kernel, out_scale=1.0 - lam_init, lam_init=lam_init),
        grid=(bsz, n_blk),
        in_specs=[
            pl.BlockSpec((1, nt, grp, TOK), lambda b, c: (b, 0, c, 0)),
            pl.BlockSpec((1, nt, 1, TOK, TOK), lambda b, c: (b, 0, c, 0, 0)),
            pl.BlockSpec((1, nt, 2 * C_V_DIM, TOK), lambda b, c: (b, 0, c, 0)),
            pl.BlockSpec((8, LANES), lambda b, c: (0, 0)),
            pl.BlockSpec((C_V_DIM, LANES), lambda b, c: (0, 0)),
        ],
        out_specs=pl.BlockSpec((1, nt, 2 * C_V_DIM, TOK), lambda b, c: (b, 0, c, 0)),
        out_shape=jax.ShapeDtypeStruct((bsz, nt, MIX_WIDTH, TOK), _BF16),
        scratch_shapes=_attend_scratch(C_V_DIM, DIFF_KEY_GROUP),
        compiler_params=_cparams("arbitrary", "arbitrary"),
        name="diff_attn",
    )(qt, kt, vt, lamv, sg)


N_BAND = 3
NBR_QUADS_PER_TRIP = 4


def _nbr_kernel(q_ref, kc_ref, k0_ref, k1_ref, k2_ref, vc_ref, v0_ref, v1_ref, v2_ref, bias_ref, o_ref,
                qcat_ref, s_ref, mx_ref, m_ref, acc_ref):
    k_refs = (kc_ref, k0_ref, k1_ref, k2_ref)
    v_refs = (vc_ref, v0_ref, v1_ref, v2_ref)
    n_tiles = len(k_refs)
    n_quad = B_HEADS // SLOT
    n_slot = NBR_QUADS_PER_TRIP * n_tiles

    def trip(u, carry):
        quad = lambda i: NBR_QUADS_PER_TRIP * u + i // n_tiles
        par = lambda i: (i // n_tiles) % 2

        def head(i, g):
            return pl.ds(pl.multiple_of((quad(i) * SLOT + g) * HEAD_DIM, HEAD_DIM), HEAD_DIM)

        def score(i):
            c, t = quad(i), i % n_tiles
            if t == 0:
                for g in range(SLOT):
                    qcat_ref[par(i), :, g * TOK:(g + 1) * TOK] = _pad_slot(q_ref[0, 0, head(i, g), :], g)
                m_ref[par(i)] = jnp.full(m_ref.shape[1:], NEG_INF, _F32)
                acc_ref[par(i)] = jnp.zeros(acc_ref.shape[1:], _F32)
            bias = None if t == 0 else bias_ref[0, c, t - 1]
            _score_stage(k_refs[t][0, 0, c], qcat_ref.at[par(i)], s_ref, mx_ref, i % 2, bias=bias)

        score(0)
        for i in range(n_slot):
            t = i % n_tiles
            if i + 1 < n_slot:
                score(i + 1)
            acc = acc_ref.at[par(i)]
            _softmax_value_stage(lambda g, i=i, t=t: v_refs[t][0, 0, head(i, g), :],
                                 s_ref, mx_ref, m_ref.at[par(i)], acc, i % 2)
            if t == n_tiles - 1:
                o = acc[0:HEAD_DIM, :] / acc[HEAD_DIM:HEAD_DIM + 1, :]
                for g in range(SLOT):
                    o_ref[0, 0, head(i, g), :] = o[:, g * TOK:(g + 1) * TOK].astype(_BF16)
        return carry

    lax.fori_loop(0, n_quad // NBR_QUADS_PER_TRIP, trip, 0)


def _nbr_attention(qt, kt, vt, bias):
    bsz, nt = qt.shape[:2]
    n_blk = kt.shape[2]
    n_lat = nt - 1
    tile = lambda j, b: (b, j, 0, 0)

    def band(d):
        return lambda j, b: (b, 1 + jnp.clip(j - 2, 0, n_lat - N_BAND) + d, 0, 0, 0)

    def vband(d):
        return lambda j, b: (b, 1 + jnp.clip(j - 2, 0, n_lat - N_BAND) + d, 0, 0)

    case = lambda j, b: (jnp.where(j == 0, 3, jnp.where(j == 1, 0, jnp.where(j == nt - 1, 2, 1))), 0, 0, 0, 0)
    kspec = lambda im: pl.BlockSpec((1, 1, n_blk, TOK, TOK), im)
    vspec = lambda im: pl.BlockSpec((1, 1, MIX_WIDTH, TOK), im)
    return pl.pallas_call(
        _nbr_kernel,
        grid=(nt, bsz),
        in_specs=[
            pl.BlockSpec((1, 1, MIX_WIDTH, TOK), tile),
            kspec(lambda j, b: (b, 0, 0, 0, 0)), kspec(band(0)), kspec(band(1)), kspec(band(2)),
            vspec(lambda j, b: (b, 0, 0, 0)), vspec(vband(0)), vspec(vband(1)), vspec(vband(2)),
            pl.BlockSpec((1, B_HEADS // SLOT, N_BAND, TOK, NCOL), case),
        ],
        out_specs=pl.BlockSpec((1, 1, MIX_WIDTH, TOK), tile),
        out_shape=jax.ShapeDtypeStruct((bsz, nt, MIX_WIDTH, TOK), _BF16),
        scratch_shapes=[
            pltpu.VMEM((2, SLOT * HEAD_DIM, NCOL), _BF16),
            pltpu.VMEM((2, TOK, NCOL), _F32),
            pltpu.VMEM((2, 1, NCOL), _F32),
            pltpu.VMEM((2, 1, NCOL), _F32),
            pltpu.VMEM((2, HEAD_DIM + ONES_ROWS, NCOL), _F32),
        ],
        compiler_params=_cparams("arbitrary", "arbitrary"),
        name="nbr_attn",
    )(qt, kt, kt, kt, kt, vt, vt, vt, vt, bias)


def _nbr_bias(rpb):
    rows = 4096 // GRID_W
    q_per = TOK // GRID_W
    hi = lax.Precision.HIGHEST
    kc = np.arange(GRID_W)[:, None]
    qc = np.arange(GRID_W)[None, :]
    d_col = np.clip(kc - qc + WIN_W - 1, 0, 2 * WIN_W - 2)
    cs = np.clip(qc - WIN_W // 2, 0, GRID_W - WIN_W)
    col_in = (kc >= cs) & (kc < cs + WIN_W)
    e_col = (d_col[None] == np.arange(2 * WIN_W - 1)[:, None, None]).astype(np.float32)
    by_col = jnp.einsum("hdj,jkq->hdkq", rpb.astype(_F32), e_col, precision=hi)
    by_col = by_col.reshape(B_HEADS // SLOT, SLOT, 2 * WIN_H - 1, GRID_W, GRID_W)
    tables = []
    for i in (0, 1, rows // q_per - 1):
        s = min(max(i - 1, 0), rows // q_per - N_BAND)
        kr = (q_per * s + np.arange(N_BAND * q_per))[:, None]
        qr = (q_per * i + np.arange(q_per))[None, :]
        d_row = np.clip(kr - qr + WIN_H - 1, 0, 2 * WIN_H - 2)
        rs = np.clip(qr - WIN_H // 2, 0, rows - WIN_H)
        row_in = (kr >= rs) & (kr < rs + WIN_H)
        e_row = (d_row[None] == np.arange(2 * WIN_H - 1)[:, None, None]).astype(np.float32)
        tab = jnp.einsum("dab,cgdkq->cakgbq", e_row, by_col, precision=hi)
        inside = row_in[:, None, None, :, None] & col_in[None, :, None, None, :]
        tab = jnp.where(inside[None], tab * LOG2E, NEG_INF)
        tables.append(tab.reshape(B_HEADS // SLOT, N_BAND, TOK, NCOL))
    tables.append(jnp.full_like(tables[0], NEG_INF))
    return jnp.stack(tables, axis=0)


def _rope_tables(s):
    tok = jnp.arange(s, dtype=jnp.int32)
    rows = (tok // GRID_W).astype(_F32)
    cols = (tok % GRID_W).astype(_F32)
    n_freq = HEAD_DIM // 4
    inv_freq = ROPE_BASE ** (-jnp.arange(n_freq, dtype=_F32) / n_freq)
    ang = jnp.concatenate([rows[:, None] * inv_freq, cols[:, None] * inv_freq], axis=1)
    ang = jnp.concatenate([jnp.zeros((CTX_LEN, 32), _F32), ang], axis=0)
    return jnp.cos(ang).T, jnp.sin(ang).T


def _col(v):
    return jnp.broadcast_to(v[..., None], v.shape + (LANES,))


def _lambda_init(layer):
    return 0.8 - 0.6 * math.exp(-0.3 * layer)


def kernel(x, c, ctx, c_ctx, norm_g, ada_w, ada_b, a_w_in, a_q_g, a_k_g, a_w_out, b_w_in, b_q_g, b_k_g, b_rpb, b_w_out, c_w_in, c_q_g, c_k_g, c_lam_q1, c_lam_k1, c_lam_q2, c_lam_k2, c_subln_g, c_w_out):
    bsz, s, _ = x.shape
    assert c.shape[0] + 1 <= 16
    scale = HEAD_DIM ** -0.5

    cvec = jnp.zeros((16, D_MODEL), _F32).at[:bsz].set(c).at[bsz].set(c_ctx)
    mod = _modulation(cvec, ada_w, ada_b)[:, :bsz + 1]
    sh, sc, gt = jnp.split(mod, 3, axis=-1)
    acol = _col(norm_g[:, None, :] * (1.0 + sc))
    shcol = _col(sh)
    gtcol = _col(gt)

    cos_t, sin_t = _rope_tables(s)
    w_ins, w_outs, q_gs, k_gs = (a_w_in, b_w_in, c_w_in), (a_w_out, b_w_out, c_w_out), \
        (a_q_g, b_q_g, c_q_g), (a_k_g, b_k_g, c_k_g)
    n_kv = (A_KV_HEADS * HEAD_DIM, MIX_WIDTH, MIX_WIDTH)

    ot = zt = xt = wo_t = None
    for i in range(DEPTH):
        kind, j = i % 3, i // 3
        dims = dict(n_q=MIX_WIDTH, n_k=n_kv[kind], n_v=n_kv[kind], rope=kind != 1)
        proj_args = (acol[i], shcol[i], w_ins[kind][j].T.astype(_BF16), cos_t, sin_t,
                     _col(q_gs[kind][j] * (scale * LOG2E)), _col(k_gs[kind][j]))
        if i == 0:
            xt, qt, kt, vt, zt = _first_proj(ctx, x, proj_args, **dims)
        else:
            xt, qt, kt, vt, zt = _out_in_proj(ot, zt, wo_t, gtcol[i - 1], xt, proj_args, **dims)
        if kind == 0:
            ot = _gqa_attention(qt, kt, vt)
        elif kind == 1:
            ot = _nbr_attention(qt, kt, vt, _nbr_bias(b_rpb[j]))
        else:
            lamv = jnp.zeros((8, LANES), _F32).at[0:4, :HEAD_DIM].set(
                jnp.stack([c_lam_q1[j], c_lam_k1[j], c_lam_q2[j], c_lam_k2[j]]).astype(_F32))
            ot = _diff_attention(qt, kt, vt, lamv, _col(c_subln_g[j]), _lambda_init(i))
        wo_t = w_outs[kind][j].T.astype(_BF16)

    return _last_proj(ot, zt, wo_t, gtcol[DEPTH - 1], xt)
```

```python
import functools
import math

import jax
import jax.numpy as jnp
import numpy as np
from jax import lax
from jax.experimental import pallas as pl
from jax.experimental.pallas import tpu as pltpu

D_MODEL = 1024
DEPTH = 4
GRID_W = 64
CTX_LEN = 256
HEAD_DIM = 64
MIX_WIDTH = 1024
A_HEADS = 16
A_KV_HEADS = 4
B_HEADS = 16
C_HEADS = 8
C_V_DIM = 128
WIN_H = 8
WIN_W = 16
ROPE_BASE = 10000.0
EPS = 1e-6
NEG_INF = -1e30
LOG2E = 1.4426950408889634

LANES = 128
TOK = 256
SLOT = 4
VMEM_LIMIT = 56 * 1024 * 1024

_BF16 = jnp.bfloat16
_F32 = jnp.float32


def _cparams(*sem):
    return pltpu.CompilerParams(dimension_semantics=sem, vmem_limit_bytes=VMEM_LIMIT)


def _mod_kernel(c_ref, w_ref, b_ref, o_ref):
    cv = c_ref[...]
    s = cv * jax.nn.sigmoid(cv)
    o_ref[0] = jnp.dot(s, w_ref[0], preferred_element_type=_F32) + b_ref[0]


def _modulation(cvec, ada_w, ada_b):
    n_col = 512
    return pl.pallas_call(
        _mod_kernel,
        grid=(DEPTH, 3 * D_MODEL // n_col),
        in_specs=[
            pl.BlockSpec((16, D_MODEL), lambda l, n: (0, 0)),
            pl.BlockSpec((1, D_MODEL, n_col), lambda l, n: (l, 0, n)),
            pl.BlockSpec((1, 1, n_col), lambda l, n: (l, 0, n)),
        ],
        out_specs=pl.BlockSpec((1, 16, n_col), lambda l, n: (l, 0, n)),
        out_shape=jax.ShapeDtypeStruct((DEPTH, 16, 3 * D_MODEL), _F32),
        compiler_params=_cparams("arbitrary", "arbitrary"),
        name="adaln_mod",
    )(cvec, ada_w, ada_b.reshape(DEPTH, 1, 3 * D_MODEL))


def _lane_tile(col_ref_val):
    return jnp.concatenate([col_ref_val] * (TOK // LANES), axis=1)


def _head_norm_rope(u, gain, cos, sin, rope):
    ms = jnp.sum(u * u, axis=0, keepdims=True) * (1.0 / HEAD_DIM)
    y = u * lax.rsqrt(ms + EPS) * gain
    if not rope:
        return y
    x1r, x2r, x1c, x2c = y[0:16], y[16:32], y[32:48], y[48:64]
    cr, cc = cos[0:16], cos[16:32]
    sr, sc = sin[0:16], sin[16:32]
    return jnp.concatenate(
        [x1r * cr - x2r * sr, x2r * cr + x1r * sr, x1c * cc - x2c * sc, x2c * cc + x1c * sc], axis=0)


def _proj_body(x, a_ref, sh_ref, w_ref, cos_ref, sin_ref, qg_ref, kg_ref,
               q_ref, k_ref, v_ref, g_ref, *, n_q, n_k, n_v, rope):
    ms = jnp.sum(x * x, axis=0, keepdims=True) * (1.0 / D_MODEL)
    h = x * lax.rsqrt(ms + EPS) * _lane_tile(a_ref[0]) + _lane_tile(sh_ref[0])
    hb = h.astype(_BF16)
    cos, sin = cos_ref[...], sin_ref[...]
    qg, kg = _lane_tile(qg_ref[...]), _lane_tile(kg_ref[...])

    uq = jnp.dot(w_ref[0:n_q, :], hb, preferred_element_type=_F32)
    for hd in range(n_q // HEAD_DIM):
        r = slice(hd * HEAD_DIM, (hd + 1) * HEAD_DIM)
        q_ref[0, 0, r, :] = _head_norm_rope(uq[r], qg, cos, sin, rope).astype(_BF16)

    uk = jnp.dot(w_ref[n_q:n_q + n_k, :], hb, preferred_element_type=_F32)
    for hd in range(n_k // HEAD_DIM):
        kh = _head_norm_rope(uk[hd * HEAD_DIM:(hd + 1) * HEAD_DIM], kg, cos, sin, rope)
        kt = jnp.concatenate([kh, jnp.zeros((LANES - HEAD_DIM, TOK), _F32)], axis=0)
        k_ref[0, 0, hd] = kt.T.astype(_BF16)

    o = n_q + n_k
    uv = jnp.dot(w_ref[o:o + n_v, :], hb, preferred_element_type=_F32)
    v_ref[0, 0] = uv.astype(_BF16)

    o = o + n_v
    uz = jnp.dot(w_ref[o:o + MIX_WIDTH, :], hb, preferred_element_type=_F32)
    g_ref[0] = (uz * jax.nn.sigmoid(uz)).astype(_BF16)


def _residual_update(o_ref, g_ref, w_ref, gt_ref, x_ref):
    og = (o_ref[0, 0].astype(_F32) * g_ref[0].astype(_F32)).astype(_BF16)
    y = jnp.dot(w_ref[...], og, preferred_element_type=_F32)
    return x_ref[0] + _lane_tile(gt_ref[0]) * y


def _first_kernel(ctx_ref, x_ref, *refs, **dims):
    xt_ref, proj_out = refs[7], refs[8:]
    x = jnp.where(pl.program_id(1) == 0, ctx_ref[0], x_ref[0]).T
    xt_ref[0] = x
    _proj_body(x, *refs[:7], *proj_out, **dims)


def _mid_kernel(o_ref, g_ref, wo_ref, gt_ref, x_ref, *refs, **dims):
    xt_ref, proj_out = refs[7], refs[8:]
    x = _residual_update(o_ref, g_ref, wo_ref, gt_ref, x_ref)
    xt_ref[0] = x
    _proj_body(x, *refs[:7], *proj_out, **dims)


def _last_kernel(o_ref, g_ref, wo_ref, gt_ref, x_ref, out_ref):
    out_ref[0] = _residual_update(o_ref, g_ref, wo_ref, gt_ref, x_ref).T


def _proj_specs(bsz, nt, n_in, n_q, n_k, n_v):
    col = lambda b, j: (jnp.where(j == 0, bsz, b), 0, 0)
    t = nt * TOK
    n_kh = n_k // HEAD_DIM
    in_specs = [
        pl.BlockSpec((1, D_MODEL, LANES), col),
        pl.BlockSpec((1, D_MODEL, LANES), col),
        pl.BlockSpec((n_in, D_MODEL), lambda b, j: (0, 0)),
        pl.BlockSpec((32, TOK), lambda b, j: (0, j)),
        pl.BlockSpec((32, TOK), lambda b, j: (0, j)),
        pl.BlockSpec((HEAD_DIM, LANES), lambda b, j: (0, 0)),
        pl.BlockSpec((HEAD_DIM, LANES), lambda b, j: (0, 0)),
    ]
    out_specs = [
        pl.BlockSpec((1, D_MODEL, TOK), lambda b, j: (b, 0, j)),
        pl.BlockSpec((1, 1, n_q, TOK), lambda b, j: (b, j, 0, 0)),
        pl.BlockSpec((1, 1, n_kh, TOK, LANES), lambda b, j: (b, j, 0, 0, 0)),
        pl.BlockSpec((1, 1, n_v, TOK), lambda b, j: (b, j, 0, 0)),
        pl.BlockSpec((1, MIX_WIDTH, TOK), lambda b, j: (b, 0, j)),
    ]
    out_shape = [
        jax.ShapeDtypeStruct((bsz, D_MODEL, t), _F32),
        jax.ShapeDtypeStruct((bsz, nt, n_q, TOK), _BF16),
        jax.ShapeDtypeStruct((bsz, nt, n_kh, TOK, LANES), _BF16),
        jax.ShapeDtypeStruct((bsz, nt, n_v, TOK), _BF16),
        jax.ShapeDtypeStruct((bsz, MIX_WIDTH, t), _BF16),
    ]
    return in_specs, out_specs, out_shape


def _first_proj(ctx, x, proj_args, *, n_q, n_k, n_v, rope):
    bsz, s, _ = x.shape
    nt = (ctx.shape[1] + s) // TOK
    in_specs, out_specs, out_shape = _proj_specs(bsz, nt, proj_args[2].shape[0], n_q, n_k, n_v)
    return pl.pallas_call(
        functools.partial(_first_kernel, n_q=n_q, n_k=n_k, n_v=n_v, rope=rope),
        grid=(bsz, nt),
        in_specs=[
            pl.BlockSpec((1, TOK, D_MODEL), lambda b, j: (b, 0, 0)),
            pl.BlockSpec((1, TOK, D_MODEL), lambda b, j: (b, jnp.maximum(j - 1, 0), 0)),
        ] + in_specs,
        out_specs=out_specs,
        out_shape=out_shape,
        compiler_params=_cparams("arbitrary", "arbitrary"),
        name="first_proj",
    )(ctx, x, *proj_args)


def _out_in_proj(ot, gz, wo_t, gtcol, xt, proj_args, *, n_q, n_k, n_v, rope):
    bsz, _, t = xt.shape
    nt = t // TOK
    in_specs, out_specs, out_shape = _proj_specs(bsz, nt, proj_args[2].shape[0], n_q, n_k, n_v)
    col = lambda b, j: (jnp.where(j == 0, bsz, b), 0, 0)
    tile = lambda b, j: (b, 0, j)
    return pl.pallas_call(
        functools.partial(_mid_kernel, n_q=n_q, n_k=n_k, n_v=n_v, rope=rope),
        grid=(bsz, nt),
        in_specs=[
            pl.BlockSpec((1, 1, MIX_WIDTH, TOK), lambda b, j: (b, j, 0, 0)),
            pl.BlockSpec((1, MIX_WIDTH, TOK), tile),
            pl.BlockSpec((D_MODEL, MIX_WIDTH), lambda b, j: (0, 0)),
            pl.BlockSpec((1, D_MODEL, LANES), col),
            pl.BlockSpec((1, D_MODEL, TOK), tile),
        ] + in_specs,
        out_specs=out_specs,
        out_shape=out_shape,
        input_output_aliases={4: 0},
        compiler_params=_cparams("arbitrary", "arbitrary"),
        name="out_in_proj",
    )(ot, gz, wo_t, gtcol, xt, *proj_args)


def _last_proj(ot, gz, wo_t, gtcol, xt):
    bsz, _, t = xt.shape
    n_lat = t // TOK - 1
    tile = lambda b, j: (b, 0, j + 1)
    return pl.pallas_call(
        _last_kernel,
        grid=(bsz, n_lat),
        in_specs=[
            pl.BlockSpec((1, 1, MIX_WIDTH, TOK), lambda b, j: (b, j + 1, 0, 0)),
            pl.BlockSpec((1, MIX_WIDTH, TOK), tile),
            pl.BlockSpec((D_MODEL, MIX_WIDTH), lambda b, j: (0, 0)),
            pl.BlockSpec((1, D_MODEL, LANES), lambda b, j: (b, 0, 0)),
            pl.BlockSpec((1, D_MODEL, TOK), tile),
        ],
        out_specs=pl.BlockSpec((1, TOK, D_MODEL), lambda b, j: (b, j, 0)),
        out_shape=jax.ShapeDtypeStruct((bsz, n_lat * TOK, D_MODEL), _F32),
        compiler_params=_cparams("arbitrary", "arbitrary"),
        name="last_proj",
    )(ot, gz, wo_t, gtcol, xt)


def _pad_query(qh):
    return jnp.concatenate([qh, jnp.zeros_like(qh)], axis=0)


NCOL = SLOT * TOK
ONES_ROWS = 16


def _score_stage(k_tile, qcat_ref, s_ref, mx_ref, sb, rows=TOK, bias=None):
    if isinstance(k_tile, (list, tuple)):
        s = jnp.concatenate([jnp.dot(k, qcat_ref[:, j * TOK:(j + 1) * TOK], preferred_element_type=_F32)
                             for j, k in enumerate(k_tile)], axis=1)
    else:
        s = jnp.dot(k_tile, qcat_ref[...], preferred_element_type=_F32)
    if bias is not None:
        s = s + bias
    s_ref[sb, 0:rows] = s
    mx_ref[sb] = jnp.max(s, axis=0, keepdims=True)


def _softmax_value_stage(v_tile, s_ref, mx_ref, m_ref, acc_ref, sb, rows=TOK):
    m_old = m_ref[...]
    m_new = jnp.maximum(m_old, mx_ref[sb])
    alpha = jnp.exp2(m_old - m_new)
    m_ref[...] = m_new
    for j in range(SLOT):
        cols = slice(j * TOK, (j + 1) * TOK)
        v = v_tile(j)
        v_aug = jnp.concatenate([v, jnp.ones((ONES_ROWS, v.shape[1]), _BF16)], axis=0)
        p = jnp.exp2(s_ref[sb, 0:rows, cols] - m_new[:, cols]).astype(_BF16)
        acc_ref[:, cols] = (alpha[:, cols] * acc_ref[:, cols]
                            + jnp.dot(v_aug, p, preferred_element_type=_F32))


LAST_BUF = 2
GQA_KEY_GROUP = 1
DIFF_KEY_GROUP = 2


def _attend_tiles(n_t, build_qcat, k_tile, v_tile, finalize, qcat_ref, s_ref, mx_ref, m_ref, acc_ref):
    key_group = s_ref.shape[1] // TOK
    groups = [(0, 1)] + [(t, key_group) for t in range(1, n_t, key_group)]
    assert (n_t - 1) % key_group == 0
    n_g = len(groups)
    buf = lambda g: LAST_BUF if g == n_g - 1 else g % 2

    def score(g):
        t, n = groups[g]
        _score_stage(k_tile(t, n), qcat_ref, s_ref, mx_ref, buf(g), rows=n * TOK)

    def stage(g):
        t, n = groups[g]
        _softmax_value_stage(functools.partial(v_tile, t, n), s_ref, mx_ref, m_ref, acc_ref, buf(g), rows=n * TOK)

    def reset():
        m_ref[...] = jnp.full(m_ref.shape, NEG_INF, _F32)
        acc_ref[...] = jnp.zeros(acc_ref.shape, _F32)

    def start(j):
        build_qcat(j)
        score(0)

    reset()
    start(0)
    stage(0)
    finalize(0)
    reset()
    start(1)

    def tile(j, carry):
        for g in range(n_g - 1):
            score(g + 1)
            stage(g)
        stage(n_g - 1)
        finalize(j)
        reset()
        start(jnp.minimum(j + 1, n_t - 1))
        return carry

    lax.fori_loop(1, n_t, tile, 0)


def _attend_scratch(dv, key_group):
    return [
        pltpu.VMEM((LANES, NCOL), _BF16),
        pltpu.VMEM((3, key_group * TOK, NCOL), _F32),
        pltpu.VMEM((3, 1, NCOL), _F32),
        pltpu.VMEM((1, NCOL), _F32),
        pltpu.VMEM((dv + ONES_ROWS, NCOL), _F32),
    ]


def _gqa_kernel(q_ref, k_ref, v_ref, o_ref, qcat_ref, s_ref, mx_ref, m_ref, acc_ref):
    def build_qcat(j):
        for g in range(SLOT):
            qcat_ref[:, g * TOK:(g + 1) * TOK] = _pad_query(q_ref[0, j, g * HEAD_DIM:(g + 1) * HEAD_DIM, :])

    def finalize(j):
        o = acc_ref[0:HEAD_DIM, :] / acc_ref[HEAD_DIM:HEAD_DIM + 1, :]
        for g in range(SLOT):
            o_ref[0, j, g * HEAD_DIM:(g + 1) * HEAD_DIM, :] = o[:, g * TOK:(g + 1) * TOK].astype(_BF16)

    _attend_tiles(k_ref.shape[1], build_qcat,
                  lambda t, n: k_ref[0, t:t + n, 0].reshape(n * TOK, LANES),
                  lambda t, n, g: jnp.concatenate([v_ref[0, t + i] for i in range(n)], axis=1),
                  finalize, qcat_ref, s_ref, mx_ref, m_ref, acc_ref)


def _gqa_attention(qt, kt, vt):
    bsz, nt = qt.shape[:2]
    grp = SLOT * HEAD_DIM
    return pl.pallas_call(
        _gqa_kernel,
        grid=(bsz, A_KV_HEADS),
        in_specs=[
            pl.BlockSpec((1, nt, grp, TOK), lambda b, h: (b, 0, h, 0)),
            pl.BlockSpec((1, nt, 1, TOK, LANES), lambda b, h: (b, 0, h, 0, 0)),
            pl.BlockSpec((1, nt, HEAD_DIM, TOK), lambda b, h: (b, 0, h, 0)),
        ],
        out_specs=pl.BlockSpec((1, nt, grp, TOK), lambda b, h: (b, 0, h, 0)),
        out_shape=jax.ShapeDtypeStruct((bsz, nt, MIX_WIDTH, TOK), _BF16),
        scratch_shapes=_attend_scratch(HEAD_DIM, GQA_KEY_GROUP),
        compiler_params=_cparams("arbitrary", "arbitrary"),
        name="gqa_attn",
    )(qt, kt, vt)


def _diff_kernel(q_ref, k_ref, v_ref, lam_ref, sg_ref, o_ref, qcat_ref, s_ref, mx_ref, m_ref, acc_ref,
                 *, out_scale, lam_init):
    def build_qcat(j):
        for s in range(SLOT):
            qcat_ref[:, s * TOK:(s + 1) * TOK] = _pad_query(q_ref[0, j, s * HEAD_DIM:(s + 1) * HEAD_DIM, :])

    def finalize(j):
        lv = lam_ref[...]
        lam = (jnp.exp(jnp.sum(lv[0:1] * lv[1:2], axis=1, keepdims=True))
               - jnp.exp(jnp.sum(lv[2:3] * lv[3:4], axis=1, keepdims=True)) + lam_init)
        sg = _lane_tile(sg_ref[...])
        attn = acc_ref[0:C_V_DIM, :] / acc_ref[C_V_DIM:C_V_DIM + 1, :]
        for hh in range(2):
            o = attn[:, 2 * hh * TOK:(2 * hh + 1) * TOK] - lam * attn[:, (2 * hh + 1) * TOK:(2 * hh + 2) * TOK]
            ms = jnp.sum(o * o, axis=0, keepdims=True) * (1.0 / C_V_DIM)
            y = o * lax.rsqrt(ms + EPS) * sg * out_scale
            o_ref[0, j, hh * C_V_DIM:(hh + 1) * C_V_DIM, :] = y.astype(_BF16)

    head_rows = lambda s: slice((s // 2) * C_V_DIM, (s // 2 + 1) * C_V_DIM)
    _attend_tiles(k_ref.shape[1], build_qcat,
                  lambda t, n: [k_ref[0, t:t + n, s].reshape(n * TOK, LANES) for s in range(SLOT)],
                  lambda t, n, s: jnp.concatenate([v_ref[0, t + i, head_rows(s), :] for i in range(n)], axis=1),
                  finalize, qcat_ref, s_ref, mx_ref, m_ref, acc_ref)


def _diff_attention(qt, kt, vt, lamv, sg, lam_init):
    bsz, nt = qt.shape[:2]
    n_blk = kt.shape[2] // SLOT
    grp = SLOT * HEAD_DIM
    return pl.pallas_call(
        functools.partial(_diff_kernel, out_scale=1.0 - lam_init, lam_init=lam_init),
        grid=(bsz, n_blk),
        in_specs=[
            pl.BlockSpec((1, nt, grp, TOK), lambda b, c: (b, 0, c, 0)),
            pl.BlockSpec((1, nt, SLOT, TOK, LANES), lambda b, c: (b, 0, c, 0, 0)),
            pl.BlockSpec((1, nt, 2 * C_V_DIM, TOK), lambda b, c: (b, 0, c, 0)),
            pl.BlockSpec((8, LANES), lambda b, c: (0, 0)),
            pl.BlockSpec((C_V_DIM, LANES), lambda b, c: (0, 0)),
        ],
        out_specs=pl.BlockSpec((1, nt, 2 * C_V_DIM, TOK), lambda b, c: (b, 0, c, 0)),
        out_shape=jax.ShapeDtypeStruct((bsz, nt, MIX_WIDTH, TOK), _BF16),
        scratch_shapes=_attend_scratch(C_V_DIM, DIFF_KEY_GROUP),
        compiler_params=_cparams("arbitrary", "arbitrary"),
        name="diff_attn",
    )(qt, kt, vt, lamv, sg)


N_BAND = 3
NBR_QUADS_PER_TRIP = 4


def _nbr_kernel(q_ref, kc_ref, k0_ref, k1_ref, k2_ref, vc_ref, v0_ref, v1_ref, v2_ref, bias_ref, o_ref,
                qcat_ref, s_ref, mx_ref, m_ref, acc_ref):
    k_refs = (kc_ref, k0_ref, k1_ref, k2_ref)
    v_refs = (vc_ref, v0_ref, v1_ref, v2_ref)
    n_tiles = len(k_refs)
    n_quad = B_HEADS // SLOT
    n_slot = NBR_QUADS_PER_TRIP * n_tiles

    def trip(u, carry):
        quad = lambda i: NBR_QUADS_PER_TRIP * u + i // n_tiles
        par = lambda i: (i // n_tiles) % 2

        def head(i, g):
            return pl.ds(pl.multiple_of((quad(i) * SLOT + g) * HEAD_DIM, HEAD_DIM), HEAD_DIM)

        def score(i):
            c, t = quad(i), i % n_tiles
            if t == 0:
                for g in range(SLOT):
                    qcat_ref[par(i), :, g * TOK:(g + 1) * TOK] = _pad_query(q_ref[0, 0, head(i, g), :])
                m_ref[par(i)] = jnp.full(m_ref.shape[1:], NEG_INF, _F32)
                acc_ref[par(i)] = jnp.zeros(acc_ref.shape[1:], _F32)
            bias = None if t == 0 else bias_ref[0, c, t - 1]
            keys = [k_refs[t][0, 0, c * SLOT + g] for g in range(SLOT)]
            _score_stage(keys, qcat_ref.at[par(i)], s_ref, mx_ref, i % 2, bias=bias)

        score(0)
        for i in range(n_slot):
            t = i % n_tiles
            if i + 1 < n_slot:
                score(i + 1)
            acc = acc_ref.at[par(i)]
            _softmax_value_stage(lambda g, i=i, t=t: v_refs[t][0, 0, head(i, g), :],
                                 s_ref, mx_ref, m_ref.at[par(i)], acc, i % 2)
            if t == n_tiles - 1:
                o = acc[0:HEAD_DIM, :] / acc[HEAD_DIM:HEAD_DIM + 1, :]
                for g in range(SLOT):
                    o_ref[0, 0, head(i, g), :] = o[:, g * TOK:(g + 1) * TOK].astype(_BF16)
        return carry

    lax.fori_loop(0, n_quad // NBR_QUADS_PER_TRIP, trip, 0)


def _nbr_attention(qt, kt, vt, bias):
    bsz, nt = qt.shape[:2]
    n_lat = nt - 1
    tile = lambda j, b: (b, j, 0, 0)

    def band(d):
        return lambda j, b: (b, 1 + jnp.clip(j - 2, 0, n_lat - N_BAND) + d, 0, 0, 0)

    def vband(d):
        return lambda j, b: (b, 1 + jnp.clip(j - 2, 0, n_lat - N_BAND) + d, 0, 0)

    case = lambda j, b: (jnp.where(j == 0, 3, jnp.where(j == 1, 0, jnp.where(j == nt - 1, 2, 1))), 0, 0, 0, 0)
    kspec = lambda im: pl.BlockSpec((1, 1, B_HEADS, TOK, LANES), im)
    vspec = lambda im: pl.BlockSpec((1, 1, MIX_WIDTH, TOK), im)
    return pl.pallas_call(
        _nbr_kernel,
        grid=(nt, bsz),
        in_specs=[
            pl.BlockSpec((1, 1, MIX_WIDTH, TOK), tile),
            kspec(lambda j, b: (b, 0, 0, 0, 0)), kspec(band(0)), kspec(band(1)), kspec(band(2)),
            vspec(lambda j, b: (b, 0, 0, 0)), vspec(vband(0)), vspec(vband(1)), vspec(vband(2)),
            pl.BlockSpec((1, B_HEADS // SLOT, N_BAND, TOK, NCOL), case),
        ],
        out_specs=pl.BlockSpec((1, 1, MIX_WIDTH, TOK), tile),
        out_shape=jax.ShapeDtypeStruct((bsz, nt, MIX_WIDTH, TOK), _BF16),
        scratch_shapes=[
            pltpu.VMEM((2, LANES, NCOL), _BF16),
            pltpu.VMEM((2, TOK, NCOL), _F32),
            pltpu.VMEM((2, 1, NCOL), _F32),
            pltpu.VMEM((2, 1, NCOL), _F32),
            pltpu.VMEM((2, HEAD_DIM + ONES_ROWS, NCOL), _F32),
        ],
        compiler_params=_cparams("arbitrary", "arbitrary"),
        name="nbr_attn",
    )(qt, kt, kt, kt, kt, vt, vt, vt, vt, bias)


def _nbr_bias(rpb):
    rows = 4096 // GRID_W
    q_per = TOK // GRID_W
    hi = lax.Precision.HIGHEST
    kc = np.arange(GRID_W)[:, None]
    qc = np.arange(GRID_W)[None, :]
    d_col = np.clip(kc - qc + WIN_W - 1, 0, 2 * WIN_W - 2)
    cs = np.clip(qc - WIN_W // 2, 0, GRID_W - WIN_W)
    col_in = (kc >= cs) & (kc < cs + WIN_W)
    e_col = (d_col[None] == np.arange(2 * WIN_W - 1)[:, None, None]).astype(np.float32)
    by_col = jnp.einsum("hdj,jkq->hdkq", rpb.astype(_F32), e_col, precision=hi)
    by_col = by_col.reshape(B_HEADS // SLOT, SLOT, 2 * WIN_H - 1, GRID_W, GRID_W)
    tables = []
    for i in (0, 1, rows // q_per - 1):
        s = min(max(i - 1, 0), rows // q_per - N_BAND)
        kr = (q_per * s + np.arange(N_BAND * q_per))[:, None]
        qr = (q_per * i + np.arange(q_per))[None, :]
        d_row = np.clip(kr - qr + WIN_H - 1, 0, 2 * WIN_H - 2)
        rs = np.clip(qr - WIN_H // 2, 0, rows - WIN_H)
        row_in = (kr >= rs) & (kr < rs + WIN_H)
        e_row = (d_row[None] == np.arange(2 * WIN_H - 1)[:, None, None]).astype(np.float32)
        tab = jnp.einsum("dab,cgdkq->cakgbq", e_row, by_col, precision=hi)
        inside = row_in[:, None, None, :, None] & col_in[None, :, None, None, :]
        tab = jnp.where(inside[None], tab * LOG2E, NEG_INF)
        tables.append(tab.reshape(B_HEADS // SLOT, N_BAND, TOK, NCOL))
    tables.append(jnp.full_like(tables[0], NEG_INF))
    return jnp.stack(tables, axis=0)


def _rope_tables(s):
    tok = jnp.arange(s, dtype=jnp.int32)
    rows = (tok // GRID_W).astype(_F32)
    cols = (tok % GRID_W).astype(_F32)
    n_freq = HEAD_DIM // 4
    inv_freq = ROPE_BASE ** (-jnp.arange(n_freq, dtype=_F32) / n_freq)
    ang = jnp.concatenate([rows[:, None] * inv_freq, cols[:, None] * inv_freq], axis=1)
    ang = jnp.concatenate([jnp.zeros((CTX_LEN, 32), _F32), ang], axis=0)
    return jnp.cos(ang).T, jnp.sin(ang).T


def _col(v):
    return jnp.broadcast_to(v[..., None], v.shape + (LANES,))


def _lambda_init(layer):
    return 0.8 - 0.6 * math.exp(-0.3 * layer)


def kernel(x, c, ctx, c_ctx, norm_g, ada_w, ada_b, a_w_in, a_q_g, a_k_g, a_w_out, b_w_in, b_q_g, b_k_g, b_rpb, b_w_out, c_w_in, c_q_g, c_k_g, c_lam_q1, c_lam_k1, c_lam_q2, c_lam_k2, c_subln_g, c_w_out):
    bsz, s, _ = x.shape
    assert c.shape[0] + 1 <= 16
    scale = HEAD_DIM ** -0.5

    cvec = jnp.zeros((16, D_MODEL), _F32).at[:bsz].set(c).at[bsz].set(c_ctx)
    mod = _modulation(cvec, ada_w, ada_b)[:, :bsz + 1]
    sh, sc, gt = jnp.split(mod, 3, axis=-1)
    acol = _col(norm_g[:, None, :] * (1.0 + sc))
    shcol = _col(sh)
    gtcol = _col(gt)

    cos_t, sin_t = _rope_tables(s)
    w_ins, w_outs, q_gs, k_gs = (a_w_in, b_w_in, c_w_in), (a_w_out, b_w_out, c_w_out), \
        (a_q_g, b_q_g, c_q_g), (a_k_g, b_k_g, c_k_g)
    n_kv = (A_KV_HEADS * HEAD_DIM, MIX_WIDTH, MIX_WIDTH)

    ot = zt = xt = wo_t = None
    for i in range(DEPTH):
        kind, j = i % 3, i // 3
        dims = dict(n_q=MIX_WIDTH, n_k=n_kv[kind], n_v=n_kv[kind], rope=kind != 1)
        proj_args = (acol[i], shcol[i], w_ins[kind][j].T.astype(_BF16), cos_t, sin_t,
                     _col(q_gs[kind][j] * (scale * LOG2E)), _col(k_gs[kind][j]))
        if i == 0:
            xt, qt, kt, vt, zt = _first_proj(ctx, x, proj_args, **dims)
        else:
            xt, qt, kt, vt, zt = _out_in_proj(ot, zt, wo_t, gtcol[i - 1], xt, proj_args, **dims)
        if kind == 0:
            ot = _gqa_attention(qt, kt, vt)
        elif kind == 1:
            ot = _nbr_attention(qt, kt, vt, _nbr_bias(b_rpb[j]))
        else:
            lamv = jnp.zeros((8, LANES), _F32).at[0:4, :HEAD_DIM].set(
                jnp.stack([c_lam_q1[j], c_lam_k1[j], c_lam_q2[j], c_lam_k2[j]]).astype(_F32))
            ot = _diff_attention(qt, kt, vt, lamv, _col(c_subln_g[j]), _lambda_init(i))
        wo_t = w_outs[kind][j].T.astype(_BF16)

    return _last_proj(ot, zt, wo_t, gtcol[DEPTH - 1], xt)
```

```python
import functools
import math

import jax
import jax.numpy as jnp
import numpy as np
from jax import lax
from jax.experimental import pallas as pl
from jax.experimental.pallas import tpu as pltpu

D_MODEL = 1024
DEPTH = 4
GRID_W = 64
CTX_LEN = 256
HEAD_DIM = 64
MIX_WIDTH = 1024
A_HEADS = 16
A_KV_HEADS = 4
B_HEADS = 16
C_HEADS = 8
C_V_DIM = 128
WIN_H = 8
WIN_W = 16
ROPE_BASE = 10000.0
EPS = 1e-6
NEG_INF = -1e30
LOG2E = 1.4426950408889634

LANES = 128
TOK = 256
SLOT = 4
VMEM_LIMIT = 56 * 1024 * 1024

_BF16 = jnp.bfloat16
_F32 = jnp.float32


def _cparams(*sem):
    return pltpu.CompilerParams(dimension_semantics=sem, vmem_limit_bytes=VMEM_LIMIT)


def _mod_kernel(c_ref, w_ref, b_ref, o_ref):
    cv = c_ref[...]
    s = cv * jax.nn.sigmoid(cv)
    o_ref[0] = jnp.dot(s, w_ref[0], preferred_element_type=_F32) + b_ref[0]


def _modulation(cvec, ada_w, ada_b):
    n_col = 512
    return pl.pallas_call(
        _mod_kernel,
        grid=(DEPTH, 3 * D_MODEL // n_col),
        in_specs=[
            pl.BlockSpec((16, D_MODEL), lambda l, n: (0, 0)),
            pl.BlockSpec((1, D_MODEL, n_col), lambda l, n: (l, 0, n)),
            pl.BlockSpec((1, 1, n_col), lambda l, n: (l, 0, n)),
        ],
        out_specs=pl.BlockSpec((1, 16, n_col), lambda l, n: (l, 0, n)),
        out_shape=jax.ShapeDtypeStruct((DEPTH, 16, 3 * D_MODEL), _F32),
        compiler_params=_cparams("arbitrary", "arbitrary"),
        name="adaln_mod",
    )(cvec, ada_w, ada_b.reshape(DEPTH, 1, 3 * D_MODEL))


def _lane_tile(col_ref_val):
    return jnp.concatenate([col_ref_val] * (TOK // LANES), axis=1)


def _head_norm_rope(u, gain, cos, sin, rope):
    ms = jnp.sum(u * u, axis=0, keepdims=True) * (1.0 / HEAD_DIM)
    y = u * lax.rsqrt(ms + EPS) * gain
    if not rope:
        return y
    x1r, x2r, x1c, x2c = y[0:16], y[16:32], y[32:48], y[48:64]
    cr, cc = cos[0:16], cos[16:32]
    sr, sc = sin[0:16], sin[16:32]
    return jnp.concatenate(
        [x1r * cr - x2r * sr, x2r * cr + x1r * sr, x1c * cc - x2c * sc, x2c * cc + x1c * sc], axis=0)


def _proj_body(x, a_ref, sh_ref, w_ref, cos_ref, sin_ref, qg_ref, kg_ref,
               q_ref, k_ref, v_ref, g_ref, *, n_q, n_k, n_v, rope):
    ms = jnp.sum(x * x, axis=0, keepdims=True) * (1.0 / D_MODEL)
    h = x * lax.rsqrt(ms + EPS) * _lane_tile(a_ref[0]) + _lane_tile(sh_ref[0])
    hb = h.astype(_BF16)
    cos, sin = cos_ref[...], sin_ref[...]
    qg, kg = _lane_tile(qg_ref[...]), _lane_tile(kg_ref[...])

    uq = jnp.dot(w_ref[0:n_q, :], hb, preferred_element_type=_F32)
    for hd in range(n_q // HEAD_DIM):
        r = slice(hd * HEAD_DIM, (hd + 1) * HEAD_DIM)
        q_ref[0, 0, r, :] = _head_norm_rope(uq[r], qg, cos, sin, rope).astype(_BF16)

    uk = jnp.dot(w_ref[n_q:n_q + n_k, :], hb, preferred_element_type=_F32)
    for hd in range(n_k // HEAD_DIM):
        kh = _head_norm_rope(uk[hd * HEAD_DIM:(hd + 1) * HEAD_DIM], kg, cos, sin, rope)
        kt = jnp.concatenate([kh, jnp.zeros((LANES - HEAD_DIM, TOK), _F32)], axis=0)
        k_ref[0, 0, hd] = kt.T.astype(_BF16)

    o = n_q + n_k
    uv = jnp.dot(w_ref[o:o + n_v, :], hb, preferred_element_type=_F32)
    v_ref[0, 0] = uv.astype(_BF16)

    o = o + n_v
    uz = jnp.dot(w_ref[o:o + MIX_WIDTH, :], hb, preferred_element_type=_F32)
    g_ref[0] = (uz * jax.nn.sigmoid(uz)).astype(_BF16)


def _residual_update(o_ref, g_ref, w_ref, gt_ref, x_ref):
    og = (o_ref[0, 0].astype(_F32) * g_ref[0].astype(_F32)).astype(_BF16)
    y = jnp.dot(w_ref[...], og, preferred_element_type=_F32)
    return x_ref[0] + _lane_tile(gt_ref[0]) * y


def _first_kernel(ctx_ref, x_ref, *refs, **dims):
    xt_ref, proj_out = refs[7], refs[8:]
    x = jnp.where(pl.program_id(1) == 0, ctx_ref[0], x_ref[0]).T
    xt_ref[0] = x
    _proj_body(x, *refs[:7], *proj_out, **dims)


def _mid_kernel(o_ref, g_ref, wo_ref, gt_ref, x_ref, *refs, **dims):
    xt_ref, proj_out = refs[7], refs[8:]
    x = _residual_update(o_ref, g_ref, wo_ref, gt_ref, x_ref)
    xt_ref[0] = x
    _proj_body(x, *refs[:7], *proj_out, **dims)


def _last_kernel(o_ref, g_ref, wo_ref, gt_ref, x_ref, out_ref):
    out_ref[0] = _residual_update(o_ref, g_ref, wo_ref, gt_ref, x_ref).T


def _proj_specs(bsz, nt, n_in, n_q, n_k, n_v):
    col = lambda b, j: (jnp.where(j == 0, bsz, b), 0, 0)
    t = nt * TOK
    n_kh = n_k // HEAD_DIM
    in_specs = [
        pl.BlockSpec((1, D_MODEL, LANES), col),
        pl.BlockSpec((1, D_MODEL, LANES), col),
        pl.BlockSpec((n_in, D_MODEL), lambda b, j: (0, 0)),
        pl.BlockSpec((32, TOK), lambda b, j: (0, j)),
        pl.BlockSpec((32, TOK), lambda b, j: (0, j)),
        pl.BlockSpec((HEAD_DIM, LANES), lambda b, j: (0, 0)),
        pl.BlockSpec((HEAD_DIM, LANES), lambda b, j: (0, 0)),
    ]
    out_specs = [
        pl.BlockSpec((1, D_MODEL, TOK), lambda b, j: (b, 0, j)),
        pl.BlockSpec((1, 1, n_q, TOK), lambda b, j: (b, j, 0, 0)),
        pl.BlockSpec((1, 1, n_kh, TOK, LANES), lambda b, j: (b, j, 0, 0, 0)),
        pl.BlockSpec((1, 1, n_v, TOK), lambda b, j: (b, j, 0, 0)),
        pl.BlockSpec((1, MIX_WIDTH, TOK), lambda b, j: (b, 0, j)),
    ]
    out_shape = [
        jax.ShapeDtypeStruct((bsz, D_MODEL, t), _F32),
        jax.ShapeDtypeStruct((bsz, nt, n_q, TOK), _BF16),
        jax.ShapeDtypeStruct((bsz, nt, n_kh, TOK, LANES), _BF16),
        jax.ShapeDtypeStruct((bsz, nt, n_v, TOK), _BF16),
        jax.ShapeDtypeStruct((bsz, MIX_WIDTH, t), _BF16),
    ]
    return in_specs, out_specs, out_shape


def _first_proj(ctx, x, proj_args, *, n_q, n_k, n_v, rope):
    bsz, s, _ = x.shape
    nt = (ctx.shape[1] + s) // TOK
    in_specs, out_specs, out_shape = _proj_specs(bsz, nt, proj_args[2].shape[0], n_q, n_k, n_v)
    return pl.pallas_call(
        functools.partial(_first_kernel, n_q=n_q, n_k=n_k, n_v=n_v, rope=rope),
        grid=(bsz, nt),
        in_specs=[
            pl.BlockSpec((1, TOK, D_MODEL), lambda b, j: (b, 0, 0)),
            pl.BlockSpec((1, TOK, D_MODEL), lambda b, j: (b, jnp.maximum(j - 1, 0), 0)),
        ] + in_specs,
        out_specs=out_specs,
        out_shape=out_shape,
        compiler_params=_cparams("arbitrary", "arbitrary"),
        name="first_proj",
    )(ctx, x, *proj_args)


def _out_in_proj(ot, gz, wo_t, gtcol, xt, proj_args, *, n_q, n_k, n_v, rope):
    bsz, _, t = xt.shape
    nt = t // TOK
    in_specs, out_specs, out_shape = _proj_specs(bsz, nt, proj_args[2].shape[0], n_q, n_k, n_v)
    col = lambda b, j: (jnp.where(j == 0, bsz, b), 0, 0)
    tile = lambda b, j: (b, 0, j)
    return pl.pallas_call(
        functools.partial(_mid_kernel, n_q=n_q, n_k=n_k, n_v=n_v, rope=rope),
        grid=(bsz, nt),
        in_specs=[
            pl.BlockSpec((1, 1, MIX_WIDTH, TOK), lambda b, j: (b, j, 0, 0)),
            pl.BlockSpec((1, MIX_WIDTH, TOK), tile),
            pl.BlockSpec((D_MODEL, MIX_WIDTH), lambda b, j: (0, 0)),
            pl.BlockSpec((1, D_MODEL, LANES), col),
            pl.BlockSpec((1, D_MODEL, TOK), tile),
        ] + in_specs,
        out_specs=out_specs,
        out_shape=out_shape,
        input_output_aliases={4: 0},
        compiler_params=_cparams("arbitrary", "arbitrary"),
        name="out_in_proj",
    )(ot, gz, wo_t, gtcol, xt, *proj_args)


def _last_proj(ot, gz, wo_t, gtcol, xt):
    bsz, _, t = xt.shape
    n_lat = t // TOK - 1
    tile = lambda b, j: (b, 0, j + 1)
    return pl.pallas_call(
        _last_kernel,
        grid=(bsz, n_lat),
        in_specs=[
            pl.BlockSpec((1, 1, MIX_WIDTH, TOK), lambda b, j: (b, j + 1, 0, 0)),
            pl.BlockSpec((1, MIX_WIDTH, TOK), tile),
            pl.BlockSpec((D_MODEL, MIX_WIDTH), lambda b, j: (0, 0)),
            pl.BlockSpec((1, D_MODEL, LANES), lambda b, j: (b, 0, 0)),
            pl.BlockSpec((1, D_MODEL, TOK), tile),
        ],
        out_specs=pl.BlockSpec((1, TOK, D_MODEL), lambda b, j: (b, j, 0)),
        out_shape=jax.ShapeDtypeStruct((bsz, n_lat * TOK, D_MODEL), _F32),
        compiler_params=_cparams("arbitrary", "arbitrary"),
        name="last_proj",
    )(ot, gz, wo_t, gtcol, xt)


def _pad_query(qh):
    return jnp.concatenate([qh, jnp.zeros_like(qh)], axis=0)


NCOL = SLOT * TOK
ONES_ROWS = 16


def _score_stage(k_tile, qcat_ref, s_ref, mx_ref, sb, rows=TOK, bias=None):
    if isinstance(k_tile, (list, tuple)):
        s = jnp.concatenate([jnp.dot(k[:, 0:HEAD_DIM], qcat_ref[0:HEAD_DIM, j * TOK:(j + 1) * TOK],
                                     preferred_element_type=_F32)
                             for j, k in enumerate(k_tile)], axis=1)
    else:
        s = jnp.dot(k_tile[:, 0:HEAD_DIM], qcat_ref[0:HEAD_DIM, :], preferred_element_type=_F32)
    if bias is not None:
        s = s + bias
    s_ref[sb, 0:rows] = s
    mx_ref[sb] = jnp.max(s, axis=0, keepdims=True)


def _softmax_value_stage(v_tile, s_ref, mx_ref, m_ref, acc_ref, sb, rows=TOK):
    m_old = m_ref[...]
    m_new = jnp.maximum(m_old, mx_ref[sb])
    alpha = jnp.exp2(m_old - m_new)
    m_ref[...] = m_new
    for j in range(SLOT):
        cols = slice(j * TOK, (j + 1) * TOK)
        v = v_tile(j)
        v_aug = jnp.concatenate([v, jnp.ones((ONES_ROWS, v.shape[1]), _BF16)], axis=0)
        p = jnp.exp2(s_ref[sb, 0:rows, cols] - m_new[:, cols]).astype(_BF16)
        acc_ref[:, cols] = (alpha[:, cols] * acc_ref[:, cols]
                            + jnp.dot(v_aug, p, preferred_element_type=_F32))


LAST_BUF = 2
GQA_KEY_GROUP = 1
DIFF_KEY_GROUP = 2


def _attend_tiles(n_t, build_qcat, k_tile, v_tile, finalize, qcat_ref, s_ref, mx_ref, m_ref, acc_ref):
    key_group = s_ref.shape[1] // TOK
    groups = [(0, 1)] + [(t, key_group) for t in range(1, n_t, key_group)]
    assert (n_t - 1) % key_group == 0
    n_g = len(groups)
    buf = lambda g: LAST_BUF if g == n_g - 1 else g % 2

    def score(g):
        t, n = groups[g]
        _score_stage(k_tile(t, n), qcat_ref, s_ref, mx_ref, buf(g), rows=n * TOK)

    def stage(g):
        t, n = groups[g]
        _softmax_value_stage(functools.partial(v_tile, t, n), s_ref, mx_ref, m_ref, acc_ref, buf(g), rows=n * TOK)

    def reset():
        m_ref[...] = jnp.full(m_ref.shape, NEG_INF, _F32)
        acc_ref[...] = jnp.zeros(acc_ref.shape, _F32)

    def start(j):
        build_qcat(j)
        score(0)

    reset()
    start(0)
    stage(0)
    finalize(0)
    reset()
    start(1)

    def tile(j, carry):
        for g in range(n_g - 1):
            score(g + 1)
            stage(g)
        stage(n_g - 1)
        finalize(j)
        reset()
        start(jnp.minimum(j + 1, n_t - 1))
        return carry

    lax.fori_loop(1, n_t, tile, 0)


def _attend_scratch(dv, key_group):
    return [
        pltpu.VMEM((LANES, NCOL), _BF16),
        pltpu.VMEM((3, key_group * TOK, NCOL), _F32),
        pltpu.VMEM((3, 1, NCOL), _F32),
        pltpu.VMEM((1, NCOL), _F32),
        pltpu.VMEM((dv + ONES_ROWS, NCOL), _F32),
    ]


def _gqa_kernel(q_ref, k_ref, v_ref, o_ref, qcat_ref, s_ref, mx_ref, m_ref, acc_ref):
    def build_qcat(j):
        for g in range(SLOT):
            qcat_ref[:, g * TOK:(g + 1) * TOK] = _pad_query(q_ref[0, j, g * HEAD_DIM:(g + 1) * HEAD_DIM, :])

    def finalize(j):
        o = acc_ref[0:HEAD_DIM, :] / acc_ref[HEAD_DIM:HEAD_DIM + 1, :]
        for g in range(SLOT):
            o_ref[0, j, g * HEAD_DIM:(g + 1) * HEAD_DIM, :] = o[:, g * TOK:(g + 1) * TOK].astype(_BF16)

    _attend_tiles(k_ref.shape[1], build_qcat,
                  lambda t, n: k_ref[0, t:t + n, 0].reshape(n * TOK, LANES),
                  lambda t, n, g: jnp.concatenate([v_ref[0, t + i] for i in range(n)], axis=1),
                  finalize, qcat_ref, s_ref, mx_ref, m_ref, acc_ref)


def _gqa_attention(qt, kt, vt):
    bsz, nt = qt.shape[:2]
    grp = SLOT * HEAD_DIM
    return pl.pallas_call(
        _gqa_kernel,
        grid=(bsz, A_KV_HEADS),
        in_specs=[
            pl.BlockSpec((1, nt, grp, TOK), lambda b, h: (b, 0, h, 0)),
            pl.BlockSpec((1, nt, 1, TOK, LANES), lambda b, h: (b, 0, h, 0, 0)),
            pl.BlockSpec((1, nt, HEAD_DIM, TOK), lambda b, h: (b, 0, h, 0)),
        ],
        out_specs=pl.BlockSpec((1, nt, grp, TOK), lambda b, h: (b, 0, h, 0)),
        out_shape=jax.ShapeDtypeStruct((bsz, nt, MIX_WIDTH, TOK), _BF16),
        scratch_shapes=_attend_scratch(HEAD_DIM, GQA_KEY_GROUP),
        compiler_params=_cparams("arbitrary", "arbitrary"),
        name="gqa_attn",
    )(qt, kt, vt)


def _diff_kernel(q_ref, k_ref, v_ref, lam_ref, sg_ref, o_ref, qcat_ref, s_ref, mx_ref, m_ref, acc_ref,
                 *, out_scale, lam_init):
    def build_qcat(j):
        for s in range(SLOT):
            qcat_ref[:, s * TOK:(s + 1) * TOK] = _pad_query(q_ref[0, j, s * HEAD_DIM:(s + 1) * HEAD_DIM, :])

    def finalize(j):
        lv = lam_ref[...]
        lam = (jnp.exp(jnp.sum(lv[0:1] * lv[1:2], axis=1, keepdims=True))
               - jnp.exp(jnp.sum(lv[2:3] * lv[3:4], axis=1, keepdims=True)) + lam_init)
        sg = _lane_tile(sg_ref[...])
        attn = acc_ref[0:C_V_DIM, :] / acc_ref[C_V_DIM:C_V_DIM + 1, :]
        for hh in range(2):
            o = attn[:, 2 * hh * TOK:(2 * hh + 1) * TOK] - lam * attn[:, (2 * hh + 1) * TOK:(2 * hh + 2) * TOK]
            ms = jnp.sum(o * o, axis=0, keepdims=True) * (1.0 / C_V_DIM)
            y = o * lax.rsqrt(ms + EPS) * sg * out_scale
            o_ref[0, j, hh * C_V_DIM:(hh + 1) * C_V_DIM, :] = y.astype(_BF16)

    head_rows = lambda s: slice((s // 2) * C_V_DIM, (s // 2 + 1) * C_V_DIM)
    _attend_tiles(k_ref.shape[1], build_qcat,
                  lambda t, n: [k_ref[0, t:t + n, s].reshape(n * TOK, LANES) for s in range(SLOT)],
                  lambda t, n, s: jnp.concatenate([v_ref[0, t + i, head_rows(s), :] for i in range(n)], axis=1),
                  finalize, qcat_ref, s_ref, mx_ref, m_ref, acc_ref)


def _diff_attention(qt, kt, vt, lamv, sg, lam_init):
    bsz, nt = qt.shape[:2]
    n_blk = kt.shape[2] // SLOT
    grp = SLOT * HEAD_DIM
    return pl.pallas_call(
        functools.partial(_diff_kernel, out_scale=1.0 - lam_init, lam_init=lam_init),
        grid=(bsz, n_blk),
        in_specs=[
            pl.BlockSpec((1, nt, grp, TOK), lambda b, c: (b, 0, c, 0)),
            pl.BlockSpec((1, nt, SLOT, TOK, LANES), lambda b, c: (b, 0, c, 0, 0)),
            pl.BlockSpec((1, nt, 2 * C_V_DIM, TOK), lambda b, c: (b, 0, c, 0)),
            pl.BlockSpec((8, LANES), lambda b, c: (0, 0)),
            pl.BlockSpec((C_V_DIM, LANES), lambda b, c: (0, 0)),
        ],
        out_specs=pl.BlockSpec((1, nt, 2 * C_V_DIM, TOK), lambda b, c: (b, 0, c, 0)),
        out_shape=jax.ShapeDtypeStruct((bsz, nt, MIX_WIDTH, TOK), _BF16),
        scratch_shapes=_attend_scratch(C_V_DIM, DIFF_KEY_GROUP),
        compiler_params=_cparams("arbitrary", "arbitrary"),
        name="diff_attn",
    )(qt, kt, vt, lamv, sg)


N_BAND = 3
NBR_QUADS_PER_TRIP = 4


def _nbr_kernel(q_ref, kc_ref, k0_ref, k1_ref, k2_ref, vc_ref, v0_ref, v1_ref, v2_ref, bias_ref, o_ref,
                qcat_ref, s_ref, mx_ref, m_ref, acc_ref):
    k_refs = (kc_ref, k0_ref, k1_ref, k2_ref)
    v_refs = (vc_ref, v0_ref, v1_ref, v2_ref)
    n_tiles = len(k_refs)
    n_quad = B_HEADS // SLOT
    n_slot = NBR_QUADS_PER_TRIP * n_tiles

    def trip(u, carry):
        quad = lambda i: NBR_QUADS_PER_TRIP * u + i // n_tiles
        par = lambda i: (i // n_tiles) % 2

        def head(i, g):
            return pl.ds(pl.multiple_of((quad(i) * SLOT + g) * HEAD_DIM, HEAD_DIM), HEAD_DIM)

        def score(i):
            c, t = quad(i), i % n_tiles
            if t == 0:
                for g in range(SLOT):
                    qcat_ref[par(i), :, g * TOK:(g + 1) * TOK] = _pad_query(q_ref[0, 0, head(i, g), :])
                m_ref[par(i)] = jnp.full(m_ref.shape[1:], NEG_INF, _F32)
                acc_ref[par(i)] = jnp.zeros(acc_ref.shape[1:], _F32)
            bias = None if t == 0 else bias_ref[0, c, t - 1]
            keys = [k_refs[t][0, 0, c * SLOT + g] for g in range(SLOT)]
            _score_stage(keys, qcat_ref.at[par(i)], s_ref, mx_ref, i % 2, bias=bias)

        score(0)
        for i in range(n_slot):
            t = i % n_tiles
            if i + 1 < n_slot:
                score(i + 1)
            acc = acc_ref.at[par(i)]
            _softmax_value_stage(lambda g, i=i, t=t: v_refs[t][0, 0, head(i, g), :],
                                 s_ref, mx_ref, m_ref.at[par(i)], acc, i % 2)
            if t == n_tiles - 1:
                o = acc[0:HEAD_DIM, :] / acc[HEAD_DIM:HEAD_DIM + 1, :]
                for g in range(SLOT):
                    o_ref[0, 0, head(i, g), :] = o[:, g * TOK:(g + 1) * TOK].astype(_BF16)
        return carry

    lax.fori_loop(0, n_quad // NBR_QUADS_PER_TRIP, trip, 0)


def _nbr_attention(qt, kt, vt, bias):
    bsz, nt = qt.shape[:2]
    n_lat = nt - 1
    tile = lambda j, b: (b, j, 0, 0)

    def band(d):
        return lambda j, b: (b, 1 + jnp.clip(j - 2, 0, n_lat - N_BAND) + d, 0, 0, 0)

    def vband(d):
        return lambda j, b: (b, 1 + jnp.clip(j - 2, 0, n_lat - N_BAND) + d, 0, 0)

    case = lambda j, b: (jnp.where(j == 0, 3, jnp.where(j == 1, 0, jnp.where(j == nt - 1, 2, 1))), 0, 0, 0, 0)
    kspec = lambda im: pl.BlockSpec((1, 1, B_HEADS, TOK, LANES), im)
    vspec = lambda im: pl.BlockSpec((1, 1, MIX_WIDTH, TOK), im)
    return pl.pallas_call(
        _nbr_kernel,
        grid=(nt, bsz),
        in_specs=[
            pl.BlockSpec((1, 1, MIX_WIDTH, TOK), tile),
            kspec(lambda j, b: (b, 0, 0, 0, 0)), kspec(band(0)), kspec(band(1)), kspec(band(2)),
            vspec(lambda j, b: (b, 0, 0, 0)), vspec(vband(0)), vspec(vband(1)), vspec(vband(2)),
            pl.BlockSpec((1, B_HEADS // SLOT, N_BAND, TOK, NCOL), case),
        ],
        out_specs=pl.BlockSpec((1, 1, MIX_WIDTH, TOK), tile),
        out_shape=jax.ShapeDtypeStruct((bsz, nt, MIX_WIDTH, TOK), _BF16),
        scratch_shapes=[
            pltpu.VMEM((2, LANES, NCOL), _BF16),
            pltpu.VMEM((2, TOK, NCOL), _F32),
            pltpu.VMEM((2, 1, NCOL), _F32),
            pltpu.VMEM((2, 1, NCOL), _F32),
            pltpu.VMEM((2, HEAD_DIM + ONES_ROWS, NCOL), _F32),
        ],
        compiler_params=_cparams("arbitrary", "arbitrary"),
        name="nbr_attn",
    )(qt, kt, kt, kt, kt, vt, vt, vt, vt, bias)


def _nbr_bias(rpb):
    rows = 4096 // GRID_W
    q_per = TOK // GRID_W
    hi = lax.Precision.HIGHEST
    kc = np.arange(GRID_W)[:, None]
    qc = np.arange(GRID_W)[None, :]
    d_col = np.clip(kc - qc + WIN_W - 1, 0, 2 * WIN_W - 2)
    cs = np.clip(qc - WIN_W // 2, 0, GRID_W - WIN_W)
    col_in = (kc >= cs) & (kc < cs + WIN_W)
    e_col = (d_col[None] == np.arange(2 * WIN_W - 1)[:, None, None]).astype(np.float32)
    by_col = jnp.einsum("hdj,jkq->hdkq", rpb.astype(_F32), e_col, precision=hi)
    by_col = by_col.reshape(B_HEADS // SLOT, SLOT, 2 * WIN_H - 1, GRID_W, GRID_W)
    tables = []
    for i in (0, 1, rows // q_per - 1):
        s = min(max(i - 1, 0), rows // q_per - N_BAND)
        kr = (q_per * s + np.arange(N_BAND * q_per))[:, None]
        qr = (q_per * i + np.arange(q_per))[None, :]
        d_row = np.clip(kr - qr + WIN_H - 1, 0, 2 * WIN_H - 2)
        rs = np.clip(qr - WIN_H // 2, 0, rows - WIN_H)
        row_in = (kr >= rs) & (kr < rs + WIN_H)
        e_row = (d_row[None] == np.arange(2 * WIN_H - 1)[:, None, None]).astype(np.float32)
        tab = jnp.einsum("dab,cgdkq->cakgbq", e_row, by_col, precision=hi)
        inside = row_in[:, None, None, :, None] & col_in[None, :, None, None, :]
        tab = jnp.where(inside[None], tab * LOG2E, NEG_INF)
        tables.append(tab.reshape(B_HEADS // SLOT, N_BAND, TOK, NCOL))
    tables.append(jnp.full_like(tables[0], NEG_INF))
    return jnp.stack(tables, axis=0)


def _rope_tables(s):
    tok = jnp.arange(s, dtype=jnp.int32)
    rows = (tok // GRID_W).astype(_F32)
    cols = (tok % GRID_W).astype(_F32)
    n_freq = HEAD_DIM // 4
    inv_freq = ROPE_BASE ** (-jnp.arange(n_freq, dtype=_F32) / n_freq)
    ang = jnp.concatenate([rows[:, None] * inv_freq, cols[:, None] * inv_freq], axis=1)
    ang = jnp.concatenate([jnp.zeros((CTX_LEN, 32), _F32), ang], axis=0)
    return jnp.cos(ang).T, jnp.sin(ang).T


def _col(v):
    return jnp.broadcast_to(v[..., None], v.shape + (LANES,))


def _lambda_init(layer):
    return 0.8 - 0.6 * math.exp(-0.3 * layer)


def kernel(x, c, ctx, c_ctx, norm_g, ada_w, ada_b, a_w_in, a_q_g, a_k_g, a_w_out, b_w_in, b_q_g, b_k_g, b_rpb, b_w_out, c_w_in, c_q_g, c_k_g, c_lam_q1, c_lam_k1, c_lam_q2, c_lam_k2, c_subln_g, c_w_out):
    bsz, s, _ = x.shape
    assert c.shape[0] + 1 <= 16
    scale = HEAD_DIM ** -0.5

    cvec = jnp.zeros((16, D_MODEL), _F32).at[:bsz].set(c).at[bsz].set(c_ctx)
    mod = _modulation(cvec, ada_w, ada_b)[:, :bsz + 1]
    sh, sc, gt = jnp.split(mod, 3, axis=-1)
    acol = _col(norm_g[:, None, :] * (1.0 + sc))
    shcol = _col(sh)
    gtcol = _col(gt)

    cos_t, sin_t = _rope_tables(s)
    w_ins, w_outs, q_gs, k_gs = (a_w_in, b_w_in, c_w_in), (a_w_out, b_w_out, c_w_out), \
        (a_q_g, b_q_g, c_q_g), (a_k_g, b_k_g, c_k_g)
    n_kv = (A_KV_HEADS * HEAD_DIM, MIX_WIDTH, MIX_WIDTH)

    ot = zt = xt = wo_t = None
    for i in range(DEPTH):
        kind, j = i % 3, i // 3
        dims = dict(n_q=MIX_WIDTH, n_k=n_kv[kind], n_v=n_kv[kind], rope=kind != 1)
        proj_args = (acol[i], shcol[i], w_ins[kind][j].T.astype(_BF16), cos_t, sin_t,
                     _col(q_gs[kind][j] * (scale * LOG2E)), _col(k_gs[kind][j]))
        if i == 0:
            xt, qt, kt, vt, zt = _first_proj(ctx, x, proj_args, **dims)
        else:
            xt, qt, kt, vt, zt = _out_in_proj(ot, zt, wo_t, gtcol[i - 1], xt, proj_args, **dims)
        if kind == 0:
            ot = _gqa_attention(qt, kt, vt)
        elif kind == 1:
            ot = _nbr_attention(qt, kt, vt, _nbr_bias(b_rpb[j]))
        else:
            lamv = jnp.zeros((8, LANES), _F32).at[0:4, :HEAD_DIM].set(
                jnp.stack([c_lam_q1[j], c_lam_k1[j], c_lam_q2[j], c_lam_k2[j]]).astype(_F32))
            ot = _diff_attention(qt, kt, vt, lamv, _col(c_subln_g[j]), _lambda_init(i))
        wo_t = w_outs[kind][j].T.astype(_BF16)

    return _last_proj(ot, zt, wo_t, gtcol[DEPTH - 1], xt)
```

```python
import functools
import math

import jax
import jax.numpy as jnp
import numpy as np
from jax import lax
from jax.experimental import pallas as pl
from jax.experimental.pallas import tpu as pltpu

D_MODEL = 1024
DEPTH = 4
GRID_W = 64
CTX_LEN = 256
HEAD_DIM = 64
MIX_WIDTH = 1024
A_HEADS = 16
A_KV_HEADS = 4
B_HEADS = 16
C_HEADS = 8
C_V_DIM = 128
WIN_H = 8
WIN_W = 16
ROPE_BASE = 10000.0
EPS = 1e-6
NEG_INF = -1e30
LOG2E = 1.4426950408889634

LANES = 128
TOK = 256
SLOT = 4
VMEM_LIMIT = 56 * 1024 * 1024

_BF16 = jnp.bfloat16
_F32 = jnp.float32


def _cparams(*sem):
    return pltpu.CompilerParams(dimension_semantics=sem, vmem_limit_bytes=VMEM_LIMIT)


def _mod_kernel(c_ref, w_ref, b_ref, o_ref):
    cv = c_ref[...]
    s = cv * jax.nn.sigmoid(cv)
    o_ref[0] = jnp.dot(s, w_ref[0], preferred_element_type=_F32) + b_ref[0]


def _modulation(cvec, ada_w, ada_b):
    n_col = 512
    return pl.pallas_call(
        _mod_kernel,
        grid=(DEPTH, 3 * D_MODEL // n_col),
        in_specs=[
            pl.BlockSpec((16, D_MODEL), lambda l, n: (0, 0)),
            pl.BlockSpec((1, D_MODEL, n_col), lambda l, n: (l, 0, n)),
            pl.BlockSpec((1, 1, n_col), lambda l, n: (l, 0, n)),
        ],
        out_specs=pl.BlockSpec((1, 16, n_col), lambda l, n: (l, 0, n)),
        out_shape=jax.ShapeDtypeStruct((DEPTH, 16, 3 * D_MODEL), _F32),
        compiler_params=_cparams("arbitrary", "arbitrary"),
        name="adaln_mod",
    )(cvec, ada_w, ada_b.reshape(DEPTH, 1, 3 * D_MODEL))


def _lane_tile(col_ref_val):
    return jnp.concatenate([col_ref_val] * (TOK // LANES), axis=1)


def _head_norm_rope(u, gain, cos, sin, rope):
    ms = jnp.sum(u * u, axis=0, keepdims=True) * (1.0 / HEAD_DIM)
    y = u * lax.rsqrt(ms + EPS) * gain
    if not rope:
        return y
    x1r, x2r, x1c, x2c = y[0:16], y[16:32], y[32:48], y[48:64]
    cr, cc = cos[0:16], cos[16:32]
    sr, sc = sin[0:16], sin[16:32]
    return jnp.concatenate(
        [x1r * cr - x2r * sr, x2r * cr + x1r * sr, x1c * cc - x2c * sc, x2c * cc + x1c * sc], axis=0)


def _proj_body(x, a_ref, sh_ref, w_ref, cos_ref, sin_ref, qg_ref, kg_ref,
               q_ref, k_ref, v_ref, g_ref, *, n_q, n_k, n_v, rope):
    ms = jnp.sum(x * x, axis=0, keepdims=True) * (1.0 / D_MODEL)
    h = x * lax.rsqrt(ms + EPS) * _lane_tile(a_ref[0]) + _lane_tile(sh_ref[0])
    hb = h.astype(_BF16)
    cos, sin = cos_ref[...], sin_ref[...]
    qg, kg = _lane_tile(qg_ref[...]), _lane_tile(kg_ref[...])

    uq = jnp.dot(w_ref[0:n_q, :], hb, preferred_element_type=_F32)
    for hd in range(n_q // HEAD_DIM):
        r = slice(hd * HEAD_DIM, (hd + 1) * HEAD_DIM)
        q_ref[0, 0, r, :] = _head_norm_rope(uq[r], qg, cos, sin, rope).astype(_BF16)

    uk = jnp.dot(w_ref[n_q:n_q + n_k, :], hb, preferred_element_type=_F32)
    for hd in range(n_k // HEAD_DIM):
        kh = _head_norm_rope(uk[hd * HEAD_DIM:(hd + 1) * HEAD_DIM], kg, cos, sin, rope)
        kt = jnp.concatenate([kh, jnp.zeros((LANES - HEAD_DIM, TOK), _F32)], axis=0)
        k_ref[0, 0, hd] = kt.T.astype(_BF16)

    o = n_q + n_k
    uv = jnp.dot(w_ref[o:o + n_v, :], hb, preferred_element_type=_F32)
    v_ref[0, 0] = uv.astype(_BF16)

    o = o + n_v
    uz = jnp.dot(w_ref[o:o + MIX_WIDTH, :], hb, preferred_element_type=_F32)
    g_ref[0] = (uz * jax.nn.sigmoid(uz)).astype(_BF16)


def _residual_update(o_ref, g_ref, w_ref, gt_ref, x_ref):
    og = (o_ref[0, 0].astype(_F32) * g_ref[0].astype(_F32)).astype(_BF16)
    y = jnp.dot(w_ref[...], og, preferred_element_type=_F32)
    return x_ref[0] + _lane_tile(gt_ref[0]) * y


def _first_kernel(ctx_ref, x_ref, *refs, **dims):
    xt_ref, proj_out = refs[7], refs[8:]
    x = jnp.where(pl.program_id(1) == 0, ctx_ref[0], x_ref[0]).T
    xt_ref[0] = x
    _proj_body(x, *refs[:7], *proj_out, **dims)


def _mid_kernel(o_ref, g_ref, wo_ref, gt_ref, x_ref, *refs, **dims):
    xt_ref, proj_out = refs[7], refs[8:]
    x = _residual_update(o_ref, g_ref, wo_ref, gt_ref, x_ref)
    xt_ref[0] = x
    _proj_body(x, *refs[:7], *proj_out, **dims)


def _last_kernel(o_ref, g_ref, wo_ref, gt_ref, x_ref, out_ref):
    out_ref[0] = _residual_update(o_ref, g_ref, wo_ref, gt_ref, x_ref).T


def _proj_specs(bsz, nt, n_in, n_q, n_k, n_v):
    col = lambda b, j: (jnp.where(j == 0, bsz, b), 0, 0)
    t = nt * TOK
    n_kh = n_k // HEAD_DIM
    in_specs = [
        pl.BlockSpec((1, D_MODEL, LANES), col),
        pl.BlockSpec((1, D_MODEL, LANES), col),
        pl.BlockSpec((n_in, D_MODEL), lambda b, j: (0, 0)),
        pl.BlockSpec((32, TOK), lambda b, j: (0, j)),
        pl.BlockSpec((32, TOK), lambda b, j: (0, j)),
        pl.BlockSpec((HEAD_DIM, LANES), lambda b, j: (0, 0)),
        pl.BlockSpec((HEAD_DIM, LANES), lambda b, j: (0, 0)),
    ]
    out_specs = [
        pl.BlockSpec((1, D_MODEL, TOK), lambda b, j: (b, 0, j)),
        pl.BlockSpec((1, 1, n_q, TOK), lambda b, j: (b, j, 0, 0)),
        pl.BlockSpec((1, 1, n_kh, TOK, LANES), lambda b, j: (b, j, 0, 0, 0)),
        pl.BlockSpec((1, 1, n_v, TOK), lambda b, j: (b, j, 0, 0)),
        pl.BlockSpec((1, MIX_WIDTH, TOK), lambda b, j: (b, 0, j)),
    ]
    out_shape = [
        jax.ShapeDtypeStruct((bsz, D_MODEL, t), _F32),
        jax.ShapeDtypeStruct((bsz, nt, n_q, TOK), _BF16),
        jax.ShapeDtypeStruct((bsz, nt, n_kh, TOK, LANES), _BF16),
        jax.ShapeDtypeStruct((bsz, nt, n_v, TOK), _BF16),
        jax.ShapeDtypeStruct((bsz, MIX_WIDTH, t), _BF16),
    ]
    return in_specs, out_specs, out_shape


def _first_proj(ctx, x, proj_args, *, n_q, n_k, n_v, rope):
    bsz, s, _ = x.shape
    nt = (ctx.shape[1] + s) // TOK
    in_specs, out_specs, out_shape = _proj_specs(bsz, nt, proj_args[2].shape[0], n_q, n_k, n_v)
    return pl.pallas_call(
        functools.partial(_first_kernel, n_q=n_q, n_k=n_k, n_v=n_v, rope=rope),
        grid=(bsz, nt),
        in_specs=[
            pl.BlockSpec((1, TOK, D_MODEL), lambda b, j: (b, 0, 0)),
            pl.BlockSpec((1, TOK, D_MODEL), lambda b, j: (b, jnp.maximum(j - 1, 0), 0)),
        ] + in_specs,
        out_specs=out_specs,
        out_shape=out_shape,
        compiler_params=_cparams("arbitrary", "arbitrary"),
        name="first_proj",
    )(ctx, x, *proj_args)


def _out_in_proj(ot, gz, wo_t, gtcol, xt, proj_args, *, n_q, n_k, n_v, rope):
    bsz, _, t = xt.shape
    nt = t // TOK
    in_specs, out_specs, out_shape = _proj_specs(bsz, nt, proj_args[2].shape[0], n_q, n_k, n_v)
    col = lambda b, j: (jnp.where(j == 0, bsz, b), 0, 0)
    tile = lambda b, j: (b, 0, j)
    return pl.pallas_call(
        functools.partial(_mid_kernel, n_q=n_q, n_k=n_k, n_v=n_v, rope=rope),
        grid=(bsz, nt),
        in_specs=[
            pl.BlockSpec((1, 1, MIX_WIDTH, TOK), lambda b, j: (b, j, 0, 0)),
            pl.BlockSpec((1, MIX_WIDTH, TOK), tile),
            pl.BlockSpec((D_MODEL, MIX_WIDTH), lambda b, j: (0, 0)),
            pl.BlockSpec((1, D_MODEL, LANES), col),
            pl.BlockSpec((1, D_MODEL, TOK), tile),
        ] + in_specs,
        out_specs=out_specs,
        out_shape=out_shape,
        input_output_aliases={4: 0},
        compiler_params=_cparams("arbitrary", "arbitrary"),
        name="out_in_proj",
    )(ot, gz, wo_t, gtcol, xt, *proj_args)


def _last_proj(ot, gz, wo_t, gtcol, xt):
    bsz, _, t = xt.shape
    n_lat = t // TOK - 1
    tile = lambda b, j: (b, 0, j + 1)
    return pl.pallas_call(
        _last_kernel,
        grid=(bsz, n_lat),
        in_specs=[
            pl.BlockSpec((1, 1, MIX_WIDTH, TOK), lambda b, j: (b, j + 1, 0, 0)),
            pl.BlockSpec((1, MIX_WIDTH, TOK), tile),
            pl.BlockSpec((D_MODEL, MIX_WIDTH), lambda b, j: (0, 0)),
            pl.BlockSpec((1, D_MODEL, LANES), lambda b, j: (b, 0, 0)),
            pl.BlockSpec((1, D_MODEL, TOK), tile),
        ],
        out_specs=pl.BlockSpec((1, TOK, D_MODEL), lambda b, j: (b, j, 0)),
        out_shape=jax.ShapeDtypeStruct((bsz, n_lat * TOK, D_MODEL), _F32),
        compiler_params=_cparams("arbitrary", "arbitrary"),
        name="last_proj",
    )(ot, gz, wo_t, gtcol, xt)


NCOL = SLOT * TOK
ONES_ROWS = 16


def _score_stage(k_tile, qcat_ref, s_ref, mx_ref, sb, rows=TOK, bias=None):
    if isinstance(k_tile, (list, tuple)):
        s = jnp.concatenate([jnp.dot(k[:, 0:HEAD_DIM], qcat_ref[:, j * TOK:(j + 1) * TOK],
                                     preferred_element_type=_F32)
                             for j, k in enumerate(k_tile)], axis=1)
    else:
        s = jnp.dot(k_tile[:, 0:HEAD_DIM], qcat_ref[...], preferred_element_type=_F32)
    if bias is not None:
        s = s + bias
    s_ref[sb, 0:rows] = s
    mx_ref[sb] = jnp.max(s, axis=0, keepdims=True)


def _softmax_value_stage(v_tile, s_ref, mx_ref, m_ref, acc_ref, sb, rows=TOK):
    m_old = m_ref[...]
    m_new = jnp.maximum(m_old, mx_ref[sb])
    alpha = jnp.exp2(m_old - m_new)
    m_ref[...] = m_new
    for j in range(SLOT):
        cols = slice(j * TOK, (j + 1) * TOK)
        v = v_tile(j)
        v_aug = jnp.concatenate([v, jnp.ones((ONES_ROWS, v.shape[1]), _BF16)], axis=0)
        p = jnp.exp2(s_ref[sb, 0:rows, cols] - m_new[:, cols]).astype(_BF16)
        acc_ref[:, cols] = (alpha[:, cols] * acc_ref[:, cols]
                            + jnp.dot(v_aug, p, preferred_element_type=_F32))


LAST_BUF = 2
GQA_KEY_GROUP = 1
DIFF_KEY_GROUP = 2


def _attend_tiles(n_t, build_qcat, k_tile, v_tile, finalize, qcat_ref, s_ref, mx_ref, m_ref, acc_ref):
    key_group = s_ref.shape[1] // TOK
    groups = [(0, 1)] + [(t, key_group) for t in range(1, n_t, key_group)]
    assert (n_t - 1) % key_group == 0
    n_g = len(groups)
    buf = lambda g: LAST_BUF if g == n_g - 1 else g % 2

    def score(g):
        t, n = groups[g]
        _score_stage(k_tile(t, n), qcat_ref, s_ref, mx_ref, buf(g), rows=n * TOK)

    def stage(g):
        t, n = groups[g]
        _softmax_value_stage(functools.partial(v_tile, t, n), s_ref, mx_ref, m_ref, acc_ref, buf(g), rows=n * TOK)

    def reset():
        m_ref[...] = jnp.full(m_ref.shape, NEG_INF, _F32)
        acc_ref[...] = jnp.zeros(acc_ref.shape, _F32)

    def start(j):
        build_qcat(j)
        score(0)

    reset()
    start(0)
    stage(0)
    finalize(0)
    reset()
    start(1)

    def tile(j, carry):
        for g in range(n_g - 1):
            score(g + 1)
            stage(g)
        stage(n_g - 1)
        finalize(j)
        reset()
        start(jnp.minimum(j + 1, n_t - 1))
        return carry

    lax.fori_loop(1, n_t, tile, 0)


def _attend_scratch(dv, key_group):
    return [
        pltpu.VMEM((HEAD_DIM, NCOL), _BF16),
        pltpu.VMEM((3, key_group * TOK, NCOL), _F32),
        pltpu.VMEM((3, 1, NCOL), _F32),
        pltpu.VMEM((1, NCOL), _F32),
        pltpu.VMEM((dv + ONES_ROWS, NCOL), _F32),
    ]


def _gqa_kernel(q_ref, k_ref, v_ref, o_ref, qcat_ref, s_ref, mx_ref, m_ref, acc_ref):
    def build_qcat(j):
        for g in range(SLOT):
            qcat_ref[:, g * TOK:(g + 1) * TOK] = q_ref[0, j, g * HEAD_DIM:(g + 1) * HEAD_DIM, :]

    def finalize(j):
        o = acc_ref[0:HEAD_DIM, :] / acc_ref[HEAD_DIM:HEAD_DIM + 1, :]
        for g in range(SLOT):
            o_ref[0, j, g * HEAD_DIM:(g + 1) * HEAD_DIM, :] = o[:, g * TOK:(g + 1) * TOK].astype(_BF16)

    _attend_tiles(k_ref.shape[1], build_qcat,
                  lambda t, n: k_ref[0, t:t + n, 0].reshape(n * TOK, LANES),
                  lambda t, n, g: jnp.concatenate([v_ref[0, t + i] for i in range(n)], axis=1),
                  finalize, qcat_ref, s_ref, mx_ref, m_ref, acc_ref)


def _gqa_attention(qt, kt, vt):
    bsz, nt = qt.shape[:2]
    grp = SLOT * HEAD_DIM
    return pl.pallas_call(
        _gqa_kernel,
        grid=(bsz, A_KV_HEADS),
        in_specs=[
            pl.BlockSpec((1, nt, grp, TOK), lambda b, h: (b, 0, h, 0)),
            pl.BlockSpec((1, nt, 1, TOK, LANES), lambda b, h: (b, 0, h, 0, 0)),
            pl.BlockSpec((1, nt, HEAD_DIM, TOK), lambda b, h: (b, 0, h, 0)),
        ],
        out_specs=pl.BlockSpec((1, nt, grp, TOK), lambda b, h: (b, 0, h, 0)),
        out_shape=jax.ShapeDtypeStruct((bsz, nt, MIX_WIDTH, TOK), _BF16),
        scratch_shapes=_attend_scratch(HEAD_DIM, GQA_KEY_GROUP),
        compiler_params=_cparams("arbitrary", "arbitrary"),
        name="gqa_attn",
    )(qt, kt, vt)


def _diff_kernel(q_ref, k_ref, v_ref, lam_ref, sg_ref, o_ref, qcat_ref, s_ref, mx_ref, m_ref, acc_ref,
                 *, out_scale, lam_init):
    def build_qcat(j):
        for s in range(SLOT):
            qcat_ref[:, s * TOK:(s + 1) * TOK] = q_ref[0, j, s * HEAD_DIM:(s + 1) * HEAD_DIM, :]

    def finalize(j):
        lv = lam_ref[...]
        lam = (jnp.exp(jnp.sum(lv[0:1] * lv[1:2], axis=1, keepdims=True))
               - jnp.exp(jnp.sum(lv[2:3] * lv[3:4], axis=1, keepdims=True)) + lam_init)
        sg = _lane_tile(sg_ref[...])
        attn = acc_ref[0:C_V_DIM, :] / acc_ref[C_V_DIM:C_V_DIM + 1, :]
        for hh in range(2):
            o = attn[:, 2 * hh * TOK:(2 * hh + 1) * TOK] - lam * attn[:, (2 * hh + 1) * TOK:(2 * hh + 2) * TOK]
            ms = jnp.sum(o * o, axis=0, keepdims=True) * (1.0 / C_V_DIM)
            y = o * lax.rsqrt(ms + EPS) * sg * out_scale
            o_ref[0, j, hh * C_V_DIM:(hh + 1) * C_V_DIM, :] = y.astype(_BF16)

    head_rows = lambda s: slice((s // 2) * C_V_DIM, (s // 2 + 1) * C_V_DIM)
    _attend_tiles(k_ref.shape[1], build_qcat,
                  lambda t, n: [k_ref[0, t:t + n, s].reshape(n * TOK, LANES) for s in range(SLOT)],
                  lambda t, n, s: jnp.concatenate([v_ref[0, t + i, head_rows(s), :] for i in range(n)], axis=1),
                  finalize, qcat_ref, s_ref, mx_ref, m_ref, acc_ref)


def _diff_attention(qt, kt, vt, lamv, sg, lam_init):
    bsz, nt = qt.shape[:2]
    n_blk = kt.shape[2] // SLOT
    grp = SLOT * HEAD_DIM
    return pl.pallas_call(
        functools.partial(_diff_kernel, out_scale=1.0 - lam_init, lam_init=lam_init),
        grid=(bsz, n_blk),
        in_specs=[
            pl.BlockSpec((1, nt, grp, TOK), lambda b, c: (b, 0, c, 0)),
            pl.BlockSpec((1, nt, SLOT, TOK, LANES), lambda b, c: (b, 0, c, 0, 0)),
            pl.BlockSpec((1, nt, 2 * C_V_DIM, TOK), lambda b, c: (b, 0, c, 0)),
            pl.BlockSpec((8, LANES), lambda b, c: (0, 0)),
            pl.BlockSpec((C_V_DIM, LANES), lambda b, c: (0, 0)),
        ],
        out_specs=pl.BlockSpec((1, nt, 2 * C_V_DIM, TOK), lambda b, c: (b, 0, c, 0)),
        out_shape=jax.ShapeDtypeStruct((bsz, nt, MIX_WIDTH, TOK), _BF16),
        scratch_shapes=_attend_scratch(C_V_DIM, DIFF_KEY_GROUP),
        compiler_params=_cparams("arbitrary", "arbitrary"),
        name="diff_attn",
    )(qt, kt, vt, lamv, sg)


N_BAND = 3
NBR_QUADS_PER_TRIP = 4


def _nbr_kernel(q_ref, kc_ref, k0_ref, k1_ref, k2_ref, vc_ref, v0_ref, v1_ref, v2_ref, bias_ref, o_ref,
                qcat_ref, s_ref, mx_ref, m_ref, acc_ref):
    k_refs = (kc_ref, k0_ref, k1_ref, k2_ref)
    v_refs = (vc_ref, v0_ref, v1_ref, v2_ref)
    n_tiles = len(k_refs)
    n_quad = B_HEADS // SLOT
    n_slot = NBR_QUADS_PER_TRIP * n_tiles

    def trip(u, carry):
        quad = lambda i: NBR_QUADS_PER_TRIP * u + i // n_tiles
        par = lambda i: (i // n_tiles) % 2

        def head(i, g):
            return pl.ds(pl.multiple_of((quad(i) * SLOT + g) * HEAD_DIM, HEAD_DIM), HEAD_DIM)

        def score(i):
            c, t = quad(i), i % n_tiles
            if t == 0:
                for g in range(SLOT):
                    qcat_ref[par(i), :, g * TOK:(g + 1) * TOK] = q_ref[0, 0, head(i, g), :]
                m_ref[par(i)] = jnp.full(m_ref.shape[1:], NEG_INF, _F32)
                acc_ref[par(i)] = jnp.zeros(acc_ref.shape[1:], _F32)
            bias = None if t == 0 else bias_ref[0, c, t - 1]
            keys = [k_refs[t][0, 0, c * SLOT + g] for g in range(SLOT)]
            _score_stage(keys, qcat_ref.at[par(i)], s_ref, mx_ref, i % 2, bias=bias)

        score(0)
        for i in range(n_slot):
            t = i % n_tiles
            if i + 1 < n_slot:
                score(i + 1)
            acc = acc_ref.at[par(i)]
            _softmax_value_stage(lambda g, i=i, t=t: v_refs[t][0, 0, head(i, g), :],
                                 s_ref, mx_ref, m_ref.at[par(i)], acc, i % 2)
            if t == n_tiles - 1:
                o = acc[0:HEAD_DIM, :] / acc[HEAD_DIM:HEAD_DIM + 1, :]
                for g in range(SLOT):
                    o_ref[0, 0, head(i, g), :] = o[:, g * TOK:(g + 1) * TOK].astype(_BF16)
        return carry

    lax.fori_loop(0, n_quad // NBR_QUADS_PER_TRIP, trip, 0)


def _nbr_attention(qt, kt, vt, bias):
    bsz, nt = qt.shape[:2]
    n_lat = nt - 1
    tile = lambda j, b: (b, j, 0, 0)

    def band(d):
        return lambda j, b: (b, 1 + jnp.clip(j - 2, 0, n_lat - N_BAND) + d, 0, 0, 0)

    def vband(d):
        return lambda j, b: (b, 1 + jnp.clip(j - 2, 0, n_lat - N_BAND) + d, 0, 0)

    case = lambda j, b: (jnp.where(j == 0, 3, jnp.where(j == 1, 0, jnp.where(j == nt - 1, 2, 1))), 0, 0, 0, 0)
    kspec = lambda im: pl.BlockSpec((1, 1, B_HEADS, TOK, LANES), im)
    vspec = lambda im: pl.BlockSpec((1, 1, MIX_WIDTH, TOK), im)
    return pl.pallas_call(
        _nbr_kernel,
        grid=(nt, bsz),
        in_specs=[
            pl.BlockSpec((1, 1, MIX_WIDTH, TOK), tile),
            kspec(lambda j, b: (b, 0, 0, 0, 0)), kspec(band(0)), kspec(band(1)), kspec(band(2)),
            vspec(lambda j, b: (b, 0, 0, 0)), vspec(vband(0)), vspec(vband(1)), vspec(vband(2)),
            pl.BlockSpec((1, B_HEADS // SLOT, N_BAND, TOK, NCOL), case),
        ],
        out_specs=pl.BlockSpec((1, 1, MIX_WIDTH, TOK), tile),
        out_shape=jax.ShapeDtypeStruct((bsz, nt, MIX_WIDTH, TOK), _BF16),
        scratch_shapes=[
            pltpu.VMEM((2, HEAD_DIM, NCOL), _BF16),
            pltpu.VMEM((2, TOK, NCOL), _F32),
            pltpu.VMEM((2, 1, NCOL), _F32),
            pltpu.VMEM((2, 1, NCOL), _F32),
            pltpu.VMEM((2, HEAD_DIM + ONES_ROWS, NCOL), _F32),
        ],
        compiler_params=_cparams("arbitrary", "arbitrary"),
        name="nbr_attn",
    )(qt, kt, kt, kt, kt, vt, vt, vt, vt, bias)


def _nbr_bias(rpb):
    rows = 4096 // GRID_W
    q_per = TOK // GRID_W
    hi = lax.Precision.HIGHEST
    kc = np.arange(GRID_W)[:, None]
    qc = np.arange(GRID_W)[None, :]
    d_col = np.clip(kc - qc + WIN_W - 1, 0, 2 * WIN_W - 2)
    cs = np.clip(qc - WIN_W // 2, 0, GRID_W - WIN_W)
    col_in = (kc >= cs) & (kc < cs + WIN_W)
    e_col = (d_col[None] == np.arange(2 * WIN_W - 1)[:, None, None]).astype(np.float32)
    by_col = jnp.einsum("hdj,jkq->hdkq", rpb.astype(_F32), e_col, precision=hi)
    by_col = by_col.reshape(B_HEADS // SLOT, SLOT, 2 * WIN_H - 1, GRID_W, GRID_W)
    tables = []
    for i in (0, 1, rows // q_per - 1):
        s = min(max(i - 1, 0), rows // q_per - N_BAND)
        kr = (q_per * s + np.arange(N_BAND * q_per))[:, None]
        qr = (q_per * i + np.arange(q_per))[None, :]
        d_row = np.clip(kr - qr + WIN_H - 1, 0, 2 * WIN_H - 2)
        rs = np.clip(qr - WIN_H // 2, 0, rows - WIN_H)
        row_in = (kr >= rs) & (kr < rs + WIN_H)
        e_row = (d_row[None] == np.arange(2 * WIN_H - 1)[:, None, None]).astype(np.float32)
        tab = jnp.einsum("dab,cgdkq->cakgbq", e_row, by_col, precision=hi)
        inside = row_in[:, None, None, :, None] & col_in[None, :, None, None, :]
        tab = jnp.where(inside[None], tab * LOG2E, NEG_INF)
        tables.append(tab.reshape(B_HEADS // SLOT, N_BAND, TOK, NCOL))
    tables.append(jnp.full_like(tables[0], NEG_INF))
    return jnp.stack(tables, axis=0)


def _rope_tables(s):
    tok = jnp.arange(s, dtype=jnp.int32)
    rows = (tok // GRID_W).astype(_F32)
    cols = (tok % GRID_W).astype(_F32)
    n_freq = HEAD_DIM // 4
    inv_freq = ROPE_BASE ** (-jnp.arange(n_freq, dtype=_F32) / n_freq)
    ang = jnp.concatenate([rows[:, None] * inv_freq, cols[:, None] * inv_freq], axis=1)
    ang = jnp.concatenate([jnp.zeros((CTX_LEN, 32), _F32), ang], axis=0)
    return jnp.cos(ang).T, jnp.sin(ang).T


def _col(v):
    return jnp.broadcast_to(v[..., None], v.shape + (LANES,))


def _lambda_init(layer):
    return 0.8 - 0.6 * math.exp(-0.3 * layer)


def kernel(x, c, ctx, c_ctx, norm_g, ada_w, ada_b, a_w_in, a_q_g, a_k_g, a_w_out, b_w_in, b_q_g, b_k_g, b_rpb, b_w_out, c_w_in, c_q_g, c_k_g, c_lam_q1, c_lam_k1, c_lam_q2, c_lam_k2, c_subln_g, c_w_out):
    bsz, s, _ = x.shape
    assert c.shape[0] + 1 <= 16
    scale = HEAD_DIM ** -0.5

    cvec = jnp.zeros((16, D_MODEL), _F32).at[:bsz].set(c).at[bsz].set(c_ctx)
    mod = _modulation(cvec, ada_w, ada_b)[:, :bsz + 1]
    sh, sc, gt = jnp.split(mod, 3, axis=-1)
    acol = _col(norm_g[:, None, :] * (1.0 + sc))
    shcol = _col(sh)
    gtcol = _col(gt)

    cos_t, sin_t = _rope_tables(s)
    w_ins, w_outs, q_gs, k_gs = (a_w_in, b_w_in, c_w_in), (a_w_out, b_w_out, c_w_out), \
        (a_q_g, b_q_g, c_q_g), (a_k_g, b_k_g, c_k_g)
    n_kv = (A_KV_HEADS * HEAD_DIM, MIX_WIDTH, MIX_WIDTH)

    ot = zt = xt = wo_t = None
    for i in range(DEPTH):
        kind, j = i % 3, i // 3
        dims = dict(n_q=MIX_WIDTH, n_k=n_kv[kind], n_v=n_kv[kind], rope=kind != 1)
        proj_args = (acol[i], shcol[i], w_ins[kind][j].T.astype(_BF16), cos_t, sin_t,
                     _col(q_gs[kind][j] * (scale * LOG2E)), _col(k_gs[kind][j]))
        if i == 0:
            xt, qt, kt, vt, zt = _first_proj(ctx, x, proj_args, **dims)
        else:
            xt, qt, kt, vt, zt = _out_in_proj(ot, zt, wo_t, gtcol[i - 1], xt, proj_args, **dims)
        if kind == 0:
            ot = _gqa_attention(qt, kt, vt)
        elif kind == 1:
            ot = _nbr_attention(qt, kt, vt, _nbr_bias(b_rpb[j]))
        else:
            lamv = jnp.zeros((8, LANES), _F32).at[0:4, :HEAD_DIM].set(
                jnp.stack([c_lam_q1[j], c_lam_k1[j], c_lam_q2[j], c_lam_k2[j]]).astype(_F32))
            ot = _diff_attention(qt, kt, vt, lamv, _col(c_subln_g[j]), _lambda_init(i))
        wo_t = w_outs[kind][j].T.astype(_BF16)

    return _last_proj(ot, zt, wo_t, gtcol[DEPTH - 1], xt)
```

```python
import functools
import math

import jax
import jax.numpy as jnp
import numpy as np
from jax import lax
from jax.experimental import pallas as pl
from jax.experimental.pallas import tpu as pltpu

D_MODEL = 1024
DEPTH = 4
GRID_W = 64
CTX_LEN = 256
HEAD_DIM = 64
MIX_WIDTH = 1024
A_HEADS = 16
A_KV_HEADS = 4
B_HEADS = 16
C_HEADS = 8
C_V_DIM = 128
WIN_H = 8
WIN_W = 16
ROPE_BASE = 10000.0
EPS = 1e-6
NEG_INF = -1e30
LOG2E = 1.4426950408889634

LANES = 128
TOK = 256
SLOT = 4
VMEM_LIMIT = 56 * 1024 * 1024

_BF16 = jnp.bfloat16
_F32 = jnp.float32


def _cparams(*sem):
    return pltpu.CompilerParams(dimension_semantics=sem, vmem_limit_bytes=VMEM_LIMIT)


def _mod_kernel(c_ref, w_ref, b_ref, o_ref):
    cv = c_ref[...]
    s = cv * jax.nn.sigmoid(cv)
    o_ref[0] = jnp.dot(s, w_ref[0], preferred_element_type=_F32) + b_ref[0]


def _modulation(cvec, ada_w, ada_b):
    n_col = 512
    return pl.pallas_call(
        _mod_kernel,
        grid=(DEPTH, 3 * D_MODEL // n_col),
        in_specs=[
            pl.BlockSpec((16, D_MODEL), lambda l, n: (0, 0)),
            pl.BlockSpec((1, D_MODEL, n_col), lambda l, n: (l, 0, n)),
            pl.BlockSpec((1, 1, n_col), lambda l, n: (l, 0, n)),
        ],
        out_specs=pl.BlockSpec((1, 16, n_col), lambda l, n: (l, 0, n)),
        out_shape=jax.ShapeDtypeStruct((DEPTH, 16, 3 * D_MODEL), _F32),
        compiler_params=_cparams("arbitrary", "arbitrary"),
        name="adaln_mod",
    )(cvec, ada_w, ada_b.reshape(DEPTH, 1, 3 * D_MODEL))


def _lane_tile(col_ref_val):
    return jnp.concatenate([col_ref_val] * (TOK // LANES), axis=1)


def _head_norm_rope(u, gain, cos, sin, rope):
    ms = jnp.sum(u * u, axis=0, keepdims=True) * (1.0 / HEAD_DIM)
    y = u * lax.rsqrt(ms + EPS) * gain
    if not rope:
        return y
    x1r, x2r, x1c, x2c = y[0:16], y[16:32], y[32:48], y[48:64]
    cr, cc = cos[0:16], cos[16:32]
    sr, sc = sin[0:16], sin[16:32]
    return jnp.concatenate(
        [x1r * cr - x2r * sr, x2r * cr + x1r * sr, x1c * cc - x2c * sc, x2c * cc + x1c * sc], axis=0)


def _proj_body(x, a_ref, sh_ref, w_ref, cos_ref, sin_ref, qg_ref, kg_ref,
               q_ref, k_ref, v_ref, g_ref, *, n_q, n_k, n_v, rope):
    ms = jnp.sum(x * x, axis=0, keepdims=True) * (1.0 / D_MODEL)
    h = x * lax.rsqrt(ms + EPS) * _lane_tile(a_ref[0]) + _lane_tile(sh_ref[0])
    hb = h.astype(_BF16)
    cos, sin = cos_ref[...], sin_ref[...]
    qg, kg = _lane_tile(qg_ref[...]), _lane_tile(kg_ref[...])

    uq = jnp.dot(w_ref[0:n_q, :], hb, preferred_element_type=_F32)
    for hd in range(n_q // HEAD_DIM):
        r = slice(hd * HEAD_DIM, (hd + 1) * HEAD_DIM)
        q_ref[0, 0, r, :] = _head_norm_rope(uq[r], qg, cos, sin, rope).astype(_BF16)

    uk = jnp.dot(w_ref[n_q:n_q + n_k, :], hb, preferred_element_type=_F32)
    for hd in range(n_k // HEAD_DIM):
        kh = _head_norm_rope(uk[hd * HEAD_DIM:(hd + 1) * HEAD_DIM], kg, cos, sin, rope)
        kt = jnp.concatenate([kh, jnp.zeros((LANES - HEAD_DIM, TOK), _F32)], axis=0)
        k_ref[0, 0, hd] = kt.T.astype(_BF16)

    o = n_q + n_k
    uv = jnp.dot(w_ref[o:o + n_v, :], hb, preferred_element_type=_F32)
    v_ref[0, 0] = uv.astype(_BF16)

    o = o + n_v
    uz = jnp.dot(w_ref[o:o + MIX_WIDTH, :], hb, preferred_element_type=_F32)
    g_ref[0] = (uz * jax.nn.sigmoid(uz)).astype(_BF16)


def _residual_update(o_ref, g_ref, w_ref, gt_ref, x_ref):
    og = (o_ref[0, 0].astype(_F32) * g_ref[0].astype(_F32)).astype(_BF16)
    y = jnp.dot(w_ref[...], og, preferred_element_type=_F32)
    return x_ref[0] + _lane_tile(gt_ref[0]) * y


def _first_kernel(ctx_ref, x_ref, *refs, **dims):
    xt_ref, proj_out = refs[7], refs[8:]
    x = jnp.where(pl.program_id(1) == 0, ctx_ref[0], x_ref[0]).T
    xt_ref[0] = x
    _proj_body(x, *refs[:7], *proj_out, **dims)


def _mid_kernel(o_ref, g_ref, wo_ref, gt_ref, x_ref, *refs, **dims):
    xt_ref, proj_out = refs[7], refs[8:]
    x = _residual_update(o_ref, g_ref, wo_ref, gt_ref, x_ref)
    xt_ref[0] = x
    _proj_body(x, *refs[:7], *proj_out, **dims)


def _last_kernel(o_ref, g_ref, wo_ref, gt_ref, x_ref, out_ref):
    out_ref[0] = _residual_update(o_ref, g_ref, wo_ref, gt_ref, x_ref).T


def _proj_specs(bsz, nt, n_in, n_q, n_k, n_v):
    col = lambda b, j: (jnp.where(j == 0, bsz, b), 0, 0)
    t = nt * TOK
    n_kh = n_k // HEAD_DIM
    in_specs = [
        pl.BlockSpec((1, D_MODEL, LANES), col),
        pl.BlockSpec((1, D_MODEL, LANES), col),
        pl.BlockSpec((n_in, D_MODEL), lambda b, j: (0, 0)),
        pl.BlockSpec((32, TOK), lambda b, j: (0, j)),
        pl.BlockSpec((32, TOK), lambda b, j: (0, j)),
        pl.BlockSpec((HEAD_DIM, LANES), lambda b, j: (0, 0)),
        pl.BlockSpec((HEAD_DIM, LANES), lambda b, j: (0, 0)),
    ]
    out_specs = [
        pl.BlockSpec((1, D_MODEL, TOK), lambda b, j: (b, 0, j)),
        pl.BlockSpec((1, 1, n_q, TOK), lambda b, j: (b, j, 0, 0)),
        pl.BlockSpec((1, 1, n_kh, TOK, LANES), lambda b, j: (b, j, 0, 0, 0)),
        pl.BlockSpec((1, 1, n_v, TOK), lambda b, j: (b, j, 0, 0)),
        pl.BlockSpec((1, MIX_WIDTH, TOK), lambda b, j: (b, 0, j)),
    ]
    out_shape = [
        jax.ShapeDtypeStruct((bsz, D_MODEL, t), _F32),
        jax.ShapeDtypeStruct((bsz, nt, n_q, TOK), _BF16),
        jax.ShapeDtypeStruct((bsz, nt, n_kh, TOK, LANES), _BF16),
        jax.ShapeDtypeStruct((bsz, nt, n_v, TOK), _BF16),
        jax.ShapeDtypeStruct((bsz, MIX_WIDTH, t), _BF16),
    ]
    return in_specs, out_specs, out_shape


def _first_proj(ctx, x, proj_args, *, n_q, n_k, n_v, rope):
    bsz, s, _ = x.shape
    nt = (ctx.shape[1] + s) // TOK
    in_specs, out_specs, out_shape = _proj_specs(bsz, nt, proj_args[2].shape[0], n_q, n_k, n_v)
    return pl.pallas_call(
        functools.partial(_first_kernel, n_q=n_q, n_k=n_k, n_v=n_v, rope=rope),
        grid=(bsz, nt),
        in_specs=[
            pl.BlockSpec((1, TOK, D_MODEL), lambda b, j: (b, 0, 0)),
            pl.BlockSpec((1, TOK, D_MODEL), lambda b, j: (b, jnp.maximum(j - 1, 0), 0)),
        ] + in_specs,
        out_specs=out_specs,
        out_shape=out_shape,
        compiler_params=_cparams("arbitrary", "arbitrary"),
        name="first_proj",
    )(ctx, x, *proj_args)


def _out_in_proj(ot, gz, wo_t, gtcol, xt, proj_args, *, n_q, n_k, n_v, rope):
    bsz, _, t = xt.shape
    nt = t // TOK
    in_specs, out_specs, out_shape = _proj_specs(bsz, nt, proj_args[2].shape[0], n_q, n_k, n_v)
    col = lambda b, j: (jnp.where(j == 0, bsz, b), 0, 0)
    tile = lambda b, j: (b, 0, j)
    return pl.pallas_call(
        functools.partial(_mid_kernel, n_q=n_q, n_k=n_k, n_v=n_v, rope=rope),
        grid=(bsz, nt),
        in_specs=[
            pl.BlockSpec((1, 1, MIX_WIDTH, TOK), lambda b, j: (b, j, 0, 0)),
            pl.BlockSpec((1, MIX_WIDTH, TOK), tile),
            pl.BlockSpec((D_MODEL, MIX_WIDTH), lambda b, j: (0, 0)),
            pl.BlockSpec((1, D_MODEL, LANES), col),
            pl.BlockSpec((1, D_MODEL, TOK), tile),
        ] + in_specs,
        out_specs=out_specs,
        out_shape=out_shape,
        input_output_aliases={4: 0},
        compiler_params=_cparams("arbitrary", "arbitrary"),
        name="out_in_proj",
    )(ot, gz, wo_t, gtcol, xt, *proj_args)


def _last_proj(ot, gz, wo_t, gtcol, xt):
    bsz, _, t = xt.shape
    n_lat = t // TOK - 1
    tile = lambda b, j: (b, 0, j + 1)
    return pl.pallas_call(
        _last_kernel,
        grid=(bsz, n_lat),
        in_specs=[
            pl.BlockSpec((1, 1, MIX_WIDTH, TOK), lambda b, j: (b, j + 1, 0, 0)),
            pl.BlockSpec((1, MIX_WIDTH, TOK), tile),
            pl.BlockSpec((D_MODEL, MIX_WIDTH), lambda b, j: (0, 0)),
            pl.BlockSpec((1, D_MODEL, LANES), lambda b, j: (b, 0, 0)),
            pl.BlockSpec((1, D_MODEL, TOK), tile),
        ],
        out_specs=pl.BlockSpec((1, TOK, D_MODEL), lambda b, j: (b, j, 0)),
        out_shape=jax.ShapeDtypeStruct((bsz, n_lat * TOK, D_MODEL), _F32),
        compiler_params=_cparams("arbitrary", "arbitrary"),
        name="last_proj",
    )(ot, gz, wo_t, gtcol, xt)


NCOL = SLOT * TOK
ONES_ROWS = 16


def _score_stage(k_tile, qcat_ref, s_ref, mx_ref, sb, rows=TOK, bias=None):
    if isinstance(k_tile, (list, tuple)):
        s = jnp.concatenate([jnp.dot(k[:, 0:HEAD_DIM], qcat_ref[:, j * TOK:(j + 1) * TOK],
                                     preferred_element_type=_F32)
                             for j, k in enumerate(k_tile)], axis=1)
    else:
        s = jnp.dot(k_tile[:, 0:HEAD_DIM], qcat_ref[...], preferred_element_type=_F32)
    if bias is not None:
        s = s + bias
    s_ref[sb, 0:rows] = s
    mx_ref[sb] = jnp.max(s, axis=0, keepdims=True)


def _softmax_value_stage(v_tile, s_ref, mx_ref, m_ref, acc_ref, sb, rows=TOK):
    m_old = m_ref[...]
    m_new = jnp.maximum(m_old, mx_ref[sb])
    alpha = jnp.exp2(m_old - m_new)
    m_ref[...] = m_new
    for j in range(SLOT):
        cols = slice(j * TOK, (j + 1) * TOK)
        v = v_tile(j)
        v_aug = jnp.concatenate([v, jnp.ones((ONES_ROWS, v.shape[1]), _BF16)], axis=0)
        p = jnp.exp2(s_ref[sb, 0:rows, cols] - m_new[:, cols]).astype(_BF16)
        acc_ref[:, cols] = (alpha[:, cols] * acc_ref[:, cols]
                            + jnp.dot(v_aug, p, preferred_element_type=_F32))


LAST_BUF = 2
GQA_KEY_GROUP = 1
DIFF_KEY_GROUP = 2


def _attend_tiles(n_t, build_qcat, k_tile, v_tile, finalize, qcat_ref, s_ref, mx_ref, m_ref, acc_ref):
    key_group = s_ref.shape[1] // TOK
    groups = [(0, 1)] + [(t, key_group) for t in range(1, n_t, key_group)]
    assert (n_t - 1) % key_group == 0
    n_g = len(groups)
    buf = lambda g: LAST_BUF if g == n_g - 1 else g % 2

    def score(g):
        t, n = groups[g]
        _score_stage(k_tile(t, n), qcat_ref, s_ref, mx_ref, buf(g), rows=n * TOK)

    def stage(g):
        t, n = groups[g]
        _softmax_value_stage(functools.partial(v_tile, t, n), s_ref, mx_ref, m_ref, acc_ref, buf(g), rows=n * TOK)

    def reset():
        m_ref[...] = jnp.full(m_ref.shape, NEG_INF, _F32)
        acc_ref[...] = jnp.zeros(acc_ref.shape, _F32)

    def start(j):
        build_qcat(j)
        score(0)

    reset()
    start(0)
    stage(0)
    finalize(0)
    reset()
    start(1)

    def tile(j, carry):
        for g in range(n_g - 1):
            score(g + 1)
            stage(g)
        stage(n_g - 1)
        finalize(j)
        reset()
        start(jnp.minimum(j + 1, n_t - 1))
        return carry

    lax.fori_loop(1, n_t, tile, 0)


def _attend_scratch(dv, key_group):
    return [
        pltpu.VMEM((HEAD_DIM, NCOL), _BF16),
        pltpu.VMEM((3, key_group * TOK, NCOL), _F32),
        pltpu.VMEM((3, 1, NCOL), _F32),
        pltpu.VMEM((1, NCOL), _F32),
        pltpu.VMEM((dv + ONES_ROWS, NCOL), _F32),
    ]


def _gqa_kernel(q_ref, k_ref, v_ref, o_ref, qcat_ref, s_ref, mx_ref, m_ref, acc_ref):
    def build_qcat(j):
        for g in range(SLOT):
            qcat_ref[:, g * TOK:(g + 1) * TOK] = q_ref[0, j, g * HEAD_DIM:(g + 1) * HEAD_DIM, :]

    def finalize(j):
        o = acc_ref[0:HEAD_DIM, :] / acc_ref[HEAD_DIM:HEAD_DIM + 1, :]
        for g in range(SLOT):
            o_ref[0, j, g * HEAD_DIM:(g + 1) * HEAD_DIM, :] = o[:, g * TOK:(g + 1) * TOK].astype(_BF16)

    _attend_tiles(k_ref.shape[1], build_qcat,
                  lambda t, n: k_ref[0, t:t + n, 0].reshape(n * TOK, LANES),
                  lambda t, n, g: jnp.concatenate([v_ref[0, t + i] for i in range(n)], axis=1),
                  finalize, qcat_ref, s_ref, mx_ref, m_ref, acc_ref)


def _gqa_attention(qt, kt, vt):
    bsz, nt = qt.shape[:2]
    grp = SLOT * HEAD_DIM
    return pl.pallas_call(
        _gqa_kernel,
        grid=(bsz, A_KV_HEADS),
        in_specs=[
            pl.BlockSpec((1, nt, grp, TOK), lambda b, h: (b, 0, h, 0)),
            pl.BlockSpec((1, nt, 1, TOK, LANES), lambda b, h: (b, 0, h, 0, 0)),
            pl.BlockSpec((1, nt, HEAD_DIM, TOK), lambda b, h: (b, 0, h, 0)),
        ],
        out_specs=pl.BlockSpec((1, nt, grp, TOK), lambda b, h: (b, 0, h, 0)),
        out_shape=jax.ShapeDtypeStruct((bsz, nt, MIX_WIDTH, TOK), _BF16),
        scratch_shapes=_attend_scratch(HEAD_DIM, GQA_KEY_GROUP),
        compiler_params=_cparams("arbitrary", "arbitrary"),
        name="gqa_attn",
    )(qt, kt, vt)


def _diff_kernel(q_ref, k_ref, v_ref, lam_ref, sg_ref, o_ref, qcat_ref, s_ref, mx_ref, m_ref, acc_ref,
                 *, out_scale, lam_init):
    def build_qcat(j):
        for s in range(SLOT):
            qcat_ref[:, s * TOK:(s + 1) * TOK] = q_ref[0, j, s * HEAD_DIM:(s + 1) * HEAD_DIM, :]

    def finalize(j):
        lv = lam_ref[...]
        lam = (jnp.exp(jnp.sum(lv[0:1] * lv[1:2], axis=1, keepdims=True))
               - jnp.exp(jnp.sum(lv[2:3] * lv[3:4], axis=1, keepdims=True)) + lam_init)
        sg = _lane_tile(sg_ref[...])
        attn = acc_ref[0:C_V_DIM, :] / acc_ref[C_V_DIM:C_V_DIM + 1, :]
        for hh in range(2):
            o = attn[:, 2 * hh * TOK:(2 * hh + 1) * TOK] - lam * attn[:, (2 * hh + 1) * TOK:(2 * hh + 2) * TOK]
            ms = jnp.sum(o * o, axis=0, keepdims=True) * (1.0 / C_V_DIM)
            y = o * lax.rsqrt(ms + EPS) * sg * out_scale
            o_ref[0, j, hh * C_V_DIM:(hh + 1) * C_V_DIM, :] = y.astype(_BF16)

    head_rows = lambda s: slice((s // 2) * C_V_DIM, (s // 2 + 1) * C_V_DIM)
    _attend_tiles(k_ref.shape[1], build_qcat,
                  lambda t, n: [k_ref[0, t:t + n, s].reshape(n * TOK, LANES) for s in range(SLOT)],
                  lambda t, n, s: jnp.concatenate([v_ref[0, t + i, head_rows(s), :] for i in range(n)], axis=1),
                  finalize, qcat_ref, s_ref, mx_ref, m_ref, acc_ref)


def _diff_attention(qt, kt, vt, lamv, sg, lam_init):
    bsz, nt = qt.shape[:2]
    n_blk = kt.shape[2] // SLOT
    grp = SLOT * HEAD_DIM
    return pl.pallas_call(
        functools.partial(_diff_kernel, out_scale=1.0 - lam_init, lam_init=lam_init),
        grid=(bsz, n_blk),
        in_specs=[
            pl.BlockSpec((1, nt, grp, TOK), lambda b, c: (b, 0, c, 0)),
            pl.BlockSpec((1, nt, SLOT, TOK, LANES), lambda b, c: (b, 0, c, 0, 0)),
            pl.BlockSpec((1, nt, 2 * C_V_DIM, TOK), lambda b, c: (b, 0, c, 0)),
            pl.BlockSpec((8, LANES), lambda b, c: (0, 0)),
            pl.BlockSpec((C_V_DIM, LANES), lambda b, c: (0, 0)),
        ],
        out_specs=pl.BlockSpec((1, nt, 2 * C_V_DIM, TOK), lambda b, c: (b, 0, c, 0)),
        out_shape=jax.ShapeDtypeStruct((bsz, nt, MIX_WIDTH, TOK), _BF16),
        scratch_shapes=_attend_scratch(C_V_DIM, DIFF_KEY_GROUP),
        compiler_params=_cparams("arbitrary", "arbitrary"),
        name="diff_attn",
    )(qt, kt, vt, lamv, sg)


N_BAND = 3
NBR_QUADS_PER_TRIP = 4


def _nbr_kernel(q_ref, kc_ref, k0_ref, k1_ref, k2_ref, vc_ref, v0_ref, v1_ref, v2_ref, bias_ref, o_ref,
                qcat_ref, s_ref, mx_ref, m_ref, acc_ref):
    k_refs = (kc_ref, k0_ref, k1_ref, k2_ref)
    v_refs = (vc_ref, v0_ref, v1_ref, v2_ref)
    n_tiles = len(k_refs)
    n_quad = B_HEADS // SLOT
    n_slot = NBR_QUADS_PER_TRIP * n_tiles

    def trip(u, carry):
        quad = lambda i: NBR_QUADS_PER_TRIP * u + i // n_tiles
        par = lambda i: (i // n_tiles) % 2

        def head(i, g):
            return pl.ds(pl.multiple_of((quad(i) * SLOT + g) * HEAD_DIM, HEAD_DIM), HEAD_DIM)

        def score(i):
            c, t = quad(i), i % n_tiles
            if t == 0:
                for g in range(SLOT):
                    qcat_ref[par(i), :, g * TOK:(g + 1) * TOK] = q_ref[0, 0, head(i, g), :]
                m_ref[par(i)] = jnp.full(m_ref.shape[1:], NEG_INF, _F32)
                acc_ref[par(i)] = jnp.zeros(acc_ref.shape[1:], _F32)
            bias = None if t == 0 else bias_ref[0, c, t - 1]
            keys = [k_refs[t][0, 0, c * SLOT + g] for g in range(SLOT)]
            _score_stage(keys, qcat_ref.at[par(i)], s_ref, mx_ref, i % 2, bias=bias)

        score(0)
        for i in range(n_slot):
            t = i % n_tiles
            if i + 1 < n_slot:
                score(i + 1)
            acc = acc_ref.at[par(i)]
            _softmax_value_stage(lambda g, i=i, t=t: v_refs[t][0, 0, head(i, g), :],
                                 s_ref, mx_ref, m_ref.at[par(i)], acc, i % 2)
            if t == n_tiles - 1:
                o = acc[0:HEAD_DIM, :] / acc[HEAD_DIM:HEAD_DIM + 1, :]
                for g in range(SLOT):
                    o_ref[0, 0, head(i, g), :] = o[:, g * TOK:(g + 1) * TOK].astype(_BF16)
        return carry

    lax.fori_loop(0, n_quad // NBR_QUADS_PER_TRIP, trip, 0)


def _nbr_attention(qt, kt, vt, bias):
    bsz, nt = qt.shape[:2]
    n_lat = nt - 1
    tile = lambda j, b: (b, j, 0, 0)

    def band(d):
        return lambda j, b: (b, 1 + jnp.clip(j - 2, 0, n_lat - N_BAND) + d, 0, 0, 0)

    def vband(d):
        return lambda j, b: (b, 1 + jnp.clip(j - 2, 0, n_lat - N_BAND) + d, 0, 0)

    case = lambda j, b: (jnp.where(j == 0, 3, jnp.where(j == 1, 0, jnp.where(j == nt - 1, 2, 1))), 0, 0, 0, 0)
    kspec = lambda im: pl.BlockSpec((1, 1, B_HEADS, TOK, LANES), im)
    vspec = lambda im: pl.BlockSpec((1, 1, MIX_WIDTH, TOK), im)
    return pl.pallas_call(
        _nbr_kernel,
        grid=(nt, bsz),
        in_specs=[
            pl.BlockSpec((1, 1, MIX_WIDTH, TOK), tile),
            kspec(lambda j, b: (b, 0, 0, 0, 0)), kspec(band(0)), kspec(band(1)), kspec(band(2)),
            vspec(lambda j, b: (b, 0, 0, 0)), vspec(vband(0)), vspec(vband(1)), vspec(vband(2)),
            pl.BlockSpec((1, B_HEADS // SLOT, N_BAND, TOK, NCOL), case),
        ],
        out_specs=pl.BlockSpec((1, 1, MIX_WIDTH, TOK), tile),
        out_shape=jax.ShapeDtypeStruct((bsz, nt, MIX_WIDTH, TOK), _BF16),
        scratch_shapes=[
            pltpu.VMEM((2, HEAD_DIM, NCOL), _BF16),
            pltpu.VMEM((2, TOK, NCOL), _F32),
            pltpu.VMEM((2, 1, NCOL), _F32),
            pltpu.VMEM((2, 1, NCOL), _F32),
            pltpu.VMEM((2, HEAD_DIM + ONES_ROWS, NCOL), _F32),
        ],
        compiler_params=_cparams("arbitrary", "arbitrary"),
        name="nbr_attn",
    )(qt, kt, kt, kt, kt, vt, vt, vt, vt, bias)


def _nbr_bias(rpb):
    rows = 4096 // GRID_W
    q_per = TOK // GRID_W
    hi = lax.Precision.HIGHEST
    kc = np.arange(GRID_W)[:, None]
    qc = np.arange(GRID_W)[None, :]
    d_col = np.clip(kc - qc + WIN_W - 1, 0, 2 * WIN_W - 2)
    cs = np.clip(qc - WIN_W // 2, 0, GRID_W - WIN_W)
    col_in = (kc >= cs) & (kc < cs + WIN_W)
    e_col = (d_col[None] == np.arange(2 * WIN_W - 1)[:, None, None]).astype(np.float32)
    by_col = jnp.einsum("hdj,jkq->hdkq", rpb.astype(_F32), e_col, precision=hi)
    by_col = by_col.reshape(B_HEADS // SLOT, SLOT, 2 * WIN_H - 1, GRID_W, GRID_W)
    e_rows, row_ins = [], []
    for i in (0, 1, rows // q_per - 1):
        s = min(max(i - 1, 0), rows // q_per - N_BAND)
        kr = (q_per * s + np.arange(N_BAND * q_per))[:, None]
        qr = (q_per * i + np.arange(q_per))[None, :]
        d_row = np.clip(kr - qr + WIN_H - 1, 0, 2 * WIN_H - 2)
        rs = np.clip(qr - WIN_H // 2, 0, rows - WIN_H)
        row_ins.append((kr >= rs) & (kr < rs + WIN_H))
        e_rows.append((d_row[None] == np.arange(2 * WIN_H - 1)[:, None, None]).astype(np.float32))
    row_ins.append(np.zeros_like(row_ins[0]))
    e_rows.append(np.zeros_like(e_rows[0]))
    row_in, e_row = np.stack(row_ins), np.stack(e_rows)
    tab = jnp.einsum("zdab,cgdkq->zcakgbq", e_row, by_col, precision=hi)
    inside = row_in[:, None, :, None, None, :, None] & col_in[None, None, None, :, None, None, :]
    tab = jnp.where(inside, tab * LOG2E, NEG_INF)
    return tab.reshape(len(e_rows), B_HEADS // SLOT, N_BAND, TOK, NCOL)


def _rope_tables(s):
    tok = jnp.arange(s, dtype=jnp.int32)
    rows = (tok // GRID_W).astype(_F32)
    cols = (tok % GRID_W).astype(_F32)
    n_freq = HEAD_DIM // 4
    inv_freq = ROPE_BASE ** (-jnp.arange(n_freq, dtype=_F32) / n_freq)
    ang = jnp.concatenate([rows[:, None] * inv_freq, cols[:, None] * inv_freq], axis=1)
    ang = jnp.concatenate([jnp.zeros((CTX_LEN, 32), _F32), ang], axis=0)
    return jnp.cos(ang).T, jnp.sin(ang).T


def _col(v):
    return jnp.broadcast_to(v[..., None], v.shape + (LANES,))


def _lambda_init(layer):
    return 0.8 - 0.6 * math.exp(-0.3 * layer)


def kernel(x, c, ctx, c_ctx, norm_g, ada_w, ada_b, a_w_in, a_q_g, a_k_g, a_w_out, b_w_in, b_q_g, b_k_g, b_rpb, b_w_out, c_w_in, c_q_g, c_k_g, c_lam_q1, c_lam_k1, c_lam_q2, c_lam_k2, c_subln_g, c_w_out):
    bsz, s, _ = x.shape
    assert c.shape[0] + 1 <= 16
    scale = HEAD_DIM ** -0.5

    cvec = jnp.zeros((16, D_MODEL), _F32).at[:bsz].set(c).at[bsz].set(c_ctx)
    mod = _modulation(cvec, ada_w, ada_b)[:, :bsz + 1]
    sh, sc, gt = jnp.split(mod, 3, axis=-1)
    acol = _col(norm_g[:, None, :] * (1.0 + sc))
    shcol = _col(sh)
    gtcol = _col(gt)

    cos_t, sin_t = _rope_tables(s)
    w_ins, w_outs, q_gs, k_gs = (a_w_in, b_w_in, c_w_in), (a_w_out, b_w_out, c_w_out), \
        (a_q_g, b_q_g, c_q_g), (a_k_g, b_k_g, c_k_g)
    n_kv = (A_KV_HEADS * HEAD_DIM, MIX_WIDTH, MIX_WIDTH)

    ot = zt = xt = wo_t = None
    for i in range(DEPTH):
        kind, j = i % 3, i // 3
        dims = dict(n_q=MIX_WIDTH, n_k=n_kv[kind], n_v=n_kv[kind], rope=kind != 1)
        proj_args = (acol[i], shcol[i], w_ins[kind][j].T.astype(_BF16), cos_t, sin_t,
                     _col(q_gs[kind][j] * (scale * LOG2E)), _col(k_gs[kind][j]))
        if i == 0:
            xt, qt, kt, vt, zt = _first_proj(ctx, x, proj_args, **dims)
        else:
            xt, qt, kt, vt, zt = _out_in_proj(ot, zt, wo_t, gtcol[i - 1], xt, proj_args, **dims)
        if kind == 0:
            ot = _gqa_attention(qt, kt, vt)
        elif kind == 1:
            ot = _nbr_attention(qt, kt, vt, _nbr_bias(b_rpb[j]))
        else:
            lamv = jnp.zeros((8, LANES), _F32).at[0:4, :HEAD_DIM].set(
                jnp.stack([c_lam_q1[j], c_lam_k1[j], c_lam_q2[j], c_lam_k2[j]]).astype(_F32))
            ot = _diff_attention(qt, kt, vt, lamv, _col(c_subln_g[j]), _lambda_init(i))
        wo_t = w_outs[kind][j].T.astype(_BF16)

    return _last_proj(ot, zt, wo_t, gtcol[DEPTH - 1], xt)
```

```python
import functools
import math

import jax
import jax.numpy as jnp
import numpy as np
from jax import lax
from jax.experimental import pallas as pl
from jax.experimental.pallas import tpu as pltpu

D_MODEL = 1024
DEPTH = 4
GRID_W = 64
CTX_LEN = 256
HEAD_DIM = 64
MIX_WIDTH = 1024
A_HEADS = 16
A_KV_HEADS = 4
B_HEADS = 16
C_HEADS = 8
C_V_DIM = 128
WIN_H = 8
WIN_W = 16
ROPE_BASE = 10000.0
EPS = 1e-6
NEG_INF = -1e30
LOG2E = 1.4426950408889634

LANES = 128
TOK = 256
SLOT = 4
VMEM_LIMIT = 56 * 1024 * 1024

_BF16 = jnp.bfloat16
_F32 = jnp.float32


def _cparams(*sem):
    return pltpu.CompilerParams(dimension_semantics=sem, vmem_limit_bytes=VMEM_LIMIT)


def _mod_kernel(c_ref, w_ref, b_ref, o_ref):
    cv = c_ref[...]
    s = cv * jax.nn.sigmoid(cv)
    o_ref[0] = jnp.dot(s, w_ref[0], preferred_element_type=_F32) + b_ref[0]


def _modulation(cvec, ada_w, ada_b):
    n_col = 512
    return pl.pallas_call(
        _mod_kernel,
        grid=(DEPTH, 3 * D_MODEL // n_col),
        in_specs=[
            pl.BlockSpec((16, D_MODEL), lambda l, n: (0, 0)),
            pl.BlockSpec((1, D_MODEL, n_col), lambda l, n: (l, 0, n)),
            pl.BlockSpec((1, 1, n_col), lambda l, n: (l, 0, n)),
        ],
        out_specs=pl.BlockSpec((1, 16, n_col), lambda l, n: (l, 0, n)),
        out_shape=jax.ShapeDtypeStruct((DEPTH, 16, 3 * D_MODEL), _F32),
        compiler_params=_cparams("arbitrary", "arbitrary"),
        name="adaln_mod",
    )(cvec, ada_w, ada_b.reshape(DEPTH, 1, 3 * D_MODEL))


def _lane_tile(col_ref_val):
    return jnp.concatenate([col_ref_val] * (TOK // LANES), axis=1)


def _head_norm_rope(u, gain, cos, sin, rope):
    ms = jnp.sum(u * u, axis=0, keepdims=True) * (1.0 / HEAD_DIM)
    y = u * lax.rsqrt(ms + EPS) * gain
    if not rope:
        return y
    x1r, x2r, x1c, x2c = y[0:16], y[16:32], y[32:48], y[48:64]
    cr, cc = cos[0:16], cos[16:32]
    sr, sc = sin[0:16], sin[16:32]
    return jnp.concatenate(
        [x1r * cr - x2r * sr, x2r * cr + x1r * sr, x1c * cc - x2c * sc, x2c * cc + x1c * sc], axis=0)


def _proj_body(x, a_ref, sh_ref, w_ref, cos_ref, sin_ref, qg_ref, kg_ref,
               q_ref, k_ref, v_ref, g_ref, *, n_q, n_k, n_v, rope):
    ms = jnp.sum(x * x, axis=0, keepdims=True) * (1.0 / D_MODEL)
    h = x * lax.rsqrt(ms + EPS) * _lane_tile(a_ref[0]) + _lane_tile(sh_ref[0])
    hb = h.astype(_BF16)
    cos, sin = cos_ref[...], sin_ref[...]
    qg, kg = _lane_tile(qg_ref[...]), _lane_tile(kg_ref[...])

    uq = jnp.dot(w_ref[0:n_q, :], hb, preferred_element_type=_F32)
    for hd in range(n_q // HEAD_DIM):
        r = slice(hd * HEAD_DIM, (hd + 1) * HEAD_DIM)
        q_ref[0, 0, r, :] = _head_norm_rope(uq[r], qg, cos, sin, rope).astype(_BF16)

    uk = jnp.dot(w_ref[n_q:n_q + n_k, :], hb, preferred_element_type=_F32)
    for hd in range(n_k // HEAD_DIM):
        kh = _head_norm_rope(uk[hd * HEAD_DIM:(hd + 1) * HEAD_DIM], kg, cos, sin, rope)
        kt = jnp.concatenate([kh, jnp.zeros((LANES - HEAD_DIM, TOK), _F32)], axis=0)
        k_ref[0, 0, hd] = kt.T.astype(_BF16)

    o = n_q + n_k
    uv = jnp.dot(w_ref[o:o + n_v, :], hb, preferred_element_type=_F32)
    v_ref[0, 0] = uv.astype(_BF16)

    o = o + n_v
    uz = jnp.dot(w_ref[o:o + MIX_WIDTH, :], hb, preferred_element_type=_F32)
    g_ref[0] = (uz * jax.nn.sigmoid(uz)).astype(_BF16)


def _residual_update(o_ref, g_ref, w_ref, gt_ref, x_ref):
    og = (o_ref[0, 0].astype(_F32) * g_ref[0].astype(_F32)).astype(_BF16)
    y = jnp.dot(w_ref[...], og, preferred_element_type=_F32)
    return x_ref[0] + _lane_tile(gt_ref[0]) * y


def _first_kernel(ctx_ref, x_ref, *refs, **dims):
    xt_ref, proj_out = refs[7], refs[8:]
    x = jnp.where(pl.program_id(1) == 0, ctx_ref[0], x_ref[0]).T
    xt_ref[0] = x
    _proj_body(x, *refs[:7], *proj_out, **dims)


def _mid_kernel(o_ref, g_ref, wo_ref, gt_ref, x_ref, *refs, **dims):
    xt_ref, proj_out = refs[7], refs[8:]
    x = _residual_update(o_ref, g_ref, wo_ref, gt_ref, x_ref)
    xt_ref[0] = x
    _proj_body(x, *refs[:7], *proj_out, **dims)


def _last_kernel(o_ref, g_ref, wo_ref, gt_ref, x_ref, out_ref):
    out_ref[0] = _residual_update(o_ref, g_ref, wo_ref, gt_ref, x_ref).T


def _proj_specs(bsz, nt, n_in, n_q, n_k, n_v):
    col = lambda b, j: (jnp.where(j == 0, bsz, b), 0, 0)
    t = nt * TOK
    n_kh = n_k // HEAD_DIM
    in_specs = [
        pl.BlockSpec((1, D_MODEL, LANES), col),
        pl.BlockSpec((1, D_MODEL, LANES), col),
        pl.BlockSpec((n_in, D_MODEL), lambda b, j: (0, 0)),
        pl.BlockSpec((32, TOK), lambda b, j: (0, j)),
        pl.BlockSpec((32, TOK), lambda b, j: (0, j)),
        pl.BlockSpec((HEAD_DIM, LANES), lambda b, j: (0, 0)),
        pl.BlockSpec((HEAD_DIM, LANES), lambda b, j: (0, 0)),
    ]
    out_specs = [
        pl.BlockSpec((1, D_MODEL, TOK), lambda b, j: (b, 0, j)),
        pl.BlockSpec((1, 1, n_q, TOK), lambda b, j: (b, j, 0, 0)),
        pl.BlockSpec((1, 1, n_kh, TOK, LANES), lambda b, j: (b, j, 0, 0, 0)),
        pl.BlockSpec((1, 1, n_v, TOK), lambda b, j: (b, j, 0, 0)),
        pl.BlockSpec((1, MIX_WIDTH, TOK), lambda b, j: (b, 0, j)),
    ]
    out_shape = [
        jax.ShapeDtypeStruct((bsz, D_MODEL, t), _F32),
        jax.ShapeDtypeStruct((bsz, nt, n_q, TOK), _BF16),
        jax.ShapeDtypeStruct((bsz, nt, n_kh, TOK, LANES), _BF16),
        jax.ShapeDtypeStruct((bsz, nt, n_v, TOK), _BF16),
        jax.ShapeDtypeStruct((bsz, MIX_WIDTH, t), _BF16),
    ]
    return in_specs, out_specs, out_shape


def _first_proj(ctx, x, proj_args, *, n_q, n_k, n_v, rope):
    bsz, s, _ = x.shape
    nt = (ctx.shape[1] + s) // TOK
    in_specs, out_specs, out_shape = _proj_specs(bsz, nt, proj_args[2].shape[0], n_q, n_k, n_v)
    return pl.pallas_call(
        functools.partial(_first_kernel, n_q=n_q, n_k=n_k, n_v=n_v, rope=rope),
        grid=(bsz, nt),
        in_specs=[
            pl.BlockSpec((1, TOK, D_MODEL), lambda b, j: (b, 0, 0)),
            pl.BlockSpec((1, TOK, D_MODEL), lambda b, j: (b, jnp.maximum(j - 1, 0), 0)),
        ] + in_specs,
        out_specs=out_specs,
        out_shape=out_shape,
        compiler_params=_cparams("arbitrary", "arbitrary"),
        name="first_proj",
    )(ctx, x, *proj_args)


def _out_in_proj(ot, gz, wo_t, gtcol, xt, proj_args, *, n_q, n_k, n_v, rope):
    bsz, _, t = xt.shape
    nt = t // TOK
    in_specs, out_specs, out_shape = _proj_specs(bsz, nt, proj_args[2].shape[0], n_q, n_k, n_v)
    col = lambda b, j: (jnp.where(j == 0, bsz, b), 0, 0)
    tile = lambda b, j: (b, 0, j)
    return pl.pallas_call(
        functools.partial(_mid_kernel, n_q=n_q, n_k=n_k, n_v=n_v, rope=rope),
        grid=(bsz, nt),
        in_specs=[
            pl.BlockSpec((1, 1, MIX_WIDTH, TOK), lambda b, j: (b, j, 0, 0)),
            pl.BlockSpec((1, MIX_WIDTH, TOK), tile),
            pl.BlockSpec((D_MODEL, MIX_WIDTH), lambda b, j: (0, 0)),
            pl.BlockSpec((1, D_MODEL, LANES), col),
            pl.BlockSpec((1, D_MODEL, TOK), tile),
        ] + in_specs,
        out_specs=out_specs,
        out_shape=out_shape,
        input_output_aliases={4: 0},
        compiler_params=_cparams("arbitrary", "arbitrary"),
        name="out_in_proj",
    )(ot, gz, wo_t, gtcol, xt, *proj_args)


def _last_proj(ot, gz, wo_t, gtcol, xt):
    bsz, _, t = xt.shape
    n_lat = t // TOK - 1
    tile = lambda b, j: (b, 0, j + 1)
    return pl.pallas_call(
        _last_kernel,
        grid=(bsz, n_lat),
        in_specs=[
            pl.BlockSpec((1, 1, MIX_WIDTH, TOK), lambda b, j: (b, j + 1, 0, 0)),
            pl.BlockSpec((1, MIX_WIDTH, TOK), tile),
            pl.BlockSpec((D_MODEL, MIX_WIDTH), lambda b, j: (0, 0)),
            pl.BlockSpec((1, D_MODEL, LANES), lambda b, j: (b, 0, 0)),
            pl.BlockSpec((1, D_MODEL, TOK), tile),
        ],
        out_specs=pl.BlockSpec((1, TOK, D_MODEL), lambda b, j: (b, j, 0)),
        out_shape=jax.ShapeDtypeStruct((bsz, n_lat * TOK, D_MODEL), _F32),
        compiler_params=_cparams("arbitrary", "arbitrary"),
        name="last_proj",
    )(ot, gz, wo_t, gtcol, xt)


NCOL = SLOT * TOK
ONES_ROWS = 16


def _score_stage(k_tile, qcat_ref, s_ref, mx_ref, sb, rows=TOK, bias=None):
    if isinstance(k_tile, (list, tuple)):
        s = jnp.concatenate([jnp.dot(k[:, 0:HEAD_DIM], qcat_ref[:, j * TOK:(j + 1) * TOK],
                                     preferred_element_type=_F32)
                             for j, k in enumerate(k_tile)], axis=1)
    else:
        s = jnp.dot(k_tile[:, 0:HEAD_DIM], qcat_ref[...], preferred_element_type=_F32)
    if bias is not None:
        s = s + bias
    s_ref[sb, 0:rows] = s
    mx_ref[sb] = jnp.max(s, axis=0, keepdims=True)


def _softmax_value_stage(v_tile, s_ref, mx_ref, m_ref, acc_ref, sb, rows=TOK):
    m_old = m_ref[...]
    m_new = jnp.maximum(m_old, mx_ref[sb])
    alpha = jnp.exp2(m_old - m_new)
    m_ref[...] = m_new
    for j in range(SLOT):
        cols = slice(j * TOK, (j + 1) * TOK)
        v = v_tile(j)
        v_aug = jnp.concatenate([v, jnp.ones((ONES_ROWS, v.shape[1]), _BF16)], axis=0)
        p = jnp.exp2(s_ref[sb, 0:rows, cols] - m_new[:, cols]).astype(_BF16)
        acc_ref[:, cols] = (alpha[:, cols] * acc_ref[:, cols]
                            + jnp.dot(v_aug, p, preferred_element_type=_F32))


LAST_BUF = 2
GQA_KEY_GROUP = 1
DIFF_KEY_GROUP = 2


def _attend_tiles(n_t, build_qcat, k_tile, v_tile, finalize, qcat_ref, s_ref, mx_ref, m_ref, acc_ref):
    key_group = s_ref.shape[1] // TOK
    groups = [(0, 1)] + [(t, key_group) for t in range(1, n_t, key_group)]
    assert (n_t - 1) % key_group == 0
    n_g = len(groups)
    buf = lambda g: LAST_BUF if g == n_g - 1 else g % 2

    def score(g):
        t, n = groups[g]
        _score_stage(k_tile(t, n), qcat_ref, s_ref, mx_ref, buf(g), rows=n * TOK)

    def stage(g):
        t, n = groups[g]
        _softmax_value_stage(functools.partial(v_tile, t, n), s_ref, mx_ref, m_ref, acc_ref, buf(g), rows=n * TOK)

    def reset():
        m_ref[...] = jnp.full(m_ref.shape, NEG_INF, _F32)
        acc_ref[...] = jnp.zeros(acc_ref.shape, _F32)

    def start(j):
        build_qcat(j)
        score(0)

    reset()
    start(0)
    stage(0)
    finalize(0)
    reset()
    start(1)

    def tile(j, carry):
        for g in range(n_g - 1):
            score(g + 1)
            stage(g)
        stage(n_g - 1)
        finalize(j)
        reset()
        start(jnp.minimum(j + 1, n_t - 1))
        return carry

    lax.fori_loop(1, n_t, tile, 0)


def _attend_scratch(dv, key_group):
    return [
        pltpu.VMEM((2, 1, NCOL), _F32),
        pltpu.VMEM((HEAD_DIM, NCOL), _BF16),
        pltpu.VMEM((3, key_group * TOK, NCOL), _F32),
        pltpu.VMEM((4, 1, NCOL), _F32),
        pltpu.VMEM((dv + ONES_ROWS, NCOL), _F32),
    ]


def _gqa_kernel(q_ref, k_ref, v_ref, o_ref, m2_ref, qcat_ref, s_ref, mx_ref, acc_ref):
    m_ref = m2_ref.at[0]
    def build_qcat(j):
        for g in range(SLOT):
            qcat_ref[:, g * TOK:(g + 1) * TOK] = q_ref[0, j, g * HEAD_DIM:(g + 1) * HEAD_DIM, :]

    def finalize(j):
        o = acc_ref[0:HEAD_DIM, :] / acc_ref[HEAD_DIM:HEAD_DIM + 1, :]
        for g in range(SLOT):
            o_ref[0, j, g * HEAD_DIM:(g + 1) * HEAD_DIM, :] = o[:, g * TOK:(g + 1) * TOK].astype(_BF16)

    _attend_tiles(k_ref.shape[1], build_qcat,
                  lambda t, n: k_ref[0, t:t + n, 0].reshape(n * TOK, LANES),
                  lambda t, n, g: jnp.concatenate([v_ref[0, t + i] for i in range(n)], axis=1),
                  finalize, qcat_ref, s_ref, mx_ref, m_ref, acc_ref)


def _gqa_attention(qt, kt, vt):
    bsz, nt = qt.shape[:2]
    grp = SLOT * HEAD_DIM
    return pl.pallas_call(
        _gqa_kernel,
        grid=(bsz, A_KV_HEADS),
        in_specs=[
            pl.BlockSpec((1, nt, grp, TOK), lambda b, h: (b, 0, h, 0)),
            pl.BlockSpec((1, nt, 1, TOK, LANES), lambda b, h: (b, 0, h, 0, 0)),
            pl.BlockSpec((1, nt, HEAD_DIM, TOK), lambda b, h: (b, 0, h, 0)),
        ],
        out_specs=pl.BlockSpec((1, nt, grp, TOK), lambda b, h: (b, 0, h, 0)),
        out_shape=jax.ShapeDtypeStruct((bsz, nt, MIX_WIDTH, TOK), _BF16),
        scratch_shapes=_attend_scratch(HEAD_DIM, GQA_KEY_GROUP),
        compiler_params=_cparams("arbitrary", "arbitrary"),
        name="gqa_attn",
    )(qt, kt, vt)


def _diff_kernel(q_ref, k_ref, v_ref, lam_ref, sg_ref, o_ref, m2_ref, qcat_ref, s_ref, mx_ref, acc_ref,
                 *, out_scale, lam_init):
    m_ref = m2_ref.at[0]
    def build_qcat(j):
        for s in range(SLOT):
            qcat_ref[:, s * TOK:(s + 1) * TOK] = q_ref[0, j, s * HEAD_DIM:(s + 1) * HEAD_DIM, :]

    def finalize(j):
        lv = lam_ref[...]
        lam = (jnp.exp(jnp.sum(lv[0:1] * lv[1:2], axis=1, keepdims=True))
               - jnp.exp(jnp.sum(lv[2:3] * lv[3:4], axis=1, keepdims=True)) + lam_init)
        sg = _lane_tile(sg_ref[...])
        attn = acc_ref[0:C_V_DIM, :] / acc_ref[C_V_DIM:C_V_DIM + 1, :]
        for hh in range(2):
            o = attn[:, 2 * hh * TOK:(2 * hh + 1) * TOK] - lam * attn[:, (2 * hh + 1) * TOK:(2 * hh + 2) * TOK]
            ms = jnp.sum(o * o, axis=0, keepdims=True) * (1.0 / C_V_DIM)
            y = o * lax.rsqrt(ms + EPS) * sg * out_scale
            o_ref[0, j, hh * C_V_DIM:(hh + 1) * C_V_DIM, :] = y.astype(_BF16)

    head_rows = lambda s: slice((s // 2) * C_V_DIM, (s // 2 + 1) * C_V_DIM)
    _attend_tiles(k_ref.shape[1], build_qcat,
                  lambda t, n: [k_ref[0, t:t + n, s].reshape(n * TOK, LANES) for s in range(SLOT)],
                  lambda t, n, s: jnp.concatenate([v_ref[0, t + i, head_rows(s), :] for i in range(n)], axis=1),
                  finalize, qcat_ref, s_ref, mx_ref, m_ref, acc_ref)


def _diff_attention(qt, kt, vt, lamv, sg, lam_init):
    bsz, nt = qt.shape[:2]
    n_blk = kt.shape[2] // SLOT
    grp = SLOT * HEAD_DIM
    return pl.pallas_call(
        functools.partial(_diff_kernel, out_scale=1.0 - lam_init, lam_init=lam_init),
        grid=(bsz, n_blk),
        in_specs=[
            pl.BlockSpec((1, nt, grp, TOK), lambda b, c: (b, 0, c, 0)),
            pl.BlockSpec((1, nt, SLOT, TOK, LANES), lambda b, c: (b, 0, c, 0, 0)),
            pl.BlockSpec((1, nt, 2 * C_V_DIM, TOK), lambda b, c: (b, 0, c, 0)),
            pl.BlockSpec((8, LANES), lambda b, c: (0, 0)),
            pl.BlockSpec((C_V_DIM, LANES), lambda b, c: (0, 0)),
        ],
        out_specs=pl.BlockSpec((1, nt, 2 * C_V_DIM, TOK), lambda b, c: (b, 0, c, 0)),
        out_shape=jax.ShapeDtypeStruct((bsz, nt, MIX_WIDTH, TOK), _BF16),
        scratch_shapes=_attend_scratch(C_V_DIM, DIFF_KEY_GROUP),
        compiler_params=_cparams("arbitrary", "arbitrary"),
        name="diff_attn",
    )(qt, kt, vt, lamv, sg)


N_BAND = 3
NBR_QUADS_PER_TRIP = 4


def _nbr_kernel(q_ref, kc_ref, k0_ref, k1_ref, k2_ref, vc_ref, v0_ref, v1_ref, v2_ref, bias_ref, o_ref,
                m_ref, qcat_ref, s_ref, mx_ref, acc_ref):
    k_refs = (kc_ref, k0_ref, k1_ref, k2_ref)
    v_refs = (vc_ref, v0_ref, v1_ref, v2_ref)
    n_tiles = len(k_refs)
    n_quad = B_HEADS // SLOT
    n_slot = NBR_QUADS_PER_TRIP * n_tiles

    def trip(u, carry):
        quad = lambda i: NBR_QUADS_PER_TRIP * u + i // n_tiles
        par = lambda i: (i // n_tiles) % 2

        def head(i, g):
            return pl.ds(pl.multiple_of((quad(i) * SLOT + g) * HEAD_DIM, HEAD_DIM), HEAD_DIM)

        def score(i):
            c, t = quad(i), i % n_tiles
            if t == 0:
                for g in range(SLOT):
                    qcat_ref[par(i), :, g * TOK:(g + 1) * TOK] = q_ref[0, 0, head(i, g), :]
                m_ref[par(i)] = jnp.full(m_ref.shape[1:], NEG_INF, _F32)
                acc_ref[par(i)] = jnp.zeros(acc_ref.shape[1:], _F32)
            bias = None if t == 0 else bias_ref[0, c, t - 1]
            keys = [k_refs[t][0, 0, c * SLOT + g] for g in range(SLOT)]
            _score_stage(keys, qcat_ref.at[par(i)], s_ref, mx_ref, i % 2, bias=bias)

        score(0)
        for i in range(n_slot):
            t = i % n_tiles
            if i + 1 < n_slot:
                score(i + 1)
            acc = acc_ref.at[par(i)]
            _softmax_value_stage(lambda g, i=i, t=t: v_refs[t][0, 0, head(i, g), :],
                                 s_ref, mx_ref, m_ref.at[par(i)], acc, i % 2)
            if t == n_tiles - 1:
                o = acc[0:HEAD_DIM, :] / acc[HEAD_DIM:HEAD_DIM + 1, :]
                for g in range(SLOT):
                    o_ref[0, 0, head(i, g), :] = o[:, g * TOK:(g + 1) * TOK].astype(_BF16)
        return carry

    lax.fori_loop(0, n_quad // NBR_QUADS_PER_TRIP, trip, 0)


def _nbr_attention(qt, kt, vt, bias):
    bsz, nt = qt.shape[:2]
    n_lat = nt - 1
    tile = lambda j, b: (b, j, 0, 0)

    def band(d):
        return lambda j, b: (b, 1 + jnp.clip(j - 2, 0, n_lat - N_BAND) + d, 0, 0, 0)

    def vband(d):
        return lambda j, b: (b, 1 + jnp.clip(j - 2, 0, n_lat - N_BAND) + d, 0, 0)

    case = lambda j, b: (jnp.where(j == 0, 3, jnp.where(j == 1, 0, jnp.where(j == nt - 1, 2, 1))), 0, 0, 0, 0)
    kspec = lambda im: pl.BlockSpec((1, 1, B_HEADS, TOK, LANES), im)
    vspec = lambda im: pl.BlockSpec((1, 1, MIX_WIDTH, TOK), im)
    return pl.pallas_call(
        _nbr_kernel,
        grid=(nt, bsz),
        in_specs=[
            pl.BlockSpec((1, 1, MIX_WIDTH, TOK), tile),
            kspec(lambda j, b: (b, 0, 0, 0, 0)), kspec(band(0)), kspec(band(1)), kspec(band(2)),
            vspec(lambda j, b: (b, 0, 0, 0)), vspec(vband(0)), vspec(vband(1)), vspec(vband(2)),
            pl.BlockSpec((1, B_HEADS // SLOT, N_BAND, TOK, NCOL), case),
        ],
        out_specs=pl.BlockSpec((1, 1, MIX_WIDTH, TOK), tile),
        out_shape=jax.ShapeDtypeStruct((bsz, nt, MIX_WIDTH, TOK), _BF16),
        scratch_shapes=[
            pltpu.VMEM((2, 1, NCOL), _F32),
            pltpu.VMEM((2, HEAD_DIM, NCOL), _BF16),
            pltpu.VMEM((2, TOK, NCOL), _F32),
            pltpu.VMEM((4, 1, NCOL), _F32),
            pltpu.VMEM((2, HEAD_DIM + ONES_ROWS, NCOL), _F32),
        ],
        compiler_params=_cparams("arbitrary", "arbitrary"),
        name="nbr_attn",
    )(qt, kt, kt, kt, kt, vt, vt, vt, vt, bias)


def _nbr_bias(rpb):
    rows = 4096 // GRID_W
    q_per = TOK // GRID_W
    hi = lax.Precision.HIGHEST
    kc = np.arange(GRID_W)[:, None]
    qc = np.arange(GRID_W)[None, :]
    d_col = np.clip(kc - qc + WIN_W - 1, 0, 2 * WIN_W - 2)
    cs = np.clip(qc - WIN_W // 2, 0, GRID_W - WIN_W)
    col_in = (kc >= cs) & (kc < cs + WIN_W)
    e_col = (d_col[None] == np.arange(2 * WIN_W - 1)[:, None, None]).astype(np.float32)
    by_col = jnp.einsum("hdj,jkq->hdkq", rpb.astype(_F32), e_col, precision=hi)
    by_col = by_col.reshape(B_HEADS // SLOT, SLOT, 2 * WIN_H - 1, GRID_W, GRID_W)
    tables = []
    for i in (0, 1, rows // q_per - 1):
        s = min(max(i - 1, 0), rows // q_per - N_BAND)
        kr = (q_per * s + np.arange(N_BAND * q_per))[:, None]
        qr = (q_per * i + np.arange(q_per))[None, :]
        d_row = np.clip(kr - qr + WIN_H - 1, 0, 2 * WIN_H - 2)
        rs = np.clip(qr - WIN_H // 2, 0, rows - WIN_H)
        row_in = (kr >= rs) & (kr < rs + WIN_H)
        e_row = (d_row[None] == np.arange(2 * WIN_H - 1)[:, None, None]).astype(np.float32)
        tab = jnp.einsum("dab,cgdkq->cakgbq", e_row, by_col, precision=hi)
        inside = row_in[:, None, None, :, None] & col_in[None, :, None, None, :]
        tab = jnp.where(inside[None], tab * LOG2E, NEG_INF)
        tables.append(tab.reshape(B_HEADS // SLOT, N_BAND, TOK, NCOL))
    tables.append(jnp.full_like(tables[0], NEG_INF))
    return jnp.stack(tables, axis=0)


def _rope_tables(s):
    tok = jnp.arange(s, dtype=jnp.int32)
    rows = (tok // GRID_W).astype(_F32)
    cols = (tok % GRID_W).astype(_F32)
    n_freq = HEAD_DIM // 4
    inv_freq = ROPE_BASE ** (-jnp.arange(n_freq, dtype=_F32) / n_freq)
    ang = jnp.concatenate([rows[:, None] * inv_freq, cols[:, None] * inv_freq], axis=1)
    ang = jnp.concatenate([jnp.zeros((CTX_LEN, 32), _F32), ang], axis=0)
    return jnp.cos(ang).T, jnp.sin(ang).T


def _col(v):
    return jnp.broadcast_to(v[..., None], v.shape + (LANES,))


def _lambda_init(layer):
    return 0.8 - 0.6 * math.exp(-0.3 * layer)


def kernel(x, c, ctx, c_ctx, norm_g, ada_w, ada_b, a_w_in, a_q_g, a_k_g, a_w_out, b_w_in, b_q_g, b_k_g, b_rpb, b_w_out, c_w_in, c_q_g, c_k_g, c_lam_q1, c_lam_k1, c_lam_q2, c_lam_k2, c_subln_g, c_w_out):
    bsz, s, _ = x.shape
    assert c.shape[0] + 1 <= 16
    scale = HEAD_DIM ** -0.5

    cvec = jnp.zeros((16, D_MODEL), _F32).at[:bsz].set(c).at[bsz].set(c_ctx)
    mod = _modulation(cvec, ada_w, ada_b)[:, :bsz + 1]
    sh, sc, gt = jnp.split(mod, 3, axis=-1)
    acol = _col(norm_g[:, None, :] * (1.0 + sc))
    shcol = _col(sh)
    gtcol = _col(gt)

    cos_t, sin_t = _rope_tables(s)
    w_ins, w_outs, q_gs, k_gs = (a_w_in, b_w_in, c_w_in), (a_w_out, b_w_out, c_w_out), \
        (a_q_g, b_q_g, c_q_g), (a_k_g, b_k_g, c_k_g)
    n_kv = (A_KV_HEADS * HEAD_DIM, MIX_WIDTH, MIX_WIDTH)

    ot = zt = xt = wo_t = None
    for i in range(DEPTH):
        kind, j = i % 3, i // 3
        dims = dict(n_q=MIX_WIDTH, n_k=n_kv[kind], n_v=n_kv[kind], rope=kind != 1)
        proj_args = (acol[i], shcol[i], w_ins[kind][j].T.astype(_BF16), cos_t, sin_t,
                     _col(q_gs[kind][j] * (scale * LOG2E)), _col(k_gs[kind][j]))
        if i == 0:
            xt, qt, kt, vt, zt = _first_proj(ctx, x, proj_args, **dims)
        else:
            xt, qt, kt, vt, zt = _out_in_proj(ot, zt, wo_t, gtcol[i - 1], xt, proj_args, **dims)
        if kind == 0:
            ot = _gqa_attention(qt, kt, vt)
        elif kind == 1:
            ot = _nbr_attention(qt, kt, vt, _nbr_bias(b_rpb[j]))
        else:
            lamv = jnp.zeros((8, LANES), _F32).at[0:4, :HEAD_DIM].set(
                jnp.stack([c_lam_q1[j], c_lam_k1[j], c_lam_q2[j], c_lam_k2[j]]).astype(_F32))
            ot = _diff_attention(qt, kt, vt, lamv, _col(c_subln_g[j]), _lambda_init(i))
        wo_t = w_outs[kind][j].T.astype(_BF16)

    return _last_proj(ot, zt, wo_t, gtcol[DEPTH - 1], xt)
```
